```python
import jax, jax.numpy as jnp
from jax import lax
import numpy as np

D_MODEL = 1024
BATCH = 16
SEQ = 2048
DEPTH = 1

GRID_W = 64
CTX_LEN = 256
POOL_WIDTH = 512
POOL_WINDOWS = (2, 4, 8, 16)
POOL_GROUPS = len(POOL_WINDOWS)
POOL_GROUP_DIM = POOL_WIDTH // POOL_GROUPS
MLSTM_HEADS = 4
QK_HEAD_DIM = 64
V_HEAD_DIM = 128
QK_WIDTH = MLSTM_HEADS * QK_HEAD_DIM
MLSTM_WIDTH = MLSTM_HEADS * V_HEAD_DIM
N_GATES = 4 * MLSTM_HEADS
MIX_WIDTH = POOL_WIDTH + MLSTM_WIDTH
IN_WIDTH = POOL_WIDTH + 2 * QK_WIDTH + 2 * MLSTM_WIDTH + N_GATES
CONV_W = 3
CHUNK = 64
D_FF = -(-8 * D_MODEL // (3 * 256)) * 256
EPS = 1e-6

kernel_name = 'hybrid_pool_mlstm_dit_block'


def rmsnorm(x, g):
    xf = x.astype(jnp.float32)
    y = xf * lax.rsqrt(jnp.mean(xf * xf, axis=-1, keepdims=True) + EPS)
    return (y * g.astype(jnp.float32)).astype(x.dtype)


def modulate(h, shift, scale):
    return h * (1 + scale) + shift


def short_conv(u, w):
    pad = CONV_W // 2
    t = u.shape[1]
    up = jnp.pad(u, ((0, 0), (pad, pad), (0, 0)))
    return sum(w[j] * up[:, j:j + t] for j in range(CONV_W))


def pool_mixer(u, pool_w, pool_scale):
    length = u.shape[-2]
    pos = jnp.arange(length)
    uf = u.astype(jnp.float32)
    cs = jnp.cumsum(uf, axis=-2)
    cs = jnp.concatenate([jnp.zeros_like(cs[..., :1, :]), cs], axis=-2)
    outs = []
    for gi, win in enumerate(POOL_WINDOWS):
        lo = jnp.clip(pos - win // 2, 0, length - 1)
        hi = jnp.clip(pos + win // 2 - 1, 0, length - 1)
        sl = slice(gi * POOL_GROUP_DIM, (gi + 1) * POOL_GROUP_DIM)
        csg = cs[..., sl]
        s = jnp.take(csg, hi + 1, axis=-2) - jnp.take(csg, lo, axis=-2)
        cnt = (hi - lo + 1).astype(jnp.float32)[:, None]
        d = (s / cnt - uf[..., sl]).astype(u.dtype)
        outs.append(jnp.einsum('...lc,cd->...ld', d, pool_w[gi]))
    return jnp.concatenate(outs, axis=-1) * pool_scale


def mlstm_chunk_scan(q, k, v, i_pre, logf, state):
    bsz, nh, t, _ = q.shape
    dv = v.shape[-1]
    nc = t // CHUNK
    mask = jnp.tril(jnp.ones((CHUNK, CHUNK), dtype=bool))

    def to_chunks(a):
        a = a.reshape(a.shape[:2] + (nc, CHUNK) + a.shape[3:])
        return jnp.moveaxis(a, 2, 0)

    def step(carry, xs):
        c0, n0, m0 = carry
        qc, kc, vc, ic, fc = xs
        b = jnp.cumsum(fc, axis=-1)
        dmat = jnp.where(mask, b[..., :, None] - b[..., None, :] + ic[..., None, :], -jnp.inf)
        inter = b + m0[..., None]
        m = jnp.maximum(inter, jnp.max(dmat, axis=-1))
        s = jnp.einsum('bhjd,bhsd->bhjs', qc, kc) * jnp.exp(dmat - m[..., None])
        w_inter = jnp.exp(inter - m)
        num = jnp.einsum('bhjs,bhsv->bhjv', s, vc) + w_inter[..., None] * jnp.einsum('bhvd,bhjd->bhjv', c0, qc)
        den = jnp.sum(s, axis=-1) + w_inter * jnp.einsum('bhd,bhjd->bhj', n0, qc)
        h = num / jnp.maximum(jnp.abs(den), jnp.exp(-m))[..., None]
        bl = b[..., -1]
        wk = bl[..., None] - b + ic
        m_new = jnp.maximum(bl + m0, jnp.max(wk, axis=-1))
        decay = jnp.exp(bl + m0 - m_new)
        wk = jnp.exp(wk - m_new[..., None])
        c_new = decay[..., None, None] * c0 + jnp.einsum('bhs,bhsv,bhsd->bhvd', wk, vc, kc)
        n_new = decay[..., None] * n0 + jnp.einsum('bhs,bhsd->bhd', wk, kc)
        return (c_new, n_new, m_new), h

    xs = (to_chunks(q), to_chunks(k), to_chunks(v), to_chunks(i_pre), to_chunks(logf))
    state, hs = lax.scan(step, state, xs)
    hs = jnp.moveaxis(hs, 0, 2).reshape(bsz, nh, t, dv)
    return hs, state


def zero_state(bsz):
    return (jnp.zeros((bsz, MLSTM_HEADS, V_HEAD_DIM, QK_HEAD_DIM), jnp.float32),
            jnp.zeros((bsz, MLSTM_HEADS, QK_HEAD_DIM), jnp.float32),
            jnp.zeros((bsz, MLSTM_HEADS), jnp.float32))


def mlstm_bidir(q, k, v, gates, st_f, st_b):
    bsz, t, _ = gates.shape
    g = jnp.transpose(gates.astype(jnp.float32).reshape(bsz, t, 4, MLSTM_HEADS), (2, 0, 3, 1))
    i_f, i_b = g[0], g[1]
    lf_f, lf_b = jax.nn.log_sigmoid(g[2]), jax.nn.log_sigmoid(g[3])
    h_f, st_f = mlstm_chunk_scan(q, k, v, i_f, lf_f, st_f)
    fl = lambda a: jnp.flip(a, axis=2)
    h_b, st_b = mlstm_chunk_scan(fl(q), fl(k), fl(v), fl(i_b), fl(lf_b), st_b)
    return h_f + fl(h_b), st_f, st_b


def heads(a, d):
    bsz, t, _ = a.shape
    return jnp.transpose(a.reshape(bsz, t, MLSTM_HEADS, d), (0, 2, 1, 3)).astype(jnp.float32)


def project(h, w_in, conv_qk, gate_bias):
    p = h @ w_in
    o1 = POOL_WIDTH
    o2 = o1 + 2 * QK_WIDTH
    o3 = o2 + MLSTM_WIDTH
    o4 = o3 + MLSTM_WIDTH
    pool_in = p[..., :o1]
    qk = jax.nn.silu(short_conv(p[..., o1:o2], conv_qk))
    q = heads(qk[..., :QK_WIDTH], QK_HEAD_DIM) * (QK_HEAD_DIM ** -0.5)
    k = heads(qk[..., QK_WIDTH:], QK_HEAD_DIM)
    v = heads(p[..., o2:o3], V_HEAD_DIM)
    o = p[..., o3:o4]
    gates = p[..., o4:] + gate_bias
    return pool_in, q, k, v, o, gates


def mlstm_output(hm, o, head_norm):
    hn = hm * lax.rsqrt(jnp.mean(hm * hm, axis=-1, keepdims=True) + EPS)
    bsz, _, t, _ = hm.shape
    hn = jnp.transpose(hn, (0, 2, 1, 3)).reshape(bsz, t, MLSTM_WIDTH).astype(o.dtype)
    return hn * head_norm * jax.nn.sigmoid(o)


def swiglu(h, w_up, w_down):
    u = h @ w_up
    g, a = jnp.split(u, 2, axis=-1)
    return (jax.nn.silu(g) * a) @ w_down


def setup_inputs(seed: int = 0) -> dict:
    key = jax.random.key(seed)
    ks = jax.random.split(key, 20)
    f32 = jnp.float32
    nrm = lambda k, shape, s: jax.random.normal(k, shape, f32) * s
    gate_i = nrm(ks[7], (DEPTH, 2 * MLSTM_HEADS), 0.1)
    gate_f = 3.0 + 3.0 * jax.random.uniform(ks[8], (DEPTH, 2 * MLSTM_HEADS), f32)
    return {
        'x': nrm(ks[0], (BATCH, SEQ, D_MODEL), 1.0),
        'c': nrm(ks[1], (BATCH, D_MODEL), 1.0),
        'ctx': nrm(ks[2], (BATCH, CTX_LEN, D_MODEL), 1.0),
        'c_ctx': nrm(ks[3], (D_MODEL,), 1.0),
        'w_ada': nrm(ks[4], (DEPTH, D_MODEL, 6 * D_MODEL), D_MODEL ** -0.5),
        'b_ada': nrm(ks[5], (DEPTH, 6 * D_MODEL), 0.02),
        'norm1': 1.0 + nrm(ks[6], (DEPTH, D_MODEL), 0.02),
        'w_in': nrm(ks[9], (DEPTH, D_MODEL, IN_WIDTH), D_MODEL ** -0.5),
        'conv_qk': nrm(ks[10], (DEPTH, CONV_W, 2 * QK_WIDTH), CONV_W ** -0.5),
        'gate_bias': jnp.concatenate([gate_i, gate_f], axis=-1),
        'pool_w': nrm(ks[11], (DEPTH, POOL_GROUPS, POOL_GROUP_DIM, POOL_GROUP_DIM), POOL_GROUP_DIM ** -0.5),
        'pool_scale': 1.0 + nrm(ks[12], (DEPTH, POOL_WIDTH), 0.02),
        'head_norm': 1.0 + nrm(ks[13], (DEPTH, MLSTM_WIDTH), 0.02),
        'w_out': nrm(ks[14], (DEPTH, MIX_WIDTH, D_MODEL), MIX_WIDTH ** -0.5),
        'norm2': 1.0 + nrm(ks[15], (DEPTH, D_MODEL), 0.02),
        'w_up': nrm(ks[16], (DEPTH, D_MODEL, 2 * D_FF), D_MODEL ** -0.5),
        'w_down': nrm(ks[17], (DEPTH, D_FF, D_MODEL), D_FF ** -0.5),
        'norm_f': 1.0 + nrm(ks[18], (D_MODEL,), 0.02),
    }


def reference(x, c, ctx, c_ctx, w_ada, b_ada, norm1, w_in, conv_qk, gate_bias, pool_w,
              pool_scale, head_norm, w_out, norm2, w_up, w_down, norm_f):
    bsz, t, _ = x.shape
    rows = t // GRID_W
    xc = ctx
    for l in range(DEPTH):
        last = l == DEPTH - 1
        mod = (jax.nn.silu(c) @ w_ada[l] + b_ada[l])[:, None, :]
        sh1, sc1, g1, sh2, sc2, g2 = jnp.split(mod, 6, axis=-1)
        mod_c = jax.nn.silu(c_ctx) @ w_ada[l] + b_ada[l]
        csh1, csc1, cg1, csh2, csc2, cg2 = jnp.split(mod_c, 6, axis=-1)

        hx = modulate(rmsnorm(x, norm1[l]), sh1, sc1)
        hc = modulate(rmsnorm(xc, norm1[l]), csh1, csc1)
        px, qx, kx, vx, ox, gx = project(hx, w_in[l], conv_qk[l], gate_bias[l])
        pc, qc, kc, vc, oc, gc = project(hc, w_in[l], conv_qk[l], gate_bias[l])

        hc_m, st_f, st_b = mlstm_bidir(qc, kc, vc, gc, zero_state(bsz), zero_state(bsz))
        hx_m, _, _ = mlstm_bidir(qx, kx, vx, gx, st_f, st_b)

        pool_x = pool_mixer(px.reshape(bsz, rows, GRID_W, POOL_WIDTH), pool_w[l], pool_scale[l])
        pool_x = pool_x.reshape(bsz, t, POOL_WIDTH)
        mix_x = jnp.concatenate([pool_x, mlstm_output(hx_m, ox, head_norm[l])], axis=-1) @ w_out[l]
        x = x + g1 * mix_x
        x = x + g2 * swiglu(modulate(rmsnorm(x, norm2[l]), sh2, sc2), w_up[l], w_down[l])

        if not last:
            pool_c = pool_mixer(pc, pool_w[l], pool_scale[l])
            mix_c = jnp.concatenate([pool_c, mlstm_output(hc_m, oc, head_norm[l])], axis=-1) @ w_out[l]
            xc = xc + cg1 * mix_c
            xc = xc + cg2 * swiglu(modulate(rmsnorm(xc, norm2[l]), csh2, csc2), w_up[l], w_down[l])
    return rmsnorm(x, norm_f)
```

```python
import functools

import numpy as np
import jax
import jax.numpy as jnp
from jax import lax
from jax.experimental import pallas as pl
from jax.experimental.pallas import tpu as pltpu

F32 = jnp.float32
BF16 = jnp.bfloat16

EPS = 1e-6
GRID_W = 64
POOL_WINDOWS = (2, 4, 8, 16)
N_HEADS = 4
DK = 64
DV = 128
CONV_W = 3

V7X_VMEM_BYTES = 64 * 1024 * 1024
V7X_MXU_DIM = 256


def _vmem_limit(est_bytes):
    return int(min(V7X_VMEM_BYTES - 6 * 1024 * 1024, est_bytes))


def _dot(a, b):
    return jnp.dot(a, b, preferred_element_type=F32)


def _sigmoid(x):
    return 1.0 / (1.0 + jnp.exp(-x))


def _silu(x):
    return x * _sigmoid(x)


def _log_sigmoid(x):
    return jnp.minimum(x, 0.0) - jnp.log(1.0 + jnp.exp(-jnp.abs(x)))


def _rms(x):
    return x * lax.rsqrt(jnp.mean(x * x, axis=-1, keepdims=True) + EPS)


def _ada_kernel(c_ref, w_ref, b_ref, o_ref):
    s = _silu(c_ref[...]).astype(BF16)
    o_ref[...] = _dot(s, w_ref[...].astype(BF16)) + b_ref[...]


def _ada(cc, w_ada, b_ada):
    rows, d = cc.shape
    n = w_ada.shape[1]
    bn = 1536
    return pl.pallas_call(
        _ada_kernel,
        grid=(n // bn,),
        in_specs=[
            pl.BlockSpec((rows, d), lambda j: (0, 0)),
            pl.BlockSpec((d, bn), lambda j: (0, j)),
            pl.BlockSpec((1, bn), lambda j: (0, j)),
        ],
        out_specs=pl.BlockSpec((rows, bn), lambda j: (0, j)),
        out_shape=jax.ShapeDtypeStruct((rows, n), F32),
        compiler_params=pltpu.CompilerParams(
            dimension_semantics=("arbitrary",),
            vmem_limit_bytes=_vmem_limit(40 * 1024 * 1024)),
        name="ada",
    )(cc, w_ada, b_ada.reshape(1, n))


POOL_BLK = 256


def _pool_consts():
    pos = np.arange(POOL_BLK) % GRID_W
    row = np.arange(POOL_BLK) // GRID_W
    a = np.zeros((len(POOL_WINDOWS), POOL_BLK, POOL_BLK), np.float32)
    inv = np.zeros((POOL_BLK, len(POOL_WINDOWS)), np.float32)
    for gi, win in enumerate(POOL_WINDOWS):
        lo = np.clip(pos - win // 2, 0, GRID_W - 1)
        hi = np.clip(pos + win // 2 - 1, 0, GRID_W - 1)
        for j in range(POOL_BLK):
            a[gi, j, row[j] * GRID_W + lo[j]: row[j] * GRID_W + hi[j] + 1] = 1.0
        inv[:, gi] = 1.0 / (hi - lo + 1)
    return a, inv


def _proj_kernel(*refs, tm, with_pool):
    if with_pool:
        (x_ref, mod_ref, n1_ref, wp_ref, wqk_ref, wv_ref, wo_ref, wg_ref, gb_ref,
         a_ref, inv_ref, pw_ref, ps_ref,
         pool_ref, qk_ref, v_ref, o_ref, g_ref, gt_ref) = refs
    else:
        (x_ref, mod_ref, n1_ref, wqk_ref, wv_ref, wg_ref, gb_ref,
         qk_ref, v_ref, g_ref, gt_ref) = refs
    d = x_ref.shape[-1]
    x = x_ref[...]
    sh = mod_ref[0, :, 0:d]
    sc = mod_ref[0, :, d:2 * d]
    h = (_rms(x) * n1_ref[...]) * (1.0 + sc) + sh
    hb = h.astype(BF16)

    qk_ref[...] = _dot(hb, wqk_ref[...])
    v_ref[...] = _dot(hb, wv_ref[...]).astype(BF16)
    g = _dot(hb, wg_ref[...]) + gb_ref[...]
    g_ref[...] = g
    gt_ref[...] = g.T

    if with_pool:
        o_ref[...] = _dot(hb, wo_ref[...]).astype(BF16)
        u = _dot(hb, wp_ref[...])
        gd = u.shape[1] // len(POOL_WINDOWS)
        halves = []
        for r0 in range(0, tm, POOL_BLK):
            cols = []
            for gi in range(len(POOL_WINDOWS)):
                ug = u[r0:r0 + POOL_BLK, gi * gd:(gi + 1) * gd]
                win = _dot(a_ref[gi], ug.astype(BF16))
                cols.append(win * inv_ref[:, gi:gi + 1] - ug)
            halves.append(jnp.concatenate(cols, axis=1))
        dmat = jnp.concatenate(halves, axis=0).astype(BF16)
        half = dmat.shape[1] // 2
        out = jnp.concatenate(
            [_dot(dmat[:, :half], pw_ref[0]), _dot(dmat[:, half:], pw_ref[1])], axis=1)
        pool_ref[...] = (out * ps_ref[...]).astype(BF16)


def _in_proj(x2, mod3, mod_row_fn, norm1, w_parts, gate_bias, pool_parts, tm):
    n, d = x2.shape
    with_pool = pool_parts is not None
    w_pool, w_qk, w_v, w_o, w_g = w_parts
    ng = w_g.shape[1]
    const = lambda *shape: pl.BlockSpec(shape, lambda i: (0,) * len(shape))
    tok = lambda w: pl.BlockSpec((tm, w), lambda i: (i, 0))
    in_specs = [tok(d), pl.BlockSpec((1, 1, mod3.shape[-1]), lambda i: (mod_row_fn(i), 0, 0)),
                const(1, d)]
    args = [x2, mod3, norm1.reshape(1, d)]
    if with_pool:
        a_c, inv_c, pw_bd, p_scale = pool_parts
        in_specs += [const(*w_pool.shape), const(*w_qk.shape), const(*w_v.shape),
                     const(*w_o.shape), const(*w_g.shape), const(1, ng),
                     const(*a_c.shape), const(*inv_c.shape), const(*pw_bd.shape),
                     const(1, p_scale.shape[-1])]
        args += [w_pool, w_qk, w_v, w_o, w_g, gate_bias.reshape(1, ng),
                 a_c, inv_c, pw_bd, p_scale.reshape(1, -1)]
        out_specs = [tok(w_pool.shape[1]), tok(w_qk.shape[1]), tok(w_v.shape[1]),
                     tok(w_o.shape[1]), tok(ng), pl.BlockSpec((ng, tm), lambda i: (0, i))]
        out_shape = [jax.ShapeDtypeStruct((n, w_pool.shape[1]), BF16),
                     jax.ShapeDtypeStruct((n, w_qk.shape[1]), F32),
                     jax.ShapeDtypeStruct((n, w_v.shape[1]), BF16),
                     jax.ShapeDtypeStruct((n, w_o.shape[1]), BF16),
                     jax.ShapeDtypeStruct((n, ng), F32),
                     jax.ShapeDtypeStruct((ng, n), F32)]
    else:
        in_specs += [const(*w_qk.shape), const(*w_v.shape), const(*w_g.shape), const(1, ng)]
        args += [w_qk, w_v, w_g, gate_bias.reshape(1, ng)]
        out_specs = [tok(w_qk.shape[1]), tok(w_v.shape[1]), tok(ng),
                     pl.BlockSpec((ng, tm), lambda i: (0, i))]
        out_shape = [jax.ShapeDtypeStruct((n, w_qk.shape[1]), F32),
                     jax.ShapeDtypeStruct((n, w_v.shape[1]), BF16),
                     jax.ShapeDtypeStruct((n, ng), F32),
                     jax.ShapeDtypeStruct((ng, n), F32)]
    return pl.pallas_call(
        functools.partial(_proj_kernel, tm=tm, with_pool=with_pool),
        grid=(n // tm,),
        in_specs=in_specs,
        out_specs=out_specs,
        out_shape=out_shape,
        compiler_params=pltpu.CompilerParams(
            dimension_semantics=("arbitrary",),
            vmem_limit_bytes=_vmem_limit(48 * 1024 * 1024)),
        name="in_proj_pool" if with_pool else "in_proj_ctx",
    )(*args)


def _split3(x):
    hi = x.astype(BF16)
    r1 = x - hi.astype(F32)
    mid = r1.astype(BF16)
    lo = (r1 - mid.astype(F32)).astype(BF16)
    return hi, mid, lo


def _tri_dot_left(tri, x):
    return sum(_dot(tri, p) for p in _split3(x))


def _tri_dot_right(x, tri):
    return sum(_dot(p, tri) for p in _split3(x))


def _conv_silu_prologue(qk_ref, cw_ref, q_scr, kt_scr, t_len, blk, lc):
    nblk = t_len // blk
    w0 = cw_ref[0:1, :]
    w1 = cw_ref[1:2, :]
    w2 = cw_ref[2:3, :]
    qw = q_scr.shape[1]
    rows = lax.broadcasted_iota(jnp.int32, (blk, 1), 0)
    for bi in range(nblk):
        r0 = bi * blk
        cur = qk_ref[0, r0:r0 + blk, :]
        if bi > 0:
            prev_last = qk_ref[0, r0 - 1:r0, :]
        else:
            prev_last = jnp.zeros((1, cur.shape[1]), F32)
        if bi < nblk - 1:
            next_first = qk_ref[0, r0 + blk:r0 + blk + 1, :]
        else:
            next_first = jnp.zeros((1, cur.shape[1]), F32)
        dn = jnp.where(rows == 0, prev_last, pltpu.roll(cur, 1, axis=0))
        up = jnp.where(rows == blk - 1, next_first, pltpu.roll(cur, blk - 1, axis=0))
        act = _silu(w0 * dn + w1 * cur + w2 * up)
        q_scr[r0:r0 + blk, :] = (act[:, :qw] * (DK ** -0.5)).astype(BF16)
        kt = act[:, qw:].T
        for ci in range(blk // lc):
            kt_scr[(r0 // lc) + ci] = kt[:, ci * lc:(ci + 1) * lc].astype(BF16)


def _chunk_gates(g_col, g_row, di, tril, triu):
    nh = N_HEADS
    lf_col = _log_sigmoid(g_col[:, (2 + di) * nh:(3 + di) * nh])
    lf_row = _log_sigmoid(g_row[(2 + di) * nh:(3 + di) * nh, :])
    bcol = _tri_dot_left(triu if di else tril, lf_col)
    brow = _tri_dot_right(lf_row, tril if di else triu)
    arow = g_row[di * nh:(di + 1) * nh, :] - brow
    return bcol, arow


def _chain_step(q_c, kt_c, v_aug, bcol, arow, mask, last, ct_ref, m0, want_out):
    lc = q_c.shape[0]
    ct = ct_ref[...]
    if want_out:
        am = jnp.where(mask, arow, -jnp.inf)
        cm = jnp.max(am, axis=1, keepdims=True)
        mcol = jnp.maximum(m0, cm)
        e = jnp.exp(am - mcol)
        s = _dot(q_c, kt_c)
        p = (s * e).astype(BF16)
        m_last = mcol[last:last + 1, :]
        wk = e[last:last + 1, :]
    else:
        m_last = jnp.maximum(m0, jnp.max(arow, axis=1, keepdims=True))
        wk = jnp.exp(arow - m_last)
    kw = (kt_c.astype(F32) * wk).astype(BF16)
    decay = jnp.exp(m0 - m_last)
    bl = bcol[last:last + 1, :]
    m_new = bl + m_last
    if want_out:
        r = _dot(jnp.concatenate([p, kw], axis=0), v_aug)
        r2 = _dot(q_c, ct.astype(BF16))
        w_inter = jnp.exp(m0 - mcol)
        num = r[:lc, :DV] + w_inter * r2[:, :DV]
        den = r[:lc, DV:DV + 1] + w_inter * r2[:, DV:DV + 1]
        hout = num / jnp.maximum(jnp.abs(den), jnp.exp(-(bcol + mcol)))
        d_ct = r[lc:, :]
    else:
        hout = None
        d_ct = _dot(kw, v_aug)
    ct_ref[...] = decay * ct + d_ct
    return hout, m_new


def _mlstm_kernel(qkx_ref, vx_ref, ox_ref, gx_ref, gtx_ref,
                  qkc_ref, vc_ref, gc_ref, gtc_ref,
                  cw_ref, hn_ref, out_ref,
                  qx_scr, ktx_scr, qc_scr, ktc_scr, hf_scr, hb_scr, st_scr,
                  *, lc, t_x, t_c):
    nh = N_HEADS
    ri = lax.broadcasted_iota(jnp.int32, (lc, lc), 0)
    ci = lax.broadcasted_iota(jnp.int32, (lc, lc), 1)
    lower = ci <= ri
    upper = ci >= ri
    tril = lower.astype(BF16)
    triu = upper.astype(BF16)
    ones_col = (lax.broadcasted_iota(jnp.int32, (lc, DV), 1) == 0).astype(BF16)

    _conv_silu_prologue(qkc_ref, cw_ref, qc_scr, ktc_scr, t_c, min(t_c, 256), lc)
    _conv_silu_prologue(qkx_ref, cw_ref, qx_scr, ktx_scr, t_x, 256, lc)

    st_scr[...] = jnp.zeros(st_scr.shape, F32)

    def chunk_pair(cf, cb, q_scr, kt_scr, v_ref, g_ref, gt_ref, ms, want_out):
        new_ms = []
        for di, (cidx, mask, last) in enumerate(((cf, lower, lc - 1), (cb, upper, 0))):
            r0 = cidx * lc if isinstance(cidx, int) else pl.multiple_of(cidx * lc, lc)
            g_col = g_ref[0, pl.ds(r0, lc), :]
            g_row = gt_ref[0, cidx]
            bcol_all, arow_all = _chunk_gates(g_col, g_row, di, tril, triu)
            for h in range(nh):
                q_c = q_scr[pl.ds(r0, lc), h * DK:(h + 1) * DK]
                kt_c = kt_scr[cidx, h * DK:(h + 1) * DK, :]
                v_c = v_ref[0, pl.ds(r0, lc), h * DV:(h + 1) * DV]
                v_aug = jnp.concatenate([v_c, ones_col], axis=1)
                hout, m_new = _chain_step(
                    q_c, kt_c, v_aug, bcol_all[:, h:h + 1], arow_all[h:h + 1, :],
                    mask, last, st_scr.at[di * nh + h], ms[di * nh + h], want_out)
                new_ms.append(m_new)
                if want_out:
                    dst = hf_scr if di == 0 else hb_scr
                    dst[pl.ds(r0, lc), h * DV:(h + 1) * DV] = hout
        return new_ms

    ms = [jnp.zeros((1, 1), F32) for _ in range(2 * nh)]
    nc_c = t_c // lc
    for i in range(nc_c):
        ms = chunk_pair(i, nc_c - 1 - i, qc_scr, ktc_scr, vc_ref, gc_ref, gtc_ref, ms, False)

    nc_x = t_x // lc

    def body(i, carry):
        return tuple(chunk_pair(i, nc_x - 1 - i, qx_scr, ktx_scr, vx_ref, gx_ref, gtx_ref,
                                list(carry), True))

    lax.fori_loop(0, nc_x, body, tuple(ms))

    blk = 256
    for r0 in range(0, t_x, blk):
        hm = hf_scr[r0:r0 + blk, :] + hb_scr[r0:r0 + blk, :]
        parts = [_rms(hm[:, h * DV:(h + 1) * DV]) for h in range(nh)]
        hnorm = jnp.concatenate(parts, axis=1)
        gate = _sigmoid(ox_ref[0, r0:r0 + blk, :].astype(F32))
        out_ref[0, r0:r0 + blk, :] = (hnorm * hn_ref[...] * gate).astype(BF16)


def _mlstm(qkx, vx, ox, gx, gtx, qkc, vc, gc, gtc, conv_w, head_norm, lc):
    bsz, t_x, qkw = qkx.shape
    t_c = qkc.shape[1]
    vw = vx.shape[2]
    ng = gx.shape[2]
    bspec = lambda *shape: pl.BlockSpec((1,) + shape, lambda b: (b,) + (0,) * len(shape))
    const = lambda *shape: pl.BlockSpec(shape, lambda b: (0,) * len(shape))
    return pl.pallas_call(
        functools.partial(_mlstm_kernel, lc=lc, t_x=t_x, t_c=t_c),
        grid=(bsz,),
        in_specs=[bspec(t_x, qkw), bspec(t_x, vw), bspec(t_x, vw), bspec(t_x, ng),
                  bspec(t_x // lc, ng, lc),
                  bspec(t_c, qkw), bspec(t_c, vw), bspec(t_c, ng), bspec(t_c // lc, ng, lc),
                  const(CONV_W, qkw), const(1, vw)],
        out_specs=bspec(t_x, vw),
        out_shape=jax.ShapeDtypeStruct((bsz, t_x, vw), BF16),
        scratch_shapes=[
            pltpu.VMEM((t_x, qkw // 2), BF16),
            pltpu.VMEM((t_x // lc, qkw // 2, lc), BF16),
            pltpu.VMEM((t_c, qkw // 2), BF16),
            pltpu.VMEM((t_c // lc, qkw // 2, lc), BF16),
            pltpu.VMEM((t_x, vw), F32),
            pltpu.VMEM((t_x, vw), F32),
            pltpu.VMEM((2 * N_HEADS, DK, 2 * DV), F32),
        ],
        compiler_params=pltpu.CompilerParams(
            dimension_semantics=("arbitrary",),
            vmem_limit_bytes=_vmem_limit(52 * 1024 * 1024)),
        name="mlstm",
    )(qkx, vx, ox, gx, gtx, qkc, vc, gc, gtc, conv_w, head_norm.reshape(1, vw))


def _out_ffn_kernel(x_ref, pool_ref, ml_ref, mod_ref, wo_ref, n2_ref, wg_ref, wa_ref,
                    wd_ref, nf_ref, out_ref, *, ff_blk):
    d = x_ref.shape[-1]
    g1 = mod_ref[0, :, 2 * d:3 * d]
    sh2 = mod_ref[0, :, 3 * d:4 * d]
    sc2 = mod_ref[0, :, 4 * d:5 * d]
    g2 = mod_ref[0, :, 5 * d:6 * d]
    pw = pool_ref.shape[-1]
    mix = _dot(pool_ref[...], wo_ref[0:pw, :]) + _dot(ml_ref[...], wo_ref[pw:, :])
    x1 = x_ref[...] + g1 * mix
    h2 = ((_rms(x1) * n2_ref[...]) * (1.0 + sc2) + sh2).astype(BF16)
    dff = wg_ref.shape[1]
    acc = None
    for j0 in range(0, dff, ff_blk):
        gg = _dot(h2, wg_ref[:, j0:j0 + ff_blk])
        aa = _dot(h2, wa_ref[:, j0:j0 + ff_blk])
        act = (_silu(gg) * aa).astype(BF16)
        part = _dot(act, wd_ref[j0:j0 + ff_blk, :])
        acc = part if acc is None else acc + part
    x2 = x1 + g2 * acc
    out_ref[...] = _rms(x2) * nf_ref[...]


def _out_ffn(x2d, pool, ml, mod3, w_out, norm2, w_gate, w_act, w_down, norm_f, tm, tpb, ff_blk):
    n, d = x2d.shape
    dff = w_gate.shape[1]
    tok = lambda w: pl.BlockSpec((tm, w), lambda i: (i, 0))
    const = lambda *shape: pl.BlockSpec(shape, lambda i: (0,) * len(shape),
                                        pipeline_mode=pl.Buffered(1))
    return pl.pallas_call(
        functools.partial(_out_ffn_kernel, ff_blk=ff_blk),
        grid=(n // tm,),
        in_specs=[tok(d), tok(pool.shape[1]), tok(ml.shape[1]),
                  pl.BlockSpec((1, 1, mod3.shape[-1]), lambda i: (i // tpb, 0, 0)),
                  const(*w_out.shape), const(1, d), const(d, dff), const(d, dff),
                  const(dff, d), const(1, d)],
        out_specs=tok(d),
        out_shape=jax.ShapeDtypeStruct((n, d), F32),
        compiler_params=pltpu.CompilerParams(
            dimension_semantics=("arbitrary",),
            vmem_limit_bytes=_vmem_limit(56 * 1024 * 1024)),
        name="out_ffn",
    )(x2d, pool, ml, mod3, w_out, norm2.reshape(1, d), w_gate, w_act, w_down,
      norm_f.reshape(1, d))


MLSTM_CHUNK = 64
TOKEN_TILE = 512
FF_BLOCK = 1408


def kernel(x, c, ctx, c_ctx, w_ada, b_ada, norm1, w_in, conv_qk, gate_bias, pool_w,
           pool_scale, head_norm, w_out, norm2, w_up, w_down, norm_f):
    bsz, t, d = x.shape
    t_c = ctx.shape[1]
    assert w_ada.shape[0] == 1, "single-layer block"
    pool_width = pool_w.shape[1] * pool_w.shape[2]
    qk_width = 2 * N_HEADS * DK
    ml_width = N_HEADS * DV
    o1 = pool_width
    o2 = o1 + qk_width
    o3 = o2 + ml_width
    o4 = o3 + ml_width
    lc = MLSTM_CHUNK
    tm = TOKEN_TILE
    assert t % tm == 0 and (bsz * t_c) % tm == 0 and tm % POOL_BLK == 0

    cc = jnp.concatenate([c, c_ctx[None, :]], axis=0)
    mod = _ada(cc, w_ada[0], b_ada[0])
    mod3 = mod.reshape(bsz + 1, 1, mod.shape[-1])

    wi = w_in[0]
    w_parts = (wi[:, :o1].astype(BF16), wi[:, o1:o2].astype(BF16), wi[:, o2:o3].astype(BF16),
               wi[:, o3:o4].astype(BF16), wi[:, o4:].astype(BF16))
    a_np, inv_np = _pool_consts()
    gd = pool_w.shape[2]
    pw = pool_w[0].astype(BF16)
    zero = jnp.zeros((gd, gd), BF16)
    pw_bd = jnp.stack([jnp.block([[pw[0], zero], [zero, pw[1]]]),
                       jnp.block([[pw[2], zero], [zero, pw[3]]])])
    pool_parts = (jnp.asarray(a_np, BF16), jnp.asarray(inv_np, F32), pw_bd, pool_scale[0])

    tpb = t // tm
    pool_x, qk_x, v_x, o_x, g_x, gt_x = _in_proj(
        x.reshape(bsz * t, d), mod3, lambda i: i // tpb, norm1[0], w_parts, gate_bias[0],
        pool_parts, tm)
    qk_c, v_c, g_c, gt_c = _in_proj(
        ctx.reshape(bsz * t_c, d), mod3, lambda i: bsz, norm1[0], w_parts, gate_bias[0],
        None, tm)

    ng = g_x.shape[1]

    def chunked_rows(gt, tt):
        return jnp.transpose(gt.reshape(ng, bsz, tt // lc, lc), (1, 2, 0, 3))

    ml = _mlstm(qk_x.reshape(bsz, t, qk_width), v_x.reshape(bsz, t, ml_width),
                o_x.reshape(bsz, t, ml_width), g_x.reshape(bsz, t, ng), chunked_rows(gt_x, t),
                qk_c.reshape(bsz, t_c, qk_width), v_c.reshape(bsz, t_c, ml_width),
                g_c.reshape(bsz, t_c, ng), chunked_rows(gt_c, t_c),
                conv_qk[0], head_norm[0], lc)

    dff = w_down.shape[1]
    wu = w_up[0]
    out = _out_ffn(x.reshape(bsz * t, d), pool_x, ml.reshape(bsz * t, ml_width), mod3,
                   w_out[0].astype(BF16), norm2[0], wu[:, :dff].astype(BF16),
                   wu[:, dff:].astype(BF16), w_down[0].astype(BF16), norm_f, tm, tpb, FF_BLOCK)
    return out.reshape(bsz, t, d)
```

```python
import functools

import numpy as np
import jax
import jax.numpy as jnp
from jax import lax
from jax.experimental import pallas as pl
from jax.experimental.pallas import tpu as pltpu

F32 = jnp.float32
BF16 = jnp.bfloat16

EPS = 1e-6
GRID_W = 64
POOL_WINDOWS = (2, 4, 8, 16)
N_HEADS = 4
DK = 64
DV = 128
CONV_W = 3

V7X_VMEM_BYTES = 64 * 1024 * 1024
V7X_MXU_DIM = 256


def _vmem_limit(est_bytes):
    return int(min(V7X_VMEM_BYTES - 6 * 1024 * 1024, est_bytes))


def _dot(a, b):
    return jnp.dot(a, b, preferred_element_type=F32)


def _sigmoid(x):
    return 1.0 / (1.0 + jnp.exp(-x))


def _silu(x):
    return x * _sigmoid(x)


def _log_sigmoid(x):
    return jnp.minimum(x, 0.0) - jnp.log(1.0 + jnp.exp(-jnp.abs(x)))


def _rms(x):
    return x * lax.rsqrt(jnp.mean(x * x, axis=-1, keepdims=True) + EPS)


def _ada_kernel(c_ref, w_ref, b_ref, o_ref):
    s = _silu(c_ref[...]).astype(BF16)
    o_ref[...] = _dot(s, w_ref[...].astype(BF16)) + b_ref[...]


def _ada(cc, w_ada, b_ada):
    rows, d = cc.shape
    n = w_ada.shape[1]
    bn = 1536
    return pl.pallas_call(
        _ada_kernel,
        grid=(n // bn,),
        in_specs=[
            pl.BlockSpec((rows, d), lambda j: (0, 0)),
            pl.BlockSpec((d, bn), lambda j: (0, j)),
            pl.BlockSpec((1, bn), lambda j: (0, j)),
        ],
        out_specs=pl.BlockSpec((rows, bn), lambda j: (0, j)),
        out_shape=jax.ShapeDtypeStruct((rows, n), F32),
        compiler_params=pltpu.CompilerParams(
            dimension_semantics=("arbitrary",),
            vmem_limit_bytes=_vmem_limit(40 * 1024 * 1024)),
        name="ada",
    )(cc, w_ada, b_ada.reshape(1, n))


POOL_BLK = 256


def _pool_consts():
    pos = np.arange(POOL_BLK) % GRID_W
    row = np.arange(POOL_BLK) // GRID_W
    a = np.zeros((len(POOL_WINDOWS), POOL_BLK, POOL_BLK), np.float32)
    inv = np.zeros((POOL_BLK, len(POOL_WINDOWS)), np.float32)
    for gi, win in enumerate(POOL_WINDOWS):
        lo = np.clip(pos - win // 2, 0, GRID_W - 1)
        hi = np.clip(pos + win // 2 - 1, 0, GRID_W - 1)
        for j in range(POOL_BLK):
            a[gi, j, row[j] * GRID_W + lo[j]: row[j] * GRID_W + hi[j] + 1] = 1.0
        inv[:, gi] = 1.0 / (hi - lo + 1)
    return a, inv


def _proj_kernel(*refs, tm, with_pool):
    if with_pool:
        (x_ref, mod_ref, n1_ref, wp_ref, wqk_ref, wv_ref, wo_ref, wg_ref, gb_ref,
         a_ref, inv_ref, pw_ref, ps_ref,
         pool_ref, qk_ref, v_ref, o_ref, g_ref, gt_ref) = refs
    else:
        (x_ref, mod_ref, n1_ref, wqk_ref, wv_ref, wg_ref, gb_ref,
         qk_ref, v_ref, g_ref, gt_ref) = refs
    d = x_ref.shape[-1]
    x = x_ref[...]
    sh = mod_ref[0, :, 0:d]
    sc = mod_ref[0, :, d:2 * d]
    h = (_rms(x) * n1_ref[...]) * (1.0 + sc) + sh
    hb = h.astype(BF16)

    qk_ref[...] = _dot(hb, wqk_ref[...])
    v_ref[...] = _dot(hb, wv_ref[...]).astype(BF16)
    g = _dot(hb, wg_ref[...]) + gb_ref[...]
    g_ref[...] = g
    gt_ref[...] = g.T

    if with_pool:
        o_ref[...] = _dot(hb, wo_ref[...]).astype(BF16)
        u = _dot(hb, wp_ref[...])
        gd = u.shape[1] // len(POOL_WINDOWS)
        halves = []
        for r0 in range(0, tm, POOL_BLK):
            cols = []
            for gi in range(len(POOL_WINDOWS)):
                ug = u[r0:r0 + POOL_BLK, gi * gd:(gi + 1) * gd]
                win = _dot(a_ref[gi], ug.astype(BF16))
                cols.append(win * inv_ref[:, gi:gi + 1] - ug)
            halves.append(jnp.concatenate(cols, axis=1))
        dmat = jnp.concatenate(halves, axis=0).astype(BF16)
        half = dmat.shape[1] // 2
        out = jnp.concatenate(
            [_dot(dmat[:, :half], pw_ref[0]), _dot(dmat[:, half:], pw_ref[1])], axis=1)
        pool_ref[...] = (out * ps_ref[...]).astype(BF16)


def _in_proj(x2, mod3, mod_row_fn, norm1, w_parts, gate_bias, pool_parts, tm):
    n, d = x2.shape
    with_pool = pool_parts is not None
    w_pool, w_qk, w_v, w_o, w_g = w_parts
    ng = w_g.shape[1]
    const = lambda *shape: pl.BlockSpec(shape, lambda i: (0,) * len(shape))
    tok = lambda w: pl.BlockSpec((tm, w), lambda i: (i, 0))
    in_specs = [tok(d), pl.BlockSpec((1, 1, mod3.shape[-1]), lambda i: (mod_row_fn(i), 0, 0)),
                const(1, d)]
    args = [x2, mod3, norm1.reshape(1, d)]
    if with_pool:
        a_c, inv_c, pw_bd, p_scale = pool_parts
        in_specs += [const(*w_pool.shape), const(*w_qk.shape), const(*w_v.shape),
                     const(*w_o.shape), const(*w_g.shape), const(1, ng),
                     const(*a_c.shape), const(*inv_c.shape), const(*pw_bd.shape),
                     const(1, p_scale.shape[-1])]
        args += [w_pool, w_qk, w_v, w_o, w_g, gate_bias.reshape(1, ng),
                 a_c, inv_c, pw_bd, p_scale.reshape(1, -1)]
        out_specs = [tok(w_pool.shape[1]), tok(w_qk.shape[1]), tok(w_v.shape[1]),
                     tok(w_o.shape[1]), tok(ng), pl.BlockSpec((ng, tm), lambda i: (0, i))]
        out_shape = [jax.ShapeDtypeStruct((n, w_pool.shape[1]), BF16),
                     jax.ShapeDtypeStruct((n, w_qk.shape[1]), F32),
                     jax.ShapeDtypeStruct((n, w_v.shape[1]), BF16),
                     jax.ShapeDtypeStruct((n, w_o.shape[1]), BF16),
                     jax.ShapeDtypeStruct((n, ng), F32),
                     jax.ShapeDtypeStruct((ng, n), F32)]
    else:
        in_specs += [const(*w_qk.shape), const(*w_v.shape), const(*w_g.shape), const(1, ng)]
        args += [w_qk, w_v, w_g, gate_bias.reshape(1, ng)]
        out_specs = [tok(w_qk.shape[1]), tok(w_v.shape[1]), tok(ng),
                     pl.BlockSpec((ng, tm), lambda i: (0, i))]
        out_shape = [jax.ShapeDtypeStruct((n, w_qk.shape[1]), F32),
                     jax.ShapeDtypeStruct((n, w_v.shape[1]), BF16),
                     jax.ShapeDtypeStruct((n, ng), F32),
                     jax.ShapeDtypeStruct((ng, n), F32)]
    return pl.pallas_call(
        functools.partial(_proj_kernel, tm=tm, with_pool=with_pool),
        grid=(n // tm,),
        in_specs=in_specs,
        out_specs=out_specs,
        out_shape=out_shape,
        compiler_params=pltpu.CompilerParams(
            dimension_semantics=("arbitrary",),
            vmem_limit_bytes=_vmem_limit(48 * 1024 * 1024)),
        name="in_proj_pool" if with_pool else "in_proj_ctx",
    )(*args)


def _split3(x):
    hi = x.astype(BF16)
    r1 = x - hi.astype(F32)
    mid = r1.astype(BF16)
    lo = (r1 - mid.astype(F32)).astype(BF16)
    return hi, mid, lo


def _tri_dot_left(tri, x):
    return sum(_dot(tri, p) for p in _split3(x))


def _tri_dot_right(x, tri):
    return sum(_dot(p, tri) for p in _split3(x))


def _conv_silu_prologue(qk_ref, cw_ref, q_scr, kt_scr, t_len, blk, lc):
    nblk = t_len // blk
    w0 = cw_ref[0:1, :]
    w1 = cw_ref[1:2, :]
    w2 = cw_ref[2:3, :]
    qw = q_scr.shape[1]
    rows = lax.broadcasted_iota(jnp.int32, (blk, 1), 0)
    for bi in range(nblk):
        r0 = bi * blk
        cur = qk_ref[0, r0:r0 + blk, :]
        if bi > 0:
            prev_last = qk_ref[0, r0 - 1:r0, :]
        else:
            prev_last = jnp.zeros((1, cur.shape[1]), F32)
        if bi < nblk - 1:
            next_first = qk_ref[0, r0 + blk:r0 + blk + 1, :]
        else:
            next_first = jnp.zeros((1, cur.shape[1]), F32)
        dn = jnp.where(rows == 0, prev_last, pltpu.roll(cur, 1, axis=0))
        up = jnp.where(rows == blk - 1, next_first, pltpu.roll(cur, blk - 1, axis=0))
        act = _silu(w0 * dn + w1 * cur + w2 * up)
        q_scr[r0:r0 + blk, :] = (act[:, :qw] * (DK ** -0.5)).astype(BF16)
        kt = act[:, qw:].T
        for ci in range(blk // lc):
            kt_scr[(r0 // lc) + ci] = kt[:, ci * lc:(ci + 1) * lc].astype(BF16)


def _chunk_gates(g_col, g_row, di, tril, triu):
    nh = N_HEADS
    lf_col = _log_sigmoid(g_col[:, (2 + di) * nh:(3 + di) * nh])
    lf_row = _log_sigmoid(g_row[(2 + di) * nh:(3 + di) * nh, :])
    bcol = _tri_dot_left(triu if di else tril, lf_col)
    brow = _tri_dot_right(lf_row, tril if di else triu)
    arow = g_row[di * nh:(di + 1) * nh, :] - brow
    return bcol, arow


def _chain_step(q_c, kt_c, v_aug, bcol, arow, mask, last, ct_ref, m0, want_out):
    lc = q_c.shape[0]
    ct = ct_ref[...]
    if want_out:
        am = jnp.where(mask, arow, -jnp.inf)
        cm = jnp.max(am, axis=1, keepdims=True)
        mcol = jnp.maximum(m0, cm)
        e = jnp.exp(am - mcol)
        s = _dot(q_c, kt_c)
        p = (s * e).astype(BF16)
        m_last = mcol[last:last + 1, :]
        wk = e[last:last + 1, :]
    else:
        m_last = jnp.maximum(m0, jnp.max(arow, axis=1, keepdims=True))
        wk = jnp.exp(arow - m_last)
    kw = (kt_c.astype(F32) * wk).astype(BF16)
    decay = jnp.exp(m0 - m_last)
    bl = bcol[last:last + 1, :]
    m_new = bl + m_last
    if want_out:
        r = _dot(jnp.concatenate([p, kw], axis=0), v_aug)
        r2 = _dot(q_c, ct.astype(BF16))
        w_inter = jnp.exp(m0 - mcol)
        num = r[:lc, :DV] + w_inter * r2[:, :DV]
        den = r[:lc, DV:] + w_inter * r2[:, DV:]
        hout = num / jnp.maximum(jnp.abs(den), jnp.exp(-(bcol + mcol)))
        d_ct = r[lc:, :]
    else:
        hout = None
        d_ct = _dot(kw, v_aug)
    ct_ref[...] = decay * ct + d_ct
    return hout, m_new


def _mlstm_kernel(qkx_ref, vx_ref, ox_ref, gx_ref, gtx_ref,
                  qkc_ref, vc_ref, gc_ref, gtc_ref,
                  cw_ref, hn_ref, out_ref,
                  qx_scr, ktx_scr, qc_scr, ktc_scr, hf_scr, hb_scr, st_scr,
                  *, lc, t_x, t_c):
    nh = N_HEADS
    ri = lax.broadcasted_iota(jnp.int32, (lc, lc), 0)
    ci = lax.broadcasted_iota(jnp.int32, (lc, lc), 1)
    lower = ci <= ri
    upper = ci >= ri
    tril = lower.astype(BF16)
    triu = upper.astype(BF16)
    ones_blk = jnp.ones((lc, DV), BF16)

    _conv_silu_prologue(qkc_ref, cw_ref, qc_scr, ktc_scr, t_c, min(t_c, 256), lc)
    _conv_silu_prologue(qkx_ref, cw_ref, qx_scr, ktx_scr, t_x, 256, lc)

    st_scr[...] = jnp.zeros(st_scr.shape, F32)

    def chunk_pair(cf, cb, q_scr, kt_scr, v_ref, g_ref, gt_ref, ms, want_out):
        new_ms = []
        for di, (cidx, mask, last) in enumerate(((cf, lower, lc - 1), (cb, upper, 0))):
            r0 = cidx * lc if isinstance(cidx, int) else pl.multiple_of(cidx * lc, lc)
            g_col = g_ref[0, pl.ds(r0, lc), :]
            g_row = gt_ref[0, cidx]
            bcol_all, arow_all = _chunk_gates(g_col, g_row, di, tril, triu)
            for h in range(nh):
                q_c = q_scr[pl.ds(r0, lc), h * DK:(h + 1) * DK]
                kt_c = kt_scr[cidx, h * DK:(h + 1) * DK, :]
                v_c = v_ref[0, pl.ds(r0, lc), h * DV:(h + 1) * DV]
                v_aug = jnp.concatenate([v_c, ones_blk], axis=1)
                hout, m_new = _chain_step(
                    q_c, kt_c, v_aug, bcol_all[:, h:h + 1], arow_all[h:h + 1, :],
                    mask, last, st_scr.at[di * nh + h], ms[di * nh + h], want_out)
                new_ms.append(m_new)
                if want_out:
                    dst = hf_scr if di == 0 else hb_scr
                    dst[pl.ds(r0, lc), h * DV:(h + 1) * DV] = hout
        return new_ms

    ms = [jnp.zeros((1, 1), F32) for _ in range(2 * nh)]
    nc_c = t_c // lc
    for i in range(nc_c):
        ms = chunk_pair(i, nc_c - 1 - i, qc_scr, ktc_scr, vc_ref, gc_ref, gtc_ref, ms, False)

    nc_x = t_x // lc

    def body(i, carry):
        return tuple(chunk_pair(i, nc_x - 1 - i, qx_scr, ktx_scr, vx_ref, gx_ref, gtx_ref,
                                list(carry), True))

    lax.fori_loop(0, nc_x, body, tuple(ms))

    blk = 256
    for r0 in range(0, t_x, blk):
        hm = hf_scr[r0:r0 + blk, :] + hb_scr[r0:r0 + blk, :]
        parts = [_rms(hm[:, h * DV:(h + 1) * DV]) for h in range(nh)]
        hnorm = jnp.concatenate(parts, axis=1)
        gate = _sigmoid(ox_ref[0, r0:r0 + blk, :].astype(F32))
        out_ref[0, r0:r0 + blk, :] = (hnorm * hn_ref[...] * gate).astype(BF16)


def _mlstm(qkx, vx, ox, gx, gtx, qkc, vc, gc, gtc, conv_w, head_norm, lc):
    bsz, t_x, qkw = qkx.shape
    t_c = qkc.shape[1]
    vw = vx.shape[2]
    ng = gx.shape[2]
    bspec = lambda *shape: pl.BlockSpec((1,) + shape, lambda b: (b,) + (0,) * len(shape))
    const = lambda *shape: pl.BlockSpec(shape, lambda b: (0,) * len(shape))
    return pl.pallas_call(
        functools.partial(_mlstm_kernel, lc=lc, t_x=t_x, t_c=t_c),
        grid=(bsz,),
        in_specs=[bspec(t_x, qkw), bspec(t_x, vw), bspec(t_x, vw), bspec(t_x, ng),
                  bspec(t_x // lc, ng, lc),
                  bspec(t_c, qkw), bspec(t_c, vw), bspec(t_c, ng), bspec(t_c // lc, ng, lc),
                  const(CONV_W, qkw), const(1, vw)],
        out_specs=bspec(t_x, vw),
        out_shape=jax.ShapeDtypeStruct((bsz, t_x, vw), BF16),
        scratch_shapes=[
            pltpu.VMEM((t_x, qkw // 2), BF16),
            pltpu.VMEM((t_x // lc, qkw // 2, lc), BF16),
            pltpu.VMEM((t_c, qkw // 2), BF16),
            pltpu.VMEM((t_c // lc, qkw // 2, lc), BF16),
            pltpu.VMEM((t_x, vw), F32),
            pltpu.VMEM((t_x, vw), F32),
            pltpu.VMEM((2 * N_HEADS, DK, 2 * DV), F32),
        ],
        compiler_params=pltpu.CompilerParams(
            dimension_semantics=("arbitrary",),
            vmem_limit_bytes=_vmem_limit(52 * 1024 * 1024)),
        name="mlstm",
    )(qkx, vx, ox, gx, gtx, qkc, vc, gc, gtc, conv_w, head_norm.reshape(1, vw))


def _out_ffn_kernel(x_ref, pool_ref, ml_ref, mod_ref, wo_ref, n2_ref, wg_ref, wa_ref,
                    wd_ref, nf_ref, out_ref, *, ff_blk):
    d = x_ref.shape[-1]
    g1 = mod_ref[0, :, 2 * d:3 * d]
    sh2 = mod_ref[0, :, 3 * d:4 * d]
    sc2 = mod_ref[0, :, 4 * d:5 * d]
    g2 = mod_ref[0, :, 5 * d:6 * d]
    pw = pool_ref.shape[-1]
    mix = _dot(pool_ref[...], wo_ref[0:pw, :]) + _dot(ml_ref[...], wo_ref[pw:, :])
    x1 = x_ref[...] + g1 * mix
    h2 = ((_rms(x1) * n2_ref[...]) * (1.0 + sc2) + sh2).astype(BF16)
    dff = wg_ref.shape[1]
    acc = None
    for j0 in range(0, dff, ff_blk):
        gg = _dot(h2, wg_ref[:, j0:j0 + ff_blk])
        aa = _dot(h2, wa_ref[:, j0:j0 + ff_blk])
        act = (_silu(gg) * aa).astype(BF16)
        part = _dot(act, wd_ref[j0:j0 + ff_blk, :])
        acc = part if acc is None else acc + part
    x2 = x1 + g2 * acc
    out_ref[...] = _rms(x2) * nf_ref[...]


def _out_ffn(x2d, pool, ml, mod3, w_out, norm2, w_gate, w_act, w_down, norm_f, tm, tpb, ff_blk):
    n, d = x2d.shape
    dff = w_gate.shape[1]
    tok = lambda w: pl.BlockSpec((tm, w), lambda i: (i, 0))
    const = lambda *shape: pl.BlockSpec(shape, lambda i: (0,) * len(shape),
                                        pipeline_mode=pl.Buffered(1))
    return pl.pallas_call(
        functools.partial(_out_ffn_kernel, ff_blk=ff_blk),
        grid=(n // tm,),
        in_specs=[tok(d), tok(pool.shape[1]), tok(ml.shape[1]),
                  pl.BlockSpec((1, 1, mod3.shape[-1]), lambda i: (i // tpb, 0, 0)),
                  const(*w_out.shape), const(1, d), const(d, dff), const(d, dff),
                  const(dff, d), const(1, d)],
        out_specs=tok(d),
        out_shape=jax.ShapeDtypeStruct((n, d), F32),
        compiler_params=pltpu.CompilerParams(
            dimension_semantics=("arbitrary",),
            vmem_limit_bytes=_vmem_limit(56 * 1024 * 1024)),
        name="out_ffn",
    )(x2d, pool, ml, mod3, w_out, norm2.reshape(1, d), w_gate, w_act, w_down,
      norm_f.reshape(1, d))


MLSTM_CHUNK = 256
TOKEN_TILE = 512
FF_BLOCK = 1408


def kernel(x, c, ctx, c_ctx, w_ada, b_ada, norm1, w_in, conv_qk, gate_bias, pool_w,
           pool_scale, head_norm, w_out, norm2, w_up, w_down, norm_f):
    bsz, t, d = x.shape
    t_c = ctx.shape[1]
    assert w_ada.shape[0] == 1, "single-layer block"
    pool_width = pool_w.shape[1] * pool_w.shape[2]
    qk_width = 2 * N_HEADS * DK
    ml_width = N_HEADS * DV
    o1 = pool_width
    o2 = o1 + qk_width
    o3 = o2 + ml_width
    o4 = o3 + ml_width
    lc = MLSTM_CHUNK
    tm = TOKEN_TILE
    assert t % tm == 0 and (bsz * t_c) % tm == 0 and tm % POOL_BLK == 0

    cc = jnp.concatenate([c, c_ctx[None, :]], axis=0)
    mod = _ada(cc, w_ada[0], b_ada[0])
    mod3 = mod.reshape(bsz + 1, 1, mod.shape[-1])

    wi = w_in[0]
    w_parts = (wi[:, :o1].astype(BF16), wi[:, o1:o2].astype(BF16), wi[:, o2:o3].astype(BF16),
               wi[:, o3:o4].astype(BF16), wi[:, o4:].astype(BF16))
    a_np, inv_np = _pool_consts()
    gd = pool_w.shape[2]
    pw = pool_w[0].astype(BF16)
    zero = jnp.zeros((gd, gd), BF16)
    pw_bd = jnp.stack([jnp.block([[pw[0], zero], [zero, pw[1]]]),
                       jnp.block([[pw[2], zero], [zero, pw[3]]])])
    pool_parts = (jnp.asarray(a_np, BF16), jnp.asarray(inv_np, F32), pw_bd, pool_scale[0])

    tpb = t // tm
    pool_x, qk_x, v_x, o_x, g_x, gt_x = _in_proj(
        x.reshape(bsz * t, d), mod3, lambda i: i // tpb, norm1[0], w_parts, gate_bias[0],
        pool_parts, tm)
    qk_c, v_c, g_c, gt_c = _in_proj(
        ctx.reshape(bsz * t_c, d), mod3, lambda i: bsz, norm1[0], w_parts, gate_bias[0],
        None, tm)

    ng = g_x.shape[1]

    def chunked_rows(gt, tt):
        return jnp.transpose(gt.reshape(ng, bsz, tt // lc, lc), (1, 2, 0, 3))

    ml = _mlstm(qk_x.reshape(bsz, t, qk_width), v_x.reshape(bsz, t, ml_width),
                o_x.reshape(bsz, t, ml_width), g_x.reshape(bsz, t, ng), chunked_rows(gt_x, t),
                qk_c.reshape(bsz, t_c, qk_width), v_c.reshape(bsz, t_c, ml_width),
                g_c.reshape(bsz, t_c, ng), chunked_rows(gt_c, t_c),
                conv_qk[0], head_norm[0], lc)

    dff = w_down.shape[1]
    wu = w_up[0]
    out = _out_ffn(x.reshape(bsz * t, d), pool_x, ml.reshape(bsz * t, ml_width), mod3,
                   w_out[0].astype(BF16), norm2[0], wu[:, :dff].astype(BF16),
                   wu[:, dff:].astype(BF16), w_down[0].astype(BF16), norm_f, tm, tpb, FF_BLOCK)
    return out.reshape(bsz, t, d)
```

```python
import functools

import numpy as np
import jax
import jax.numpy as jnp
from jax import lax
from jax.experimental import pallas as pl
from jax.experimental.pallas import tpu as pltpu

F32 = jnp.float32
BF16 = jnp.bfloat16

EPS = 1e-6
GRID_W = 64
POOL_WINDOWS = (2, 4, 8, 16)
N_HEADS = 4
DK = 64
DV = 128
CONV_W = 3

V7X_VMEM_BYTES = 64 * 1024 * 1024
V7X_MXU_DIM = 256
BF16_SUBLANES = 16

MLSTM_CHUNK = V7X_MXU_DIM
TOKEN_TILE = 512
FF_BLOCK = 1408


def _vmem_limit(est_bytes):
    return int(min(V7X_VMEM_BYTES - 6 * 1024 * 1024, est_bytes))


def _dot(a, b):
    return jnp.dot(a, b, preferred_element_type=F32)


def _sigmoid(x):
    return 1.0 / (1.0 + jnp.exp(-x))


def _silu(x):
    return x * _sigmoid(x)


def _log_sigmoid(x):
    return jnp.minimum(x, 0.0) - jnp.log(1.0 + jnp.exp(-jnp.abs(x)))


def _rms(x):
    return x * lax.rsqrt(jnp.mean(x * x, axis=-1, keepdims=True) + EPS)


def _ada_kernel(c_ref, w_ref, b_ref, o_ref):
    s = _silu(c_ref[...]).astype(BF16)
    o_ref[...] = _dot(s, w_ref[...].astype(BF16)) + b_ref[...]


def _ada(cc, w_ada, b_ada):
    rows, d = cc.shape
    n = w_ada.shape[1]
    bn = 1536
    return pl.pallas_call(
        _ada_kernel,
        grid=(n // bn,),
        in_specs=[
            pl.BlockSpec((rows, d), lambda j: (0, 0)),
            pl.BlockSpec((d, bn), lambda j: (0, j)),
            pl.BlockSpec((1, bn), lambda j: (0, j)),
        ],
        out_specs=pl.BlockSpec((rows, bn), lambda j: (0, j)),
        out_shape=jax.ShapeDtypeStruct((rows, n), F32),
        compiler_params=pltpu.CompilerParams(
            dimension_semantics=("arbitrary",),
            vmem_limit_bytes=_vmem_limit(40 * 1024 * 1024)),
        name="ada",
    )(cc, w_ada, b_ada.reshape(1, n))


POOL_BLK = 256
HALO = BF16_SUBLANES


def _pool_consts():
    pos = np.arange(POOL_BLK) % GRID_W
    row = np.arange(POOL_BLK) // GRID_W
    a = np.zeros((len(POOL_WINDOWS), POOL_BLK, POOL_BLK), np.float32)
    inv = np.zeros((POOL_BLK, len(POOL_WINDOWS)), np.float32)
    for gi, win in enumerate(POOL_WINDOWS):
        lo = np.clip(pos - win // 2, 0, GRID_W - 1)
        hi = np.clip(pos + win // 2 - 1, 0, GRID_W - 1)
        for j in range(POOL_BLK):
            a[gi, j, row[j] * GRID_W + lo[j]: row[j] * GRID_W + hi[j] + 1] = 1.0
        inv[:, gi] = 1.0 / (hi - lo + 1)
    return a, inv


def _proj_kernel(*refs, tm, seq_len, lc, with_pool):
    if with_pool:
        (x_ref, xp_ref, xn_ref, mod_ref, n1_ref, wqk_ref, wv_ref, wg_ref, gb_ref, cw_ref,
         wp_ref, wo_ref, a_ref, inv_ref, pw_ref, ps_ref,
         q_ref, kt_ref, v_ref, g_ref, gt_ref, pool_ref, o_ref, lhs_scr) = refs
    else:
        (x_ref, xp_ref, xn_ref, mod_ref, n1_ref, wqk_ref, wv_ref, wg_ref, gb_ref, cw_ref,
         q_ref, kt_ref, v_ref, g_ref, gt_ref, lhs_scr) = refs
    d = x_ref.shape[-1]
    sh = mod_ref[0, :, 0:d]
    gain = n1_ref[...] * (1.0 + mod_ref[0, :, d:2 * d])

    def mod_norm(xv):
        return (_rms(xv) * gain + sh).astype(BF16)

    lhs_scr[0:tm, :] = mod_norm(x_ref[...])
    lhs_scr[tm:tm + HALO, :] = mod_norm(xp_ref[...])
    lhs_scr[tm + HALO:tm + 2 * HALO, :] = mod_norm(xn_ref[...])
    hb = lhs_scr[0:tm, :]

    p_ext = _dot(lhs_scr[...], wqk_ref[...])
    cur = p_ext[0:tm, :]
    prev_last = p_ext[tm + HALO - 1:tm + HALO, :]
    next_first = p_ext[tm + HALO:tm + HALO + 1, :]
    row = lax.broadcasted_iota(jnp.int32, (tm, 1), 0)
    pos = (pl.program_id(0) * tm + row) % seq_len
    dn = jnp.where(row == 0, prev_last, pltpu.roll(cur, 1, axis=0))
    dn = jnp.where(pos == 0, 0.0, dn)
    up = jnp.where(row == tm - 1, next_first, pltpu.roll(cur, tm - 1, axis=0))
    up = jnp.where(pos == seq_len - 1, 0.0, up)
    act = _silu(cw_ref[0:1, :] * dn + cw_ref[1:2, :] * cur + cw_ref[2:3, :] * up)
    qw = q_ref.shape[-1]
    q_ref[...] = (act[:, :qw] * (DK ** -0.5)).astype(BF16)
    kt = act[:, qw:].T
    for ci in range(tm // lc):
        kt_ref[ci] = kt[:, ci * lc:(ci + 1) * lc].astype(BF16)

    v_ref[...] = _dot(hb, wv_ref[...]).astype(BF16)
    g = _dot(hb, wg_ref[...]) + gb_ref[...]
    g_ref[...] = g
    gt = g.T
    for ci in range(tm // lc):
        gt_ref[ci] = gt[:, ci * lc:(ci + 1) * lc]

    if with_pool:
        o_ref[...] = _dot(hb, wo_ref[...]).astype(BF16)
        u = _dot(hb, wp_ref[...])
        gd = u.shape[1] // len(POOL_WINDOWS)
        halves = []
        for r0 in range(0, tm, POOL_BLK):
            cols = []
            for gi in range(len(POOL_WINDOWS)):
                ug = u[r0:r0 + POOL_BLK, gi * gd:(gi + 1) * gd]
                win = _dot(a_ref[gi], ug.astype(BF16))
                cols.append(win * inv_ref[:, gi:gi + 1] - ug)
            halves.append(jnp.concatenate(cols, axis=1))
        dmat = jnp.concatenate(halves, axis=0).astype(BF16)
        half = dmat.shape[1] // 2
        out = jnp.concatenate(
            [_dot(dmat[:, :half], pw_ref[0]), _dot(dmat[:, half:], pw_ref[1])], axis=1)
        pool_ref[...] = (out * ps_ref[...]).astype(BF16)


def _in_proj(x2, seq_len, mod3, mod_row_fn, norm1, w_parts, gate_bias, conv_w, pool_parts,
             tm, lc):
    n, d = x2.shape
    with_pool = pool_parts is not None
    w_pool, w_qk, w_v, w_o, w_g = w_parts
    ng = w_g.shape[1]
    qkw = w_qk.shape[1]
    hpt = tm // HALO
    const = lambda *shape: pl.BlockSpec(shape, lambda i: (0,) * len(shape))
    tok = lambda w: pl.BlockSpec((tm, w), lambda i: (i, 0))
    chunked = lambda r: pl.BlockSpec((tm // lc, r, lc), lambda i: (i, 0, 0))
    in_specs = [tok(d),
                pl.BlockSpec((HALO, d), lambda i: (jnp.maximum(i * hpt - 1, 0), 0)),
                pl.BlockSpec((HALO, d), lambda i: (jnp.minimum((i + 1) * hpt, n // HALO - 1), 0)),
                pl.BlockSpec((1, 1, mod3.shape[-1]), lambda i: (mod_row_fn(i), 0, 0)),
                const(1, d), const(*w_qk.shape), const(*w_v.shape), const(*w_g.shape),
                const(1, ng), const(*conv_w.shape)]
    args = [x2, x2, x2, mod3, norm1.reshape(1, d), w_qk, w_v, w_g, gate_bias.reshape(1, ng),
            conv_w]
    out_specs = [tok(qkw // 2), chunked(qkw // 2), tok(w_v.shape[1]), tok(ng), chunked(ng)]
    out_shape = [jax.ShapeDtypeStruct((n, qkw // 2), BF16),
                 jax.ShapeDtypeStruct((n // lc, qkw // 2, lc), BF16),
                 jax.ShapeDtypeStruct((n, w_v.shape[1]), BF16),
                 jax.ShapeDtypeStruct((n, ng), F32),
                 jax.ShapeDtypeStruct((n // lc, ng, lc), F32)]
    if with_pool:
        a_c, inv_c, pw_bd, p_scale = pool_parts
        in_specs += [const(*w_pool.shape), const(*w_o.shape), const(*a_c.shape),
                     const(*inv_c.shape), const(*pw_bd.shape), const(1, p_scale.shape[-1])]
        args += [w_pool, w_o, a_c, inv_c, pw_bd, p_scale.reshape(1, -1)]
        out_specs += [tok(w_pool.shape[1]), tok(w_o.shape[1])]
        out_shape += [jax.ShapeDtypeStruct((n, w_pool.shape[1]), BF16),
                      jax.ShapeDtypeStruct((n, w_o.shape[1]), BF16)]
    return pl.pallas_call(
        functools.partial(_proj_kernel, tm=tm, seq_len=seq_len, lc=lc, with_pool=with_pool),
        grid=(n // tm,),
        in_specs=in_specs,
        out_specs=out_specs,
        out_shape=out_shape,
        scratch_shapes=[pltpu.VMEM((tm + 2 * HALO, d), BF16)],
        compiler_params=pltpu.CompilerParams(
            dimension_semantics=("arbitrary",),
            vmem_limit_bytes=_vmem_limit(48 * 1024 * 1024)),
        name="in_proj_pool" if with_pool else "in_proj_ctx",
    )(*args)


def _split3(x):
    hi = x.astype(BF16)
    r1 = x - hi.astype(F32)
    mid = r1.astype(BF16)
    lo = (r1 - mid.astype(F32)).astype(BF16)
    return hi, mid, lo


def _tri_dot_left(tri, x):
    return sum(_dot(tri, p) for p in _split3(x))


def _tri_dot_right(x, tri):
    return sum(_dot(p, tri) for p in _split3(x))


def _gate_tables(gp_ref, gt_ref, tril, triu):
    g_row = gt_ref[0]
    lf_row = _log_sigmoid(g_row)
    lf_col = _log_sigmoid(gp_ref[0])
    brow = (_tri_dot_right(lf_row, triu), _tri_dot_right(lf_row, tril))
    bcol = (_tri_dot_left(tril, lf_col), _tri_dot_left(triu, lf_col))
    return g_row, brow, bcol


def _chain_step(q_c, kt_c, v_aug, bcol, arow, mask, last, ct_ref, m0, want_out):
    lc = kt_c.shape[1]
    if want_out:
        am = jnp.where(mask, arow, -jnp.inf)
        cm = jnp.max(am, axis=1, keepdims=True)
        e = jnp.exp(am - cm)
        p = (_dot(q_c, kt_c) * e).astype(BF16)
        amax = cm[last:last + 1, :]
        wk = e[last:last + 1, :]
    else:
        amax = jnp.max(arow, axis=1, keepdims=True)
        wk = jnp.exp(arow - amax)
    kw = (kt_c.astype(F32) * wk).astype(BF16)
    if want_out:
        r = _dot(jnp.concatenate([p, kw], axis=0), v_aug)
        d_ct = r[lc:, :]
    else:
        d_ct = _dot(kw, v_aug)

    ct = ct_ref[...]
    m_last = jnp.maximum(m0, amax)
    if want_out:
        r2 = _dot(q_c, ct.astype(BF16))
        mcol = jnp.maximum(m0, cm)
        nd = jnp.exp(cm - mcol) * r[:lc, :] + jnp.exp(m0 - mcol) * r2
        hout = nd[:, :DV] / jnp.maximum(jnp.abs(nd[:, DV:]), jnp.exp(-(bcol + mcol)))
    else:
        hout = None
    ct_ref[...] = jnp.exp(m0 - m_last) * ct + jnp.exp(amax - m_last) * d_ct
    bl = bcol[last:last + 1, :]
    return hout, bl + m_last


def _mlstm_kernel(qx_ref, ktx_ref, vx_ref, ox_ref, gpx_ref, gtx_ref,
                  qc_ref, ktc_ref, vc_ref, gpc_ref, gtc_ref, hn_ref, out_ref,
                  hf_scr, hb_scr, st_scr, *, lc, t_x, t_c):
    nh = N_HEADS
    ng = 4 * nh
    ri = lax.broadcasted_iota(jnp.int32, (lc, lc), 0)
    ci = lax.broadcasted_iota(jnp.int32, (lc, lc), 1)
    lower = ci <= ri
    upper = ci >= ri
    tril = lower.astype(BF16)
    triu = upper.astype(BF16)
    ones_blk = jnp.ones((lc, DV), BF16)
    dirs = ((lower, lc - 1), (upper, 0))

    st_scr[...] = jnp.zeros(st_scr.shape, F32)

    def chunk_pair(cf, cb, q_ref, kt_ref, v_ref, tables, ms, want_out):
        g_row, brow, bcol = tables
        new_ms = []
        for di, cidx in enumerate((cf, cb)):
            mask, last = dirs[di]
            r0 = cidx * lc
            for h in range(nh):
                col_i = cidx * ng + di * nh + h
                col_f = col_i + 2 * nh
                arow = g_row[col_i:col_i + 1, :] - brow[di][col_f:col_f + 1, :]
                q_c = q_ref[0, r0:r0 + lc, h * DK:(h + 1) * DK]
                kt_c = kt_ref[0, cidx, h * DK:(h + 1) * DK, :]
                v_c = v_ref[0, r0:r0 + lc, h * DV:(h + 1) * DV]
                v_aug = jnp.concatenate([v_c, ones_blk], axis=1)
                hout, m_new = _chain_step(
                    q_c, kt_c, v_aug, bcol[di][:, col_f:col_f + 1], arow, mask, last,
                    st_scr.at[di * nh + h], ms[di * nh + h], want_out)
                new_ms.append(m_new)
                if want_out:
                    dst = hf_scr if di == 0 else hb_scr
                    dst[r0:r0 + lc, h * DV:(h + 1) * DV] = hout
        return new_ms

    ms = [jnp.zeros((1, 1), F32) for _ in range(2 * nh)]
    nc_c = t_c // lc
    tables_c = _gate_tables(gpc_ref, gtc_ref, tril, triu)
    for i in range(nc_c):
        ms = chunk_pair(i, nc_c - 1 - i, qc_ref, ktc_ref, vc_ref, tables_c, ms, False)
    nc_x = t_x // lc
    tables_x = _gate_tables(gpx_ref, gtx_ref, tril, triu)
    for i in range(nc_x):
        ms = chunk_pair(i, nc_x - 1 - i, qx_ref, ktx_ref, vx_ref, tables_x, ms, True)

    blk = 256
    for r0 in range(0, t_x, blk):
        hm = hf_scr[r0:r0 + blk, :] + hb_scr[r0:r0 + blk, :]
        parts = [_rms(hm[:, h * DV:(h + 1) * DV]) for h in range(nh)]
        hnorm = jnp.concatenate(parts, axis=1)
        gate = _sigmoid(ox_ref[0, r0:r0 + blk, :].astype(F32))
        out_ref[0, r0:r0 + blk, :] = (hnorm * hn_ref[...] * gate).astype(BF16)


def _mlstm(qx, ktx, vx, ox, gpx, gtx, qc, ktc, vc, gpc, gtc, head_norm, lc):
    bsz, t_x, qw = qx.shape
    t_c = qc.shape[1]
    vw = vx.shape[2]
    bspec = lambda a: pl.BlockSpec((1,) + a.shape[1:], lambda b: (b,) + (0,) * (a.ndim - 1))
    args = (qx, ktx, vx, ox, gpx, gtx, qc, ktc, vc, gpc, gtc)
    return pl.pallas_call(
        functools.partial(_mlstm_kernel, lc=lc, t_x=t_x, t_c=t_c),
        grid=(bsz,),
        in_specs=[bspec(a) for a in args] + [pl.BlockSpec((1, vw), lambda b: (0, 0))],
        out_specs=pl.BlockSpec((1, t_x, vw), lambda b: (b, 0, 0)),
        out_shape=jax.ShapeDtypeStruct((bsz, t_x, vw), BF16),
        scratch_shapes=[
            pltpu.VMEM((t_x, vw), F32),
            pltpu.VMEM((t_x, vw), F32),
            pltpu.VMEM((2 * N_HEADS, DK, 2 * DV), F32),
        ],
        compiler_params=pltpu.CompilerParams(
            dimension_semantics=("arbitrary",),
            vmem_limit_bytes=_vmem_limit(52 * 1024 * 1024)),
        name="mlstm",
    )(*args, head_norm.reshape(1, vw))


def _out_ffn_kernel(x_ref, pool_ref, ml_ref, mod_ref, wo_ref, n2_ref, wg_ref, wa_ref,
                    wd_ref, nf_ref, out_ref, *, ff_blk):
    d = x_ref.shape[-1]
    g1 = mod_ref[0, :, 2 * d:3 * d]
    sh2 = mod_ref[0, :, 3 * d:4 * d]
    sc2 = mod_ref[0, :, 4 * d:5 * d]
    g2 = mod_ref[0, :, 5 * d:6 * d]
    pw = pool_ref.shape[-1]
    mix = _dot(pool_ref[...], wo_ref[0:pw, :]) + _dot(ml_ref[...], wo_ref[pw:, :])
    x1 = x_ref[...] + g1 * mix
    h2 = ((_rms(x1) * n2_ref[...]) * (1.0 + sc2) + sh2).astype(BF16)
    dff = wg_ref.shape[1]
    acc = None
    for j0 in range(0, dff, ff_blk):
        gg = _dot(h2, wg_ref[:, j0:j0 + ff_blk])
        aa = _dot(h2, wa_ref[:, j0:j0 + ff_blk])
        act = (_silu(gg) * aa).astype(BF16)
        part = _dot(act, wd_ref[j0:j0 + ff_blk, :])
        acc = part if acc is None else acc + part
    x2 = x1 + g2 * acc
    out_ref[...] = _rms(x2) * nf_ref[...]


def _out_ffn(x2d, pool, ml, mod3, w_out, norm2, w_gate, w_act, w_down, norm_f, tm, tpb, ff_blk):
    n, d = x2d.shape
    dff = w_gate.shape[1]
    tok = lambda w: pl.BlockSpec((tm, w), lambda i: (i, 0))
    const = lambda *shape: pl.BlockSpec(shape, lambda i: (0,) * len(shape),
                                        pipeline_mode=pl.Buffered(1))
    return pl.pallas_call(
        functools.partial(_out_ffn_kernel, ff_blk=ff_blk),
        grid=(n // tm,),
        in_specs=[tok(d), tok(pool.shape[1]), tok(ml.shape[1]),
                  pl.BlockSpec((1, 1, mod3.shape[-1]), lambda i: (i // tpb, 0, 0)),
                  const(*w_out.shape), const(1, d), const(d, dff), const(d, dff),
                  const(dff, d), const(1, d)],
        out_specs=tok(d),
        out_shape=jax.ShapeDtypeStruct((n, d), F32),
        compiler_params=pltpu.CompilerParams(
            dimension_semantics=("arbitrary",),
            vmem_limit_bytes=_vmem_limit(56 * 1024 * 1024)),
        name="out_ffn",
    )(x2d, pool, ml, mod3, w_out, norm2.reshape(1, d), w_gate, w_act, w_down,
      norm_f.reshape(1, d))


def kernel(x, c, ctx, c_ctx, w_ada, b_ada, norm1, w_in, conv_qk, gate_bias, pool_w,
           pool_scale, head_norm, w_out, norm2, w_up, w_down, norm_f):
    bsz, t, d = x.shape
    t_c = ctx.shape[1]
    assert w_ada.shape[0] == 1, "single-layer block"
    pool_width = pool_w.shape[1] * pool_w.shape[2]
    qk_width = 2 * N_HEADS * DK
    ml_width = N_HEADS * DV
    o1 = pool_width
    o2 = o1 + qk_width
    o3 = o2 + ml_width
    o4 = o3 + ml_width
    lc = MLSTM_CHUNK
    tm = TOKEN_TILE
    assert t % tm == 0 and (bsz * t_c) % tm == 0 and tm % POOL_BLK == 0
    assert tm % lc == 0 and t_c % lc == 0 and (tm % t_c == 0 or t_c % tm == 0)

    cc = jnp.concatenate([c, c_ctx[None, :]], axis=0)
    mod = _ada(cc, w_ada[0], b_ada[0])
    mod3 = mod.reshape(bsz + 1, 1, mod.shape[-1])

    wi = w_in[0]
    w_parts = (wi[:, :o1].astype(BF16), wi[:, o1:o2].astype(BF16), wi[:, o2:o3].astype(BF16),
               wi[:, o3:o4].astype(BF16), wi[:, o4:].astype(BF16))
    a_np, inv_np = _pool_consts()
    gd = pool_w.shape[2]
    pw = pool_w[0].astype(BF16)
    zero = jnp.zeros((gd, gd), BF16)
    pw_bd = jnp.stack([jnp.block([[pw[0], zero], [zero, pw[1]]]),
                       jnp.block([[pw[2], zero], [zero, pw[3]]])])
    pool_parts = (jnp.asarray(a_np, BF16), jnp.asarray(inv_np, F32), pw_bd, pool_scale[0])

    tpb = t // tm
    q_x, kt_x, v_x, g_x, gt_x, pool_x, o_x = _in_proj(
        x.reshape(bsz * t, d), t, mod3, lambda i: i // tpb, norm1[0], w_parts, gate_bias[0],
        conv_qk[0], pool_parts, tm, lc)
    q_c, kt_c, v_c, g_c, gt_c = _in_proj(
        ctx.reshape(bsz * t_c, d), t_c, mod3, lambda i: bsz, norm1[0], w_parts, gate_bias[0],
        conv_qk[0], None, tm, lc)

    ng = g_x.shape[1]

    def per_seq(a, tt):
        return a.reshape((bsz, tt) + a.shape[1:])

    def chunk_cols(g, tt):
        nc = tt // lc
        return jnp.transpose(g.reshape(bsz, nc, lc, ng), (0, 2, 1, 3)).reshape(bsz, lc, nc * ng)

    def chunk_rows(gt, tt):
        nc = tt // lc
        return gt.reshape(bsz, nc * ng, lc)

    hq = qk_width // 2
    ml = _mlstm(per_seq(q_x, t), kt_x.reshape(bsz, t // lc, hq, lc), per_seq(v_x, t),
                per_seq(o_x, t), chunk_cols(g_x, t), chunk_rows(gt_x, t),
                per_seq(q_c, t_c), kt_c.reshape(bsz, t_c // lc, hq, lc), per_seq(v_c, t_c),
                chunk_cols(g_c, t_c), chunk_rows(gt_c, t_c), head_norm[0], lc)

    dff = w_down.shape[1]
    wu = w_up[0]
    out = _out_ffn(x.reshape(bsz * t, d), pool_x, ml.reshape(bsz * t, ml_width), mod3,
                   w_out[0].astype(BF16), norm2[0], wu[:, :dff].astype(BF16),
                   wu[:, dff:].astype(BF16), w_down[0].astype(BF16), norm_f, tm, tpb, FF_BLOCK)
    return out.reshape(bsz, t, d)
```

```python
import functools

import numpy as np
import jax
import jax.numpy as jnp
from jax import lax
from jax.experimental import pallas as pl
from jax.experimental.pallas import tpu as pltpu

F32 = jnp.float32
BF16 = jnp.bfloat16

EPS = 1e-6
GRID_W = 64
POOL_WINDOWS = (2, 4, 8, 16)
N_HEADS = 4
DK = 64
DV = 128
CONV_W = 3

V7X_VMEM_BYTES = 64 * 1024 * 1024
V7X_MXU_DIM = 256
BF16_SUBLANES = 16

MLSTM_CHUNK = V7X_MXU_DIM
TOKEN_TILE = 512
OUT_TILE = 1024
OUT_SUB = 256


def _vmem_limit(est_bytes):
    return int(min(V7X_VMEM_BYTES - 6 * 1024 * 1024, est_bytes))


def _dot(a, b):
    return jnp.dot(a, b, preferred_element_type=F32)


def _sigmoid(x):
    return 1.0 / (1.0 + jnp.exp(-x))


def _silu(x):
    return x * _sigmoid(x)


def _log_sigmoid(x):
    return jnp.minimum(x, 0.0) - jnp.log(1.0 + jnp.exp(-jnp.abs(x)))


def _rms(x):
    return x * lax.rsqrt(jnp.mean(x * x, axis=-1, keepdims=True) + EPS)


def _ada_kernel(c_ref, w_ref, b_ref, o_ref):
    s = _silu(c_ref[...]).astype(BF16)
    o_ref[...] = _dot(s, w_ref[...].astype(BF16)) + b_ref[...]


def _ada(cc, w_ada, b_ada):
    rows, d = cc.shape
    n = w_ada.shape[1]
    bn = 1536
    return pl.pallas_call(
        _ada_kernel,
        grid=(n // bn,),
        in_specs=[
            pl.BlockSpec((rows, d), lambda j: (0, 0)),
            pl.BlockSpec((d, bn), lambda j: (0, j)),
            pl.BlockSpec((1, bn), lambda j: (0, j)),
        ],
        out_specs=pl.BlockSpec((rows, bn), lambda j: (0, j)),
        out_shape=jax.ShapeDtypeStruct((rows, n), F32),
        compiler_params=pltpu.CompilerParams(
            dimension_semantics=("arbitrary",),
            vmem_limit_bytes=_vmem_limit(40 * 1024 * 1024)),
        name="ada",
    )(cc, w_ada, b_ada.reshape(1, n))


POOL_BLK = 256
HALO = BF16_SUBLANES


def _pool_consts():
    pos = np.arange(POOL_BLK) % GRID_W
    row = np.arange(POOL_BLK) // GRID_W
    a = np.zeros((len(POOL_WINDOWS), POOL_BLK, POOL_BLK), np.float32)
    inv = np.zeros((POOL_BLK, len(POOL_WINDOWS)), np.float32)
    for gi, win in enumerate(POOL_WINDOWS):
        lo = np.clip(pos - win // 2, 0, GRID_W - 1)
        hi = np.clip(pos + win // 2 - 1, 0, GRID_W - 1)
        for j in range(POOL_BLK):
            a[gi, j, row[j] * GRID_W + lo[j]: row[j] * GRID_W + hi[j] + 1] = 1.0
        inv[:, gi] = 1.0 / (hi - lo + 1)
    return a, inv


def _proj_kernel(*refs, tm, seq_len, lc, with_pool):
    if with_pool:
        (x_ref, xp_ref, xn_ref, mod_ref, n1_ref, wqk_ref, wv_ref, wg_ref, gb_ref, cw_ref,
         wp_ref, wo_ref, a_ref, inv_ref, pw_ref, ps_ref,
         q_ref, kt_ref, v_ref, g_ref, gt_ref, pool_ref, o_ref, lhs_scr) = refs
    else:
        (x_ref, xp_ref, xn_ref, mod_ref, n1_ref, wqk_ref, wv_ref, wg_ref, gb_ref, cw_ref,
         q_ref, kt_ref, v_ref, g_ref, gt_ref, lhs_scr) = refs
    d = x_ref.shape[-1]
    sub = lc
    assert tm == 2 * sub and sub == POOL_BLK
    sh = mod_ref[0, :, 0:d]
    gain = n1_ref[...] * (1.0 + mod_ref[0, :, d:2 * d])
    qw = q_ref.shape[-1]
    row = lax.broadcasted_iota(jnp.int32, (sub, 1), 0)

    def mod_norm(xv):
        return (_rms(xv) * gain + sh).astype(BF16)

    def conv_act(cur, prev_last, next_first, r0, ci):
        start = pl.program_id(0) * tm + r0
        prev_last = jnp.where(start % seq_len == 0, 0.0, prev_last)
        next_first = jnp.where((start + sub) % seq_len == 0, 0.0, next_first)
        dn = jnp.where(row == 0, prev_last, pltpu.roll(cur, 1, axis=0))
        up = jnp.where(row == sub - 1, next_first, pltpu.roll(cur, sub - 1, axis=0))
        act = _silu(cw_ref[0:1, :] * dn + cw_ref[1:2, :] * cur + cw_ref[2:3, :] * up)
        q_ref[r0:r0 + sub, :] = (act[:, :qw] * (DK ** -0.5)).astype(BF16)
        kt_ref[ci] = act[:, qw:].T.astype(BF16)

    def side_dots(hb, r0, ci):
        v_ref[r0:r0 + sub, :] = _dot(hb, wv_ref[...]).astype(BF16)
        g = _dot(hb, wg_ref[...]) + gb_ref[...]
        g_ref[r0:r0 + sub, :] = g
        gt_ref[ci] = g.T
        if with_pool:
            o_ref[r0:r0 + sub, :] = _dot(hb, wo_ref[...]).astype(BF16)
            return _dot(hb, wp_ref[...])
        return None

    def pool_mix(u, r0):
        gd = u.shape[1] // len(POOL_WINDOWS)
        cols = []
        for gi in range(len(POOL_WINDOWS)):
            ug = u[:, gi * gd:(gi + 1) * gd]
            win = _dot(a_ref[gi], ug.astype(BF16))
            cols.append(win * inv_ref[:, gi:gi + 1] - ug)
        dmat = jnp.concatenate(cols, axis=1).astype(BF16)
        half = dmat.shape[1] // 2
        out = jnp.concatenate(
            [_dot(dmat[:, :half], pw_ref[0]), _dot(dmat[:, half:], pw_ref[1])], axis=1)
        pool_ref[r0:r0 + sub, :] = (out * ps_ref[...]).astype(BF16)

    lhs_scr[0:sub, :] = mod_norm(x_ref[0:sub, :])
    hb_a = lhs_scr[0:sub, :]
    p_a = _dot(hb_a, wqk_ref[...])
    u_a = side_dots(hb_a, 0, 0)
    lhs_scr[sub:tm, :] = mod_norm(x_ref[sub:tm, :])
    lhs_scr[tm:tm + HALO, :] = mod_norm(xp_ref[...])
    lhs_scr[tm + HALO:tm + 2 * HALO, :] = mod_norm(xn_ref[...])
    p_b = _dot(lhs_scr[sub:tm + 2 * HALO, :], wqk_ref[...])
    conv_act(p_a, p_b[sub + HALO - 1:sub + HALO, :], p_b[0:1, :], 0, 0)
    hb_b = lhs_scr[sub:tm, :]
    u_b = side_dots(hb_b, sub, 1)
    if with_pool:
        pool_mix(u_a, 0)
    conv_act(p_b[0:sub, :], p_a[sub - 1:sub, :], p_b[sub + HALO:sub + HALO + 1, :], sub, 1)
    if with_pool:
        pool_mix(u_b, sub)


def _in_proj(x2, seq_len, mod3, mod_row_fn, norm1, w_parts, gate_bias, conv_w, pool_parts,
             tm, lc):
    n, d = x2.shape
    with_pool = pool_parts is not None
    w_pool, w_qk, w_v, w_o, w_g = w_parts
    ng = w_g.shape[1]
    qkw = w_qk.shape[1]
    hpt = tm // HALO
    const = lambda *shape: pl.BlockSpec(shape, lambda i: (0,) * len(shape))
    tok = lambda w: pl.BlockSpec((tm, w), lambda i: (i, 0))
    chunked = lambda r: pl.BlockSpec((tm // lc, r, lc), lambda i: (i, 0, 0))
    in_specs = [tok(d),
                pl.BlockSpec((HALO, d), lambda i: (jnp.maximum(i * hpt - 1, 0), 0)),
                pl.BlockSpec((HALO, d), lambda i: (jnp.minimum((i + 1) * hpt, n // HALO - 1), 0)),
                pl.BlockSpec((1, 1, mod3.shape[-1]), lambda i: (mod_row_fn(i), 0, 0)),
                const(1, d), const(*w_qk.shape), const(*w_v.shape), const(*w_g.shape),
                const(1, ng), const(*conv_w.shape)]
    args = [x2, x2, x2, mod3, norm1.reshape(1, d), w_qk, w_v, w_g, gate_bias.reshape(1, ng),
            conv_w]
    out_specs = [tok(qkw // 2), chunked(qkw // 2), tok(w_v.shape[1]), tok(ng), chunked(ng)]
    out_shape = [jax.ShapeDtypeStruct((n, qkw // 2), BF16),
                 jax.ShapeDtypeStruct((n // lc, qkw // 2, lc), BF16),
                 jax.ShapeDtypeStruct((n, w_v.shape[1]), BF16),
                 jax.ShapeDtypeStruct((n, ng), F32),
                 jax.ShapeDtypeStruct((n // lc, ng, lc), F32)]
    if with_pool:
        a_c, inv_c, pw_bd, p_scale = pool_parts
        in_specs += [const(*w_pool.shape), const(*w_o.shape), const(*a_c.shape),
                     const(*inv_c.shape), const(*pw_bd.shape), const(1, p_scale.shape[-1])]
        args += [w_pool, w_o, a_c, inv_c, pw_bd, p_scale.reshape(1, -1)]
        out_specs += [tok(w_pool.shape[1]), tok(w_o.shape[1])]
        out_shape += [jax.ShapeDtypeStruct((n, w_pool.shape[1]), BF16),
                      jax.ShapeDtypeStruct((n, w_o.shape[1]), BF16)]
    return pl.pallas_call(
        functools.partial(_proj_kernel, tm=tm, seq_len=seq_len, lc=lc, with_pool=with_pool),
        grid=(n // tm,),
        in_specs=in_specs,
        out_specs=out_specs,
        out_shape=out_shape,
        scratch_shapes=[pltpu.VMEM((tm + 2 * HALO, d), BF16)],
        compiler_params=pltpu.CompilerParams(
            dimension_semantics=("arbitrary",),
            vmem_limit_bytes=_vmem_limit(48 * 1024 * 1024)),
        name="in_proj_pool" if with_pool else "in_proj_ctx",
    )(*args)


def _split3(x):
    hi = x.astype(BF16)
    r1 = x - hi.astype(F32)
    mid = r1.astype(BF16)
    lo = (r1 - mid.astype(F32)).astype(BF16)
    return hi, mid, lo


def _tri_dot_left(tri, x):
    return sum(_dot(tri, p) for p in _split3(x))


def _tri_dot_right(x, tri):
    return sum(_dot(p, tri) for p in _split3(x))


def _gate_tables(gp_ref, gt_ref, tril, triu):
    g_row = gt_ref[0]
    lf_row = _log_sigmoid(g_row)
    lf_col = _log_sigmoid(gp_ref[0])
    brow = (_tri_dot_right(lf_row, triu), _tri_dot_right(lf_row, tril))
    bcol = (_tri_dot_left(tril, lf_col), _tri_dot_left(triu, lf_col))
    return g_row, brow, bcol


def _chain_step(q_c, kt_c, v_aug, bcol, arow, mask, last, ct_ref, m0, want_out):
    lc = kt_c.shape[1]
    if want_out:
        am = jnp.where(mask, arow, -jnp.inf)
        cm = jnp.max(am, axis=1, keepdims=True)
        e = jnp.exp(am - cm)
        p = (_dot(q_c, kt_c) * e).astype(BF16)
        amax = cm[last:last + 1, :]
        wk = e[last:last + 1, :]
    else:
        amax = jnp.max(arow, axis=1, keepdims=True)
        wk = jnp.exp(arow - amax)
    kw = (kt_c.astype(F32) * wk).astype(BF16)
    if want_out:
        r = _dot(jnp.concatenate([p, kw], axis=0), v_aug)
        d_ct = r[lc:, :]
    else:
        d_ct = _dot(kw, v_aug)

    ct = ct_ref[...]
    m_last = jnp.maximum(m0, amax)
    if want_out:
        r2 = _dot(q_c, ct.astype(BF16))
        mcol = jnp.maximum(m0, cm)
        nd = jnp.exp(cm - mcol) * r[:lc, :] + jnp.exp(m0 - mcol) * r2
        hout = nd[:, :DV] / jnp.maximum(jnp.abs(nd[:, DV:]), jnp.exp(-(bcol + mcol)))
    else:
        hout = None
    ct_ref[...] = jnp.exp(m0 - m_last) * ct + jnp.exp(amax - m_last) * d_ct
    bl = bcol[last:last + 1, :]
    return hout, bl + m_last


def _mlstm_kernel(qx_ref, ktx_ref, vx_ref, ox_ref, gpx_ref, gtx_ref,
                  qc_ref, ktc_ref, vc_ref, gpc_ref, gtc_ref, hn_ref, out_ref,
                  hf_scr, hb_scr, st_scr, *, lc, t_x, t_c):
    nh = N_HEADS
    ng = 4 * nh
    ri = lax.broadcasted_iota(jnp.int32, (lc, lc), 0)
    ci = lax.broadcasted_iota(jnp.int32, (lc, lc), 1)
    lower = ci <= ri
    upper = ci >= ri
    tril = lower.astype(BF16)
    triu = upper.astype(BF16)
    ones_blk = jnp.ones((lc, DV), BF16)
    dirs = ((lower, lc - 1), (upper, 0))

    st_scr[...] = jnp.zeros(st_scr.shape, F32)

    def chunk_pair(cf, cb, q_ref, kt_ref, v_ref, tables, ms, want_out):
        g_row, brow, bcol = tables
        new_ms = []
        for di, cidx in enumerate((cf, cb)):
            mask, last = dirs[di]
            r0 = cidx * lc
            for h in range(nh):
                col_i = cidx * ng + di * nh + h
                col_f = col_i + 2 * nh
                arow = g_row[col_i:col_i + 1, :] - brow[di][col_f:col_f + 1, :]
                q_c = q_ref[0, r0:r0 + lc, h * DK:(h + 1) * DK]
                kt_c = kt_ref[0, cidx, h * DK:(h + 1) * DK, :]
                v_c = v_ref[0, r0:r0 + lc, h * DV:(h + 1) * DV]
                v_aug = jnp.concatenate([v_c, ones_blk], axis=1)
                hout, m_new = _chain_step(
                    q_c, kt_c, v_aug, bcol[di][:, col_f:col_f + 1], arow, mask, last,
                    st_scr.at[di * nh + h], ms[di * nh + h], want_out)
                new_ms.append(m_new)
                if want_out:
                    dst = hf_scr if di == 0 else hb_scr
                    dst[r0:r0 + lc, h * DV:(h + 1) * DV] = hout
        return new_ms

    ms = [jnp.zeros((1, 1), F32) for _ in range(2 * nh)]
    nc_c = t_c // lc
    tables_c = _gate_tables(gpc_ref, gtc_ref, tril, triu)
    for i in range(nc_c):
        ms = chunk_pair(i, nc_c - 1 - i, qc_ref, ktc_ref, vc_ref, tables_c, ms, False)
    nc_x = t_x // lc
    tables_x = _gate_tables(gpx_ref, gtx_ref, tril, triu)
    for i in range(nc_x):
        ms = chunk_pair(i, nc_x - 1 - i, qx_ref, ktx_ref, vx_ref, tables_x, ms, True)

    blk = 256
    for r0 in range(0, t_x, blk):
        hm = hf_scr[r0:r0 + blk, :] + hb_scr[r0:r0 + blk, :]
        parts = [_rms(hm[:, h * DV:(h + 1) * DV]) for h in range(nh)]
        hnorm = jnp.concatenate(parts, axis=1)
        gate = _sigmoid(ox_ref[0, r0:r0 + blk, :].astype(F32))
        out_ref[0, r0:r0 + blk, :] = (hnorm * hn_ref[...] * gate).astype(BF16)


def _mlstm(qx, ktx, vx, ox, gpx, gtx, qc, ktc, vc, gpc, gtc, head_norm, lc):
    bsz, t_x, qw = qx.shape
    t_c = qc.shape[1]
    vw = vx.shape[2]
    bspec = lambda a: pl.BlockSpec((1,) + a.shape[1:], lambda b: (b,) + (0,) * (a.ndim - 1))
    args = (qx, ktx, vx, ox, gpx, gtx, qc, ktc, vc, gpc, gtc)
    return pl.pallas_call(
        functools.partial(_mlstm_kernel, lc=lc, t_x=t_x, t_c=t_c),
        grid=(bsz,),
        in_specs=[bspec(a) for a in args] + [pl.BlockSpec((1, vw), lambda b: (0, 0))],
        out_specs=pl.BlockSpec((1, t_x, vw), lambda b: (b, 0, 0)),
        out_shape=jax.ShapeDtypeStruct((bsz, t_x, vw), BF16),
        scratch_shapes=[
            pltpu.VMEM((t_x, vw), F32),
            pltpu.VMEM((t_x, vw), F32),
            pltpu.VMEM((2 * N_HEADS, DK, 2 * DV), F32),
        ],
        compiler_params=pltpu.CompilerParams(
            dimension_semantics=("arbitrary",),
            vmem_limit_bytes=_vmem_limit(52 * 1024 * 1024)),
        name="mlstm",
    )(*args, head_norm.reshape(1, vw))


def _staggered(n_items, stages):
    state = [None] * n_items
    for step in range(len(stages) + n_items - 1):
        for i in range(n_items):
            k = step - i
            if 0 <= k < len(stages):
                state[i] = stages[k](i, state[i])


def _out_ffn_kernel(x_ref, pool_ref, ml_ref, mod_ref, wo_ref, n2_ref, wg_ref, wa_ref,
                    wd_ref, nf_ref, out_ref, *, sub):
    d = x_ref.shape[-1]
    tm = x_ref.shape[0]
    g1 = mod_ref[0, :, 2 * d:3 * d]
    sh2 = mod_ref[0, :, 3 * d:4 * d]
    gain2 = n2_ref[...] * (1.0 + mod_ref[0, :, 4 * d:5 * d])
    g2 = mod_ref[0, :, 5 * d:6 * d]
    pw = pool_ref.shape[-1]
    rows = [slice(r0, r0 + sub) for r0 in range(0, tm, sub)]

    def out_proj(i, _):
        r = rows[i]
        return _dot(pool_ref[r, :], wo_ref[0:pw, :]) + _dot(ml_ref[r, :], wo_ref[pw:, :])

    def norm2(i, mix):
        x1 = x_ref[rows[i], :] + g1 * mix
        return x1, (_rms(x1) * gain2 + sh2).astype(BF16)

    def up(i, st):
        x1, h2 = st
        return x1, _dot(h2, wg_ref[...]), _dot(h2, wa_ref[...])

    def act(i, st):
        x1, gg, aa = st
        return x1, (_silu(gg) * aa).astype(BF16)

    def down(i, st):
        x1, a = st
        return x1, _dot(a, wd_ref[...])

    def final(i, st):
        x1, acc = st
        x2 = x1 + g2 * acc
        out_ref[rows[i], :] = _rms(x2) * nf_ref[...]

    _staggered(len(rows), [out_proj, norm2, up, act, down, final])


def _out_ffn(x2d, pool, ml, mod3, w_out, norm2, w_gate, w_act, w_down, norm_f, tm, tpb, sub):
    n, d = x2d.shape
    dff = w_gate.shape[1]
    tok = lambda w: pl.BlockSpec((tm, w), lambda i: (i, 0))
    const = lambda *shape: pl.BlockSpec(shape, lambda i: (0,) * len(shape),
                                        pipeline_mode=pl.Buffered(1))
    return pl.pallas_call(
        functools.partial(_out_ffn_kernel, sub=sub),
        grid=(n // tm,),
        in_specs=[tok(d), tok(pool.shape[1]), tok(ml.shape[1]),
                  pl.BlockSpec((1, 1, mod3.shape[-1]), lambda i: (i // tpb, 0, 0)),
                  const(*w_out.shape), const(1, d), const(d, dff), const(d, dff),
                  const(dff, d), const(1, d)],
        out_specs=tok(d),
        out_shape=jax.ShapeDtypeStruct((n, d), F32),
        compiler_params=pltpu.CompilerParams(
            dimension_semantics=("arbitrary",),
            vmem_limit_bytes=_vmem_limit(56 * 1024 * 1024)),
        name="out_ffn",
    )(x2d, pool, ml, mod3, w_out, norm2.reshape(1, d), w_gate, w_act, w_down,
      norm_f.reshape(1, d))


def kernel(x, c, ctx, c_ctx, w_ada, b_ada, norm1, w_in, conv_qk, gate_bias, pool_w,
           pool_scale, head_norm, w_out, norm2, w_up, w_down, norm_f):
    bsz, t, d = x.shape
    t_c = ctx.shape[1]
    assert w_ada.shape[0] == 1, "single-layer block"
    pool_width = pool_w.shape[1] * pool_w.shape[2]
    qk_width = 2 * N_HEADS * DK
    ml_width = N_HEADS * DV
    o1 = pool_width
    o2 = o1 + qk_width
    o3 = o2 + ml_width
    o4 = o3 + ml_width
    lc = MLSTM_CHUNK
    tm = TOKEN_TILE
    assert t % tm == 0 and (bsz * t_c) % tm == 0 and t % OUT_TILE == 0 and OUT_TILE % OUT_SUB == 0
    assert tm % lc == 0 and t_c % lc == 0 and (tm % t_c == 0 or t_c % tm == 0)

    cc = jnp.concatenate([c, c_ctx[None, :]], axis=0)
    mod = _ada(cc, w_ada[0], b_ada[0])
    mod3 = mod.reshape(bsz + 1, 1, mod.shape[-1])

    wi = w_in[0]
    w_parts = (wi[:, :o1].astype(BF16), wi[:, o1:o2].astype(BF16), wi[:, o2:o3].astype(BF16),
               wi[:, o3:o4].astype(BF16), wi[:, o4:].astype(BF16))
    a_np, inv_np = _pool_consts()
    gd = pool_w.shape[2]
    pw = pool_w[0].astype(BF16)
    zero = jnp.zeros((gd, gd), BF16)
    pw_bd = jnp.stack([jnp.block([[pw[0], zero], [zero, pw[1]]]),
                       jnp.block([[pw[2], zero], [zero, pw[3]]])])
    pool_parts = (jnp.asarray(a_np, BF16), jnp.asarray(inv_np, F32), pw_bd, pool_scale[0])

    tpb = t // tm
    q_x, kt_x, v_x, g_x, gt_x, pool_x, o_x = _in_proj(
        x.reshape(bsz * t, d), t, mod3, lambda i: i // tpb, norm1[0], w_parts, gate_bias[0],
        conv_qk[0], pool_parts, tm, lc)
    q_c, kt_c, v_c, g_c, gt_c = _in_proj(
        ctx.reshape(bsz * t_c, d), t_c, mod3, lambda i: bsz, norm1[0], w_parts, gate_bias[0],
        conv_qk[0], None, tm, lc)

    ng = g_x.shape[1]

    def per_seq(a, tt):
        return a.reshape((bsz, tt) + a.shape[1:])

    def chunk_cols(g, tt):
        nc = tt // lc
        return jnp.transpose(g.reshape(bsz, nc, lc, ng), (0, 2, 1, 3)).reshape(bsz, lc, nc * ng)

    def chunk_rows(gt, tt):
        nc = tt // lc
        return gt.reshape(bsz, nc * ng, lc)

    hq = qk_width // 2
    ml = _mlstm(per_seq(q_x, t), kt_x.reshape(bsz, t // lc, hq, lc), per_seq(v_x, t),
                per_seq(o_x, t), chunk_cols(g_x, t), chunk_rows(gt_x, t),
                per_seq(q_c, t_c), kt_c.reshape(bsz, t_c // lc, hq, lc), per_seq(v_c, t_c),
                chunk_cols(g_c, t_c), chunk_rows(gt_c, t_c), head_norm[0], lc)

    dff = w_down.shape[1]
    wu = w_up[0]
    out = _out_ffn(x.reshape(bsz * t, d), pool_x, ml.reshape(bsz * t, ml_width), mod3,
                   w_out[0].astype(BF16), norm2[0], wu[:, :dff].astype(BF16),
                   wu[:, dff:].astype(BF16), w_down[0].astype(BF16), norm_f,
                   OUT_TILE, t // OUT_TILE, OUT_SUB)
    return out.reshape(bsz, t, d)
```

```python
import functools

import numpy as np
import jax
import jax.numpy as jnp
from jax import lax
from jax.experimental import pallas as pl
from jax.experimental.pallas import tpu as pltpu

F32 = jnp.float32
BF16 = jnp.bfloat16

EPS = 1e-6
GRID_W = 64
POOL_WINDOWS = (2, 4, 8, 16)
N_HEADS = 4
DK = 64
DV = 128
CONV_W = 3
LOG2E = 1.4426950408889634

V7X_VMEM_BYTES = 64 * 1024 * 1024
V7X_MXU_DIM = 256
BF16_SUBLANES = 16

MLSTM_CHUNK = V7X_MXU_DIM
TOKEN_TILE = 512
OUT_TILE = 1024
OUT_SUB = 256


def _vmem_limit(est_bytes):
    return int(min(V7X_VMEM_BYTES - 6 * 1024 * 1024, est_bytes))


def _dot(a, b):
    return jnp.dot(a, b, preferred_element_type=F32)


def _sigmoid(x):
    return 1.0 / (1.0 + jnp.exp(-x))


def _silu(x):
    return x * _sigmoid(x)


def _log_sigmoid(x):
    return jnp.minimum(x, 0.0) - jnp.log(1.0 + jnp.exp(-jnp.abs(x)))


def _rms(x):
    return x * lax.rsqrt(jnp.mean(x * x, axis=-1, keepdims=True) + EPS)


def _ada_kernel(c_ref, w_ref, b_ref, o_ref):
    s = _silu(c_ref[...]).astype(BF16)
    o_ref[...] = _dot(s, w_ref[...].astype(BF16)) + b_ref[...]


def _ada(cc, w_ada, b_ada):
    rows, d = cc.shape
    n = w_ada.shape[1]
    bn = 1536
    return pl.pallas_call(
        _ada_kernel,
        grid=(n // bn,),
        in_specs=[
            pl.BlockSpec((rows, d), lambda j: (0, 0)),
            pl.BlockSpec((d, bn), lambda j: (0, j)),
            pl.BlockSpec((1, bn), lambda j: (0, j)),
        ],
        out_specs=pl.BlockSpec((rows, bn), lambda j: (0, j)),
        out_shape=jax.ShapeDtypeStruct((rows, n), F32),
        compiler_params=pltpu.CompilerParams(
            dimension_semantics=("arbitrary",),
            vmem_limit_bytes=_vmem_limit(40 * 1024 * 1024)),
        name="ada",
    )(cc, w_ada, b_ada.reshape(1, n))


POOL_BLK = 256
HALO = BF16_SUBLANES


def _pool_consts():
    pos = np.arange(POOL_BLK) % GRID_W
    row = np.arange(POOL_BLK) // GRID_W
    a = np.zeros((len(POOL_WINDOWS), POOL_BLK, POOL_BLK), np.float32)
    inv = np.zeros((POOL_BLK, len(POOL_WINDOWS)), np.float32)
    for gi, win in enumerate(POOL_WINDOWS):
        lo = np.clip(pos - win // 2, 0, GRID_W - 1)
        hi = np.clip(pos + win // 2 - 1, 0, GRID_W - 1)
        for j in range(POOL_BLK):
            a[gi, j, row[j] * GRID_W + lo[j]: row[j] * GRID_W + hi[j] + 1] = 1.0
        inv[:, gi] = 1.0 / (hi - lo + 1)
    return a, inv


def _proj_kernel(*refs, tm, seq_len, lc, with_pool):
    if with_pool:
        (x_ref, xp_ref, xn_ref, mod_ref, n1_ref, wqk_ref, wv_ref, wg_ref, gb_ref, cw_ref,
         wp_ref, wo_ref, a_ref, inv_ref, pw_ref, ps_ref,
         q_ref, kt_ref, v_ref, g_ref, gt_ref, pool_ref, o_ref, lhs_scr) = refs
    else:
        (x_ref, xp_ref, xn_ref, mod_ref, n1_ref, wqk_ref, wv_ref, wg_ref, gb_ref, cw_ref,
         q_ref, kt_ref, v_ref, g_ref, gt_ref, lhs_scr) = refs
    d = x_ref.shape[-1]
    sub = lc
    assert tm == 2 * sub and sub == POOL_BLK
    sh = mod_ref[0, :, 0:d]
    gain = n1_ref[...] * (1.0 + mod_ref[0, :, d:2 * d])
    qw = q_ref.shape[-1]
    row = lax.broadcasted_iota(jnp.int32, (sub, 1), 0)

    def mod_norm(xv):
        return (_rms(xv) * gain + sh).astype(BF16)

    def conv_act(cur, prev_last, next_first, r0, ci):
        start = pl.program_id(0) * tm + r0
        prev_last = jnp.where(start % seq_len == 0, 0.0, prev_last)
        next_first = jnp.where((start + sub) % seq_len == 0, 0.0, next_first)
        dn = jnp.where(row == 0, prev_last, pltpu.roll(cur, 1, axis=0))
        up = jnp.where(row == sub - 1, next_first, pltpu.roll(cur, sub - 1, axis=0))
        act = _silu(cw_ref[0:1, :] * dn + cw_ref[1:2, :] * cur + cw_ref[2:3, :] * up)
        q_ref[r0:r0 + sub, :] = (act[:, :qw] * (DK ** -0.5)).astype(BF16)
        kt_ref[ci] = act[:, qw:].T.astype(BF16)

    def side_dots(hb, r0, ci):
        v_ref[r0:r0 + sub, :] = _dot(hb, wv_ref[...]).astype(BF16)
        g = _dot(hb, wg_ref[...]) + gb_ref[...]
        g_ref[r0:r0 + sub, :] = g
        gt_ref[ci] = g.T
        if with_pool:
            o_ref[r0:r0 + sub, :] = _dot(hb, wo_ref[...]).astype(BF16)
            return _dot(hb, wp_ref[...])
        return None

    def pool_mix(u, r0):
        gd = u.shape[1] // len(POOL_WINDOWS)
        cols = []
        for gi in range(len(POOL_WINDOWS)):
            ug = u[:, gi * gd:(gi + 1) * gd]
            win = _dot(a_ref[gi], ug.astype(BF16))
            cols.append(win * inv_ref[:, gi:gi + 1] - ug)
        dmat = jnp.concatenate(cols, axis=1).astype(BF16)
        half = dmat.shape[1] // 2
        out = jnp.concatenate(
            [_dot(dmat[:, :half], pw_ref[0]), _dot(dmat[:, half:], pw_ref[1])], axis=1)
        pool_ref[r0:r0 + sub, :] = (out * ps_ref[...]).astype(BF16)

    lhs_scr[0:sub, :] = mod_norm(x_ref[0:sub, :])
    hb_a = lhs_scr[0:sub, :]
    p_a = _dot(hb_a, wqk_ref[...])
    u_a = side_dots(hb_a, 0, 0)
    lhs_scr[sub:tm, :] = mod_norm(x_ref[sub:tm, :])
    lhs_scr[tm:tm + HALO, :] = mod_norm(xp_ref[...])
    lhs_scr[tm + HALO:tm + 2 * HALO, :] = mod_norm(xn_ref[...])
    p_b = _dot(lhs_scr[sub:tm + 2 * HALO, :], wqk_ref[...])
    conv_act(p_a, p_b[sub + HALO - 1:sub + HALO, :], p_b[0:1, :], 0, 0)
    hb_b = lhs_scr[sub:tm, :]
    u_b = side_dots(hb_b, sub, 1)
    if with_pool:
        pool_mix(u_a, 0)
    conv_act(p_b[0:sub, :], p_a[sub - 1:sub, :], p_b[sub + HALO:sub + HALO + 1, :], sub, 1)
    if with_pool:
        pool_mix(u_b, sub)


def _in_proj(x2, seq_len, mod3, mod_row_fn, norm1, w_parts, gate_bias, conv_w, pool_parts,
             tm, lc):
    n, d = x2.shape
    with_pool = pool_parts is not None
    w_pool, w_qk, w_v, w_o, w_g = w_parts
    ng = w_g.shape[1]
    qkw = w_qk.shape[1]
    hpt = tm // HALO
    const = lambda *shape: pl.BlockSpec(shape, lambda i: (0,) * len(shape))
    tok = lambda w: pl.BlockSpec((tm, w), lambda i: (i, 0))
    chunked = lambda r: pl.BlockSpec((tm // lc, r, lc), lambda i: (i, 0, 0))
    in_specs = [tok(d),
                pl.BlockSpec((HALO, d), lambda i: (jnp.maximum(i * hpt - 1, 0), 0)),
                pl.BlockSpec((HALO, d), lambda i: (jnp.minimum((i + 1) * hpt, n // HALO - 1), 0)),
                pl.BlockSpec((1, 1, mod3.shape[-1]), lambda i: (mod_row_fn(i), 0, 0)),
                const(1, d), const(*w_qk.shape), const(*w_v.shape), const(*w_g.shape),
                const(1, ng), const(*conv_w.shape)]
    args = [x2, x2, x2, mod3, norm1.reshape(1, d), w_qk, w_v, w_g, gate_bias.reshape(1, ng),
            conv_w]
    out_specs = [tok(qkw // 2), chunked(qkw // 2), tok(w_v.shape[1]), tok(ng), chunked(ng)]
    out_shape = [jax.ShapeDtypeStruct((n, qkw // 2), BF16),
                 jax.ShapeDtypeStruct((n // lc, qkw // 2, lc), BF16),
                 jax.ShapeDtypeStruct((n, w_v.shape[1]), BF16),
                 jax.ShapeDtypeStruct((n, ng), F32),
                 jax.ShapeDtypeStruct((n // lc, ng, lc), F32)]
    if with_pool:
        a_c, inv_c, pw_bd, p_scale = pool_parts
        in_specs += [const(*w_pool.shape), const(*w_o.shape), const(*a_c.shape),
                     const(*inv_c.shape), const(*pw_bd.shape), const(1, p_scale.shape[-1])]
        args += [w_pool, w_o, a_c, inv_c, pw_bd, p_scale.reshape(1, -1)]
        out_specs += [tok(w_pool.shape[1]), tok(w_o.shape[1])]
        out_shape += [jax.ShapeDtypeStruct((n, w_pool.shape[1]), BF16),
                      jax.ShapeDtypeStruct((n, w_o.shape[1]), BF16)]
    return pl.pallas_call(
        functools.partial(_proj_kernel, tm=tm, seq_len=seq_len, lc=lc, with_pool=with_pool),
        grid=(n // tm,),
        in_specs=in_specs,
        out_specs=out_specs,
        out_shape=out_shape,
        scratch_shapes=[pltpu.VMEM((tm + 2 * HALO, d), BF16)],
        compiler_params=pltpu.CompilerParams(
            dimension_semantics=("arbitrary",),
            vmem_limit_bytes=_vmem_limit(48 * 1024 * 1024)),
        name="in_proj_pool" if with_pool else "in_proj_ctx",
    )(*args)


def _split3(x):
    hi = x.astype(BF16)
    r1 = x - hi.astype(F32)
    mid = r1.astype(BF16)
    lo = (r1 - mid.astype(F32)).astype(BF16)
    return hi, mid, lo


def _tri_dot_left(tri, x):
    return sum(_dot(tri, p) for p in _split3(x))


def _tri_dot_right(x, tri):
    return sum(_dot(p, tri) for p in _split3(x))


def _gate_tables(gp_ref, gt_ref, tril, triu):
    g_row = gt_ref[0]
    lf_row = _log_sigmoid(g_row) * LOG2E
    lf_col = _log_sigmoid(gp_ref[0]) * LOG2E
    brow = (_tri_dot_right(lf_row, triu), _tri_dot_right(lf_row, tril))
    bcol = (_tri_dot_left(tril, lf_col), _tri_dot_left(triu, lf_col))
    return g_row * LOG2E, brow, bcol


def _chain_step(q_c, kt_c, v_aug, bcol, arow, mask, last, ct_ref, m0, want_out):
    lc = kt_c.shape[1]
    if want_out:
        am = jnp.where(mask, arow, -jnp.inf)
        cm = jnp.max(am, axis=1, keepdims=True)
        e = jnp.exp2(am - cm)
        p = (_dot(q_c, kt_c) * e).astype(BF16)
        amax = cm[last:last + 1, :]
        wk = e[last:last + 1, :]
    else:
        amax = jnp.max(arow, axis=1, keepdims=True)
        wk = jnp.exp2(arow - amax)
    kw = (kt_c.astype(F32) * wk).astype(BF16)
    if want_out:
        r = _dot(jnp.concatenate([p, kw], axis=0), v_aug)
        d_ct = r[lc:, :]
    else:
        d_ct = _dot(kw, v_aug)

    ct = ct_ref[...]
    m_last = jnp.maximum(m0, amax)
    if want_out:
        r2 = _dot(q_c, ct.astype(BF16))
        cm_r = jnp.broadcast_to(cm, (lc, DV))
        b_r = jnp.broadcast_to(bcol, (lc, DV))
        mcol = jnp.maximum(m0, cm_r)
        wi = jnp.exp2(cm_r - mcol)
        we = jnp.exp2(m0 - mcol)
        num = wi * r[:lc, :DV] + we * r2[:, :DV]
        den = wi * r[:lc, DV:] + we * r2[:, DV:]
        hout = num / jnp.maximum(jnp.abs(den), jnp.exp2(-(b_r + mcol)))
    else:
        hout = None
    ct_ref[...] = jnp.exp2(m0 - m_last) * ct + jnp.exp2(amax - m_last) * d_ct
    bl = bcol[last:last + 1, :]
    return hout, bl + m_last


def _mlstm_kernel(qx_ref, ktx_ref, vx_ref, ox_ref, gpx_ref, gtx_ref,
                  qc_ref, ktc_ref, vc_ref, gpc_ref, gtc_ref, hn_ref, out_ref,
                  hf_scr, hb_scr, st_scr, *, lc, t_x, t_c):
    nh = N_HEADS
    ng = 4 * nh
    ri = lax.broadcasted_iota(jnp.int32, (lc, lc), 0)
    ci = lax.broadcasted_iota(jnp.int32, (lc, lc), 1)
    lower = ci <= ri
    upper = ci >= ri
    tril = lower.astype(BF16)
    triu = upper.astype(BF16)
    ones_blk = jnp.ones((lc, DV), BF16)
    dirs = ((lower, lc - 1), (upper, 0))

    st_scr[...] = jnp.zeros(st_scr.shape, F32)

    def chunk_pair(cf, cb, q_ref, kt_ref, v_ref, tables, ms, want_out):
        g_row, brow, bcol = tables
        new_ms = []
        for di, cidx in enumerate((cf, cb)):
            mask, last = dirs[di]
            r0 = cidx * lc
            for h in range(nh):
                col_i = cidx * ng + di * nh + h
                col_f = col_i + 2 * nh
                arow = g_row[col_i:col_i + 1, :] - brow[di][col_f:col_f + 1, :]
                q_c = q_ref[0, r0:r0 + lc, h * DK:(h + 1) * DK]
                kt_c = kt_ref[0, cidx, h * DK:(h + 1) * DK, :]
                v_c = v_ref[0, r0:r0 + lc, h * DV:(h + 1) * DV]
                v_aug = jnp.concatenate([v_c, ones_blk], axis=1)
                hout, m_new = _chain_step(
                    q_c, kt_c, v_aug, bcol[di][:, col_f:col_f + 1], arow, mask, last,
                    st_scr.at[di * nh + h], ms[di * nh + h], want_out)
                new_ms.append(m_new)
                if want_out:
                    dst = hf_scr if di == 0 else hb_scr
                    dst[r0:r0 + lc, h * DV:(h + 1) * DV] = hout
        return new_ms

    ms = [jnp.zeros((1, 1), F32) for _ in range(2 * nh)]
    nc_c = t_c // lc
    tables_c = _gate_tables(gpc_ref, gtc_ref, tril, triu)
    for i in range(nc_c):
        ms = chunk_pair(i, nc_c - 1 - i, qc_ref, ktc_ref, vc_ref, tables_c, ms, False)
    nc_x = t_x // lc
    tables_x = _gate_tables(gpx_ref, gtx_ref, tril, triu)
    for i in range(nc_x):
        ms = chunk_pair(i, nc_x - 1 - i, qx_ref, ktx_ref, vx_ref, tables_x, ms, True)

    blk = 256
    for r0 in range(0, t_x, blk):
        hm = hf_scr[r0:r0 + blk, :] + hb_scr[r0:r0 + blk, :]
        parts = [_rms(hm[:, h * DV:(h + 1) * DV]) for h in range(nh)]
        hnorm = jnp.concatenate(parts, axis=1)
        gate = _sigmoid(ox_ref[0, r0:r0 + blk, :].astype(F32))
        out_ref[0, r0:r0 + blk, :] = (hnorm * hn_ref[...] * gate).astype(BF16)


def _mlstm(qx, ktx, vx, ox, gpx, gtx, qc, ktc, vc, gpc, gtc, head_norm, lc):
    bsz, t_x, qw = qx.shape
    t_c = qc.shape[1]
    vw = vx.shape[2]
    bspec = lambda a: pl.BlockSpec((1,) + a.shape[1:], lambda b: (b,) + (0,) * (a.ndim - 1))
    args = (qx, ktx, vx, ox, gpx, gtx, qc, ktc, vc, gpc, gtc)
    return pl.pallas_call(
        functools.partial(_mlstm_kernel, lc=lc, t_x=t_x, t_c=t_c),
        grid=(bsz,),
        in_specs=[bspec(a) for a in args] + [pl.BlockSpec((1, vw), lambda b: (0, 0))],
        out_specs=pl.BlockSpec((1, t_x, vw), lambda b: (b, 0, 0)),
        out_shape=jax.ShapeDtypeStruct((bsz, t_x, vw), BF16),
        scratch_shapes=[
            pltpu.VMEM((t_x, vw), F32),
            pltpu.VMEM((t_x, vw), F32),
            pltpu.VMEM((2 * N_HEADS, DK, 2 * DV), F32),
        ],
        compiler_params=pltpu.CompilerParams(
            dimension_semantics=("arbitrary",),
            vmem_limit_bytes=_vmem_limit(52 * 1024 * 1024)),
        name="mlstm",
    )(*args, head_norm.reshape(1, vw))


def _staggered(n_items, stages):
    state = [None] * n_items
    for step in range(len(stages) + n_items - 1):
        for i in range(n_items):
            k = step - i
            if 0 <= k < len(stages):
                state[i] = stages[k](i, state[i])


def _out_ffn_kernel(x_ref, pool_ref, ml_ref, mod_ref, wo_ref, n2_ref, wg_ref, wa_ref,
                    wd_ref, nf_ref, out_ref, *, sub):
    d = x_ref.shape[-1]
    tm = x_ref.shape[0]
    g1 = mod_ref[0, :, 2 * d:3 * d]
    sh2 = mod_ref[0, :, 3 * d:4 * d]
    gain2 = n2_ref[...] * (1.0 + mod_ref[0, :, 4 * d:5 * d])
    g2 = mod_ref[0, :, 5 * d:6 * d]
    pw = pool_ref.shape[-1]
    rows = [slice(r0, r0 + sub) for r0 in range(0, tm, sub)]

    def out_proj(i, _):
        r = rows[i]
        return _dot(pool_ref[r, :], wo_ref[0:pw, :]) + _dot(ml_ref[r, :], wo_ref[pw:, :])

    def norm2(i, mix):
        x1 = x_ref[rows[i], :] + g1 * mix
        return x1, (_rms(x1) * gain2 + sh2).astype(BF16)

    def up(i, st):
        x1, h2 = st
        return x1, _dot(h2, wg_ref[...]), _dot(h2, wa_ref[...])

    def act(i, st):
        x1, gg, aa = st
        return x1, (_silu(gg) * aa).astype(BF16)

    def down(i, st):
        x1, a = st
        return x1, _dot(a, wd_ref[...])

    def final(i, st):
        x1, acc = st
        x2 = x1 + g2 * acc
        out_ref[rows[i], :] = _rms(x2) * nf_ref[...]

    _staggered(len(rows), [out_proj, norm2, up, act, down, final])


def _out_ffn(x2d, pool, ml, mod3, w_out, norm2, w_gate, w_act, w_down, norm_f, tm, tpb, sub):
    n, d = x2d.shape
    dff = w_gate.shape[1]
    tok = lambda w: pl.BlockSpec((tm, w), lambda i: (i, 0))
    const = lambda *shape: pl.BlockSpec(shape, lambda i: (0,) * len(shape),
                                        pipeline_mode=pl.Buffered(1))
    return pl.pallas_call(
        functools.partial(_out_ffn_kernel, sub=sub),
        grid=(n // tm,),
        in_specs=[tok(d), tok(pool.shape[1]), tok(ml.shape[1]),
                  pl.BlockSpec((1, 1, mod3.shape[-1]), lambda i: (i // tpb, 0, 0)),
                  const(*w_out.shape), const(1, d), const(d, dff), const(d, dff),
                  const(dff, d), const(1, d)],
        out_specs=tok(d),
        out_shape=jax.ShapeDtypeStruct((n, d), F32),
        compiler_params=pltpu.CompilerParams(
            dimension_semantics=("arbitrary",),
            vmem_limit_bytes=_vmem_limit(56 * 1024 * 1024)),
        name="out_ffn",
    )(x2d, pool, ml, mod3, w_out, norm2.reshape(1, d), w_gate, w_act, w_down,
      norm_f.reshape(1, d))


def kernel(x, c, ctx, c_ctx, w_ada, b_ada, norm1, w_in, conv_qk, gate_bias, pool_w,
           pool_scale, head_norm, w_out, norm2, w_up, w_down, norm_f):
    bsz, t, d = x.shape
    t_c = ctx.shape[1]
    assert w_ada.shape[0] == 1, "single-layer block"
    pool_width = pool_w.shape[1] * pool_w.shape[2]
    qk_width = 2 * N_HEADS * DK
    ml_width = N_HEADS * DV
    o1 = pool_width
    o2 = o1 + qk_width
    o3 = o2 + ml_width
    o4 = o3 + ml_width
    lc = MLSTM_CHUNK
    tm = TOKEN_TILE
    assert t % tm == 0 and (bsz * t_c) % tm == 0 and t % OUT_TILE == 0 and OUT_TILE % OUT_SUB == 0
    assert tm % lc == 0 and t_c % lc == 0 and (tm % t_c == 0 or t_c % tm == 0)

    cc = jnp.concatenate([c, c_ctx[None, :]], axis=0)
    mod = _ada(cc, w_ada[0], b_ada[0])
    mod3 = mod.reshape(bsz + 1, 1, mod.shape[-1])

    wi = w_in[0]
    w_parts = (wi[:, :o1].astype(BF16), wi[:, o1:o2].astype(BF16), wi[:, o2:o3].astype(BF16),
               wi[:, o3:o4].astype(BF16), wi[:, o4:].astype(BF16))
    a_np, inv_np = _pool_consts()
    gd = pool_w.shape[2]
    pw = pool_w[0].astype(BF16)
    zero = jnp.zeros((gd, gd), BF16)
    pw_bd = jnp.stack([jnp.block([[pw[0], zero], [zero, pw[1]]]),
                       jnp.block([[pw[2], zero], [zero, pw[3]]])])
    pool_parts = (jnp.asarray(a_np, BF16), jnp.asarray(inv_np, F32), pw_bd, pool_scale[0])

    tpb = t // tm
    q_x, kt_x, v_x, g_x, gt_x, pool_x, o_x = _in_proj(
        x.reshape(bsz * t, d), t, mod3, lambda i: i // tpb, norm1[0], w_parts, gate_bias[0],
        conv_qk[0], pool_parts, tm, lc)
    q_c, kt_c, v_c, g_c, gt_c = _in_proj(
        ctx.reshape(bsz * t_c, d), t_c, mod3, lambda i: bsz, norm1[0], w_parts, gate_bias[0],
        conv_qk[0], None, tm, lc)

    ng = g_x.shape[1]

    def per_seq(a, tt):
        return a.reshape((bsz, tt) + a.shape[1:])

    def chunk_cols(g, tt):
        nc = tt // lc
        return jnp.transpose(g.reshape(bsz, nc, lc, ng), (0, 2, 1, 3)).reshape(bsz, lc, nc * ng)

    def chunk_rows(gt, tt):
        nc = tt // lc
        return gt.reshape(bsz, nc * ng, lc)

    hq = qk_width // 2
    ml = _mlstm(per_seq(q_x, t), kt_x.reshape(bsz, t // lc, hq, lc), per_seq(v_x, t),
                per_seq(o_x, t), chunk_cols(g_x, t), chunk_rows(gt_x, t),
                per_seq(q_c, t_c), kt_c.reshape(bsz, t_c // lc, hq, lc), per_seq(v_c, t_c),
                chunk_cols(g_c, t_c), chunk_rows(gt_c, t_c), head_norm[0], lc)

    dff = w_down.shape[1]
    wu = w_up[0]
    out = _out_ffn(x.reshape(bsz * t, d), pool_x, ml.reshape(bsz * t, ml_width), mod3,
                   w_out[0].astype(BF16), norm2[0], wu[:, :dff].astype(BF16),
                   wu[:, dff:].astype(BF16), w_down[0].astype(BF16), norm_f,
                   OUT_TILE, t // OUT_TILE, OUT_SUB)
    return out.reshape(bsz, t, d)
```

```python
import functools

import numpy as np
import jax
import jax.numpy as jnp
from jax import lax
from jax.experimental import pallas as pl
from jax.experimental.pallas import tpu as pltpu

F32 = jnp.float32
BF16 = jnp.bfloat16

EPS = 1e-6
GRID_W = 64
POOL_WINDOWS = (2, 4, 8, 16)
N_HEADS = 4
DK = 64
DV = 128
N_GATES = 4 * N_HEADS
CONV_W = 3
LOG2E = 1.4426950408889634

V7X_VMEM_BYTES = 64 * 1024 * 1024
V7X_MXU_DIM = 256
BF16_SUBLANES = 16

SUB = V7X_MXU_DIM
PROJ_TILE = 1024
OUT_TILE = 1024
HALO = BF16_SUBLANES


def _vmem_limit(est_bytes):
    return int(min(V7X_VMEM_BYTES - 6 * 1024 * 1024, est_bytes))


def _dot(a, b):
    return jnp.dot(a, b, preferred_element_type=F32)


def _sigmoid(x):
    return 1.0 / (1.0 + jnp.exp(-x))


def _silu(x):
    return x * _sigmoid(x)


def _log_sigmoid(x):
    return jnp.minimum(x, 0.0) - jnp.log(1.0 + jnp.exp(-jnp.abs(x)))


def _rms(x):
    return x * lax.rsqrt(jnp.mean(x * x, axis=-1, keepdims=True) + EPS)


def _ada_kernel(c_ref, w_ref, b_ref, o_ref):
    s = _silu(c_ref[...]).astype(BF16)
    o_ref[...] = _dot(s, w_ref[...].astype(BF16)) + b_ref[...]


def _ada(cc, w_ada, b_ada):
    rows, d = cc.shape
    n = w_ada.shape[1]
    bn = 1536
    return pl.pallas_call(
        _ada_kernel,
        grid=(n // bn,),
        in_specs=[
            pl.BlockSpec((rows, d), lambda j: (0, 0)),
            pl.BlockSpec((d, bn), lambda j: (0, j)),
            pl.BlockSpec((1, bn), lambda j: (0, j)),
        ],
        out_specs=pl.BlockSpec((rows, bn), lambda j: (0, j)),
        out_shape=jax.ShapeDtypeStruct((rows, n), F32),
        compiler_params=pltpu.CompilerParams(
            dimension_semantics=("arbitrary",),
            vmem_limit_bytes=_vmem_limit(40 * 1024 * 1024)),
        name="ada",
    )(cc, w_ada, b_ada.reshape(1, n))


def _pool_consts():
    pos = np.arange(SUB) % GRID_W
    row = np.arange(SUB) // GRID_W
    a = np.zeros((len(POOL_WINDOWS), SUB, SUB), np.float32)
    inv = np.zeros((SUB, len(POOL_WINDOWS)), np.float32)
    for gi, win in enumerate(POOL_WINDOWS):
        lo = np.clip(pos - win // 2, 0, GRID_W - 1)
        hi = np.clip(pos + win // 2 - 1, 0, GRID_W - 1)
        for j in range(SUB):
            a[gi, j, row[j] * GRID_W + lo[j]: row[j] * GRID_W + hi[j] + 1] = 1.0
        inv[:, gi] = 1.0 / (hi - lo + 1)
    return a, inv


def _proj_kernel(*refs, tm, seq_len, cols, with_pool):
    if with_pool:
        (x_ref, xp_ref, xn_ref, mod_ref, n1_ref, w_ref, gb_ref, cw_ref,
         a_ref, inv_ref, pw_ref, ps_ref,
         q_ref, kt_ref, v_ref, g_ref, gt_ref, pool_ref, o_ref, lhs_scr) = refs
    else:
        (x_ref, xp_ref, xn_ref, mod_ref, n1_ref, w_ref, gb_ref, cw_ref,
         q_ref, kt_ref, v_ref, g_ref, gt_ref, lhs_scr) = refs
    d = x_ref.shape[-1]
    sub = SUB
    n_sub = tm // sub
    c_pool, c_qk, c_v, c_o, c_g, c_end = cols
    sh = mod_ref[0, :, 0:d]
    gain = n1_ref[...] * (1.0 + mod_ref[0, :, d:2 * d])
    qw = q_ref.shape[-1]
    row = lax.broadcasted_iota(jnp.int32, (sub, 1), 0)

    def mod_norm(xv):
        return (_rms(xv) * gain + sh).astype(BF16)

    def conv_act(cur, prev_last, next_first, k):
        start = pl.program_id(0) * tm + k * sub
        prev_last = jnp.where(start % seq_len == 0, 0.0, prev_last)
        next_first = jnp.where((start + sub) % seq_len == 0, 0.0, next_first)
        dn = jnp.where(row == 0, prev_last, pltpu.roll(cur, 1, axis=0))
        up = jnp.where(row == sub - 1, next_first, pltpu.roll(cur, sub - 1, axis=0))
        act = _silu(cw_ref[0:1, :] * dn + cw_ref[1:2, :] * cur + cw_ref[2:3, :] * up)
        q_ref[k * sub:(k + 1) * sub, :] = (act[:, :qw] * (DK ** -0.5)).astype(BF16)
        kt_ref[k] = act[:, qw:].T.astype(BF16)

    def side_dots(hb, k):
        rows = slice(k * sub, (k + 1) * sub)
        v_ref[rows, :] = _dot(hb, w_ref[:, c_v:c_o]).astype(BF16)
        g = _dot(hb, w_ref[:, c_g:c_end]) + gb_ref[...]
        g_ref[rows, :] = g
        gt_ref[k] = g.T
        if with_pool:
            o_ref[rows, :] = _dot(hb, w_ref[:, c_o:c_g]).astype(BF16)
            return _dot(hb, w_ref[:, c_pool:c_qk])
        return None

    def pool_mix(u, k):
        gd = u.shape[1] // len(POOL_WINDOWS)
        parts = []
        for gi in range(len(POOL_WINDOWS)):
            ug = u[:, gi * gd:(gi + 1) * gd]
            win = _dot(a_ref[gi], ug.astype(BF16))
            parts.append(win * inv_ref[:, gi:gi + 1] - ug)
        dmat = jnp.concatenate(parts, axis=1).astype(BF16)
        half = dmat.shape[1] // 2
        out = jnp.concatenate(
            [_dot(dmat[:, :half], pw_ref[0]), _dot(dmat[:, half:], pw_ref[1])], axis=1)
        pool_ref[k * sub:(k + 1) * sub, :] = (out * ps_ref[...]).astype(BF16)

    ps, us = [None] * n_sub, [None] * n_sub

    def own_rows(k):
        lo = HALO if k == 0 else 0
        return lo, lo + sub

    def conv(k):
        lo, hi = own_rows(k)
        if k == 0:
            prev_last = ps[0][lo - 1:lo, :]
        else:
            prev_last = ps[k - 1][own_rows(k - 1)[1] - 1:own_rows(k - 1)[1], :]
        if k == n_sub - 1:
            next_first = ps[k][hi:hi + 1, :]
        else:
            next_first = ps[k + 1][own_rows(k + 1)[0]:own_rows(k + 1)[0] + 1, :]
        conv_act(ps[k][lo:hi, :], prev_last, next_first, k)

    for k in range(n_sub):
        r0 = HALO + k * sub
        lhs_scr[r0:r0 + sub, :] = mod_norm(x_ref[k * sub:(k + 1) * sub, :])
        lo, hi = r0, r0 + sub
        if k == 0:
            lhs_scr[0:HALO, :] = mod_norm(xp_ref[...])
            lo = 0
        if k == n_sub - 1:
            lhs_scr[hi:hi + HALO, :] = mod_norm(xn_ref[...])
            hi += HALO
        ps[k] = _dot(lhs_scr[lo:hi, :], w_ref[:, c_qk:c_v])
        if k > 0:
            conv(k - 1)
        us[k] = side_dots(lhs_scr[r0:r0 + sub, :], k)
        if k > 0 and with_pool:
            pool_mix(us[k - 1], k - 1)
    conv(n_sub - 1)
    if with_pool:
        pool_mix(us[n_sub - 1], n_sub - 1)


def _in_proj(x2, seq_len, mod3, mod_row_fn, norm1, w_in, cols, gate_bias, conv_w, pool_parts, tm):
    n, d = x2.shape
    with_pool = pool_parts is not None
    c_pool, c_qk, c_v, c_o, c_g, c_end = cols
    ng = c_end - c_g
    hq = (c_v - c_qk) // 2
    vw = c_o - c_v
    hpt = tm // HALO
    const = lambda *shape: pl.BlockSpec(shape, lambda i: (0,) * len(shape))
    tok = lambda w: pl.BlockSpec((tm, w), lambda i: (i, 0))
    chunked = lambda r: pl.BlockSpec((tm // SUB, r, SUB), lambda i: (i, 0, 0))
    in_specs = [tok(d),
                pl.BlockSpec((HALO, d), lambda i: (jnp.maximum(i * hpt - 1, 0), 0)),
                pl.BlockSpec((HALO, d), lambda i: (jnp.minimum((i + 1) * hpt, n // HALO - 1), 0)),
                pl.BlockSpec((1, 1, mod3.shape[-1]), lambda i: (mod_row_fn(i), 0, 0)),
                const(1, d), const(*w_in.shape), const(1, ng), const(*conv_w.shape)]
    args = [x2, x2, x2, mod3, norm1.reshape(1, d), w_in, gate_bias.reshape(1, ng), conv_w]
    out_specs = [tok(hq), chunked(hq), tok(vw), tok(ng), chunked(ng)]
    out_shape = [jax.ShapeDtypeStruct((n, hq), BF16),
                 jax.ShapeDtypeStruct((n // SUB, hq, SUB), BF16),
                 jax.ShapeDtypeStruct((n, vw), BF16),
                 jax.ShapeDtypeStruct((n, ng), F32),
                 jax.ShapeDtypeStruct((n // SUB, ng, SUB), F32)]
    if with_pool:
        a_c, inv_c, pw_bd, p_scale = pool_parts
        pool_w = c_qk - c_pool
        in_specs += [const(*a_c.shape), const(*inv_c.shape), const(*pw_bd.shape),
                     const(1, pool_w)]
        args += [a_c, inv_c, pw_bd, p_scale.reshape(1, pool_w)]
        out_specs += [tok(pool_w), tok(c_g - c_o)]
        out_shape += [jax.ShapeDtypeStruct((n, pool_w), BF16),
                      jax.ShapeDtypeStruct((n, c_g - c_o), BF16)]
    return pl.pallas_call(
        functools.partial(_proj_kernel, tm=tm, seq_len=seq_len, cols=cols, with_pool=with_pool),
        grid=(n // tm,),
        in_specs=in_specs,
        out_specs=out_specs,
        out_shape=out_shape,
        scratch_shapes=[pltpu.VMEM((tm + 2 * HALO, d), BF16)],
        compiler_params=pltpu.CompilerParams(
            dimension_semantics=("arbitrary",),
            vmem_limit_bytes=_vmem_limit(48 * 1024 * 1024)),
        name="in_proj_pool" if with_pool else "in_proj_ctx",
    )(*args)


def _split3(x):
    hi = x.astype(BF16)
    r1 = x - hi.astype(F32)
    mid = r1.astype(BF16)
    lo = (r1 - mid.astype(F32)).astype(BF16)
    return hi, mid, lo


def _tri_dot_left(tri, x):
    return sum(_dot(tri, p) for p in _split3(x))


def _tri_dot_right(x, tri):
    return sum(_dot(p, tri) for p in _split3(x))


def _gate_tables(g_col, g_row, tril, triu):
    lf_row = _log_sigmoid(g_row) * LOG2E
    lf_col = _log_sigmoid(g_col) * LOG2E
    brow = (_tri_dot_right(lf_row, triu), _tri_dot_right(lf_row, tril))
    bcol = (_tri_dot_left(tril, lf_col), _tri_dot_left(triu, lf_col))
    return g_row * LOG2E, brow, bcol


def _chain_step(q_c, kt_c, v_aug, bcol, arow, mask, last, ct_ref, m0, m_last, want_out):
    lc = kt_c.shape[1]
    ct = ct_ref[...]
    if want_out:
        am = jnp.where(mask, arow, -jnp.inf)
        cm = jnp.max(am, axis=1, keepdims=True)
        mcol = jnp.maximum(m0, jnp.broadcast_to(cm, (lc, DV)))
        b_r = jnp.broadcast_to(bcol, (lc, DV))
        e = jnp.exp2(am - jnp.concatenate([mcol, mcol], axis=1))
        p = (_dot(q_c, kt_c) * e).astype(BF16)
        wk = e[last:last + 1, :]
        qs = (q_c.astype(F32) * jnp.exp2(m0 - mcol)[:, :DK]).astype(BF16)
    else:
        wk = jnp.exp2(arow - m_last)
    kw = (kt_c.astype(F32) * wk).astype(BF16)
    if want_out:
        lhs = jnp.concatenate(
            [jnp.concatenate([p, qs], axis=1),
             jnp.concatenate([kw, jnp.zeros((DK, DK), BF16)], axis=1)], axis=0)
        r = _dot(lhs, jnp.concatenate([v_aug, ct.astype(BF16)], axis=0))
        hout = r[:lc, :DV] / jnp.maximum(jnp.abs(r[:lc, DV:]), jnp.exp2(-(b_r + mcol)))
        d_ct = r[lc:, :]
    else:
        hout = None
        d_ct = _dot(kw, v_aug)
    ct_ref[...] = jnp.exp2(m0 - m_last) * ct + d_ct
    return hout


def _mlstm_kernel(qx_ref, ktx_ref, vx_ref, ox_ref, gx_ref, gtx_ref,
                  qc_ref, ktc_ref, vc_ref, gc_ref, gtc_ref, hn_ref, out_ref,
                  hf_scr, hb_scr, st_scr, gpx_scr, gpc_scr, *, t_x, t_c):
    lc = SUB
    nh = N_HEADS
    ng = N_GATES
    ri = lax.broadcasted_iota(jnp.int32, (lc, lc), 0)
    ci = lax.broadcasted_iota(jnp.int32, (lc, lc), 1)
    lower = ci <= ri
    upper = ci >= ri
    tril = lower.astype(BF16)
    triu = upper.astype(BF16)
    ones_blk = jnp.ones((lc, DV), BF16)
    dirs = ((lower, lc - 1), (upper, 0))

    st_scr[...] = jnp.zeros(st_scr.shape, F32)

    def tables(g_ref, gt_ref, gp_scr):
        for c in range(g_ref.shape[1]):
            gp_scr[:, c * ng:(c + 1) * ng] = g_ref[0, c]
        return _gate_tables(gp_scr[...], gt_ref[0], tril, triu)

    def gate_slices(tabs, cidx, di, h):
        g_row, brow, bcol = tabs
        col_i = cidx * ng + di * nh + h
        col_f = col_i + 2 * nh
        arow = g_row[col_i:col_i + 1, :] - brow[di][col_f:col_f + 1, :]
        return arow, bcol[di][:, col_f:col_f + 1]

    def stabilisers(order, tabs, m_run):
        plan = []
        for i in range(len(order[0])):
            step = []
            for di in range(2):
                _, last = dirs[di]
                for h in range(nh):
                    arow, bcol = gate_slices(tabs, order[di][i], di, h)
                    m0 = m_run[di * nh + h]
                    m_last = jnp.maximum(m0, jnp.max(arow, axis=1, keepdims=True))
                    step.append((m0, m_last))
                    m_run[di * nh + h] = bcol[last:last + 1, :] + m_last
            plan.append(step)
        return plan

    def chunk_pair(cf, cb, q_ref, kt_ref, v_ref, tabs, ms, want_out):
        for di, cidx in enumerate((cf, cb)):
            mask, last = dirs[di]
            r0 = cidx * lc
            for h in range(nh):
                arow, bcol = gate_slices(tabs, cidx, di, h)
                q_c = q_ref[0, r0:r0 + lc, h * DK:(h + 1) * DK]
                kt_c = kt_ref[0, cidx, h * DK:(h + 1) * DK, :]
                v_c = v_ref[0, r0:r0 + lc, h * DV:(h + 1) * DV]
                v_aug = jnp.concatenate([v_c, ones_blk], axis=1)
                m0, m_last = ms[di * nh + h]
                hout = _chain_step(q_c, kt_c, v_aug, bcol, arow, mask, last,
                                   st_scr.at[di * nh + h], m0, m_last, want_out)
                if want_out:
                    dst = hf_scr if di == 0 else hb_scr
                    dst[r0:r0 + lc, h * DV:(h + 1) * DV] = hout

    m_run = [jnp.zeros((1, 1), F32) for _ in range(2 * nh)]
    nc_c = t_c // lc
    tabs_c = tables(gc_ref, gtc_ref, gpc_scr)
    order_c = (list(range(nc_c)), list(range(nc_c - 1, -1, -1)))
    plan_c = stabilisers(order_c, tabs_c, m_run)
    nc_x = t_x // lc
    tabs_x = tables(gx_ref, gtx_ref, gpx_scr)
    order_x = (list(range(nc_x)), list(range(nc_x - 1, -1, -1)))
    plan_x = stabilisers(order_x, tabs_x, m_run)
    for i in range(nc_c):
        chunk_pair(order_c[0][i], order_c[1][i], qc_ref, ktc_ref, vc_ref, tabs_c, plan_c[i],
                   False)
    for i in range(nc_x):
        chunk_pair(order_x[0][i], order_x[1][i], qx_ref, ktx_ref, vx_ref, tabs_x, plan_x[i],
                   True)

    for r0 in range(0, t_x, lc):
        hm = hf_scr[r0:r0 + lc, :] + hb_scr[r0:r0 + lc, :]
        parts = [_rms(hm[:, h * DV:(h + 1) * DV]) for h in range(nh)]
        hnorm = jnp.concatenate(parts, axis=1)
        gate = _sigmoid(ox_ref[0, r0:r0 + lc, :].astype(F32))
        out_ref[0, r0:r0 + lc, :] = (hnorm * hn_ref[...] * gate).astype(BF16)


def _mlstm(qx, ktx, vx, ox, gx, gtx, qc, ktc, vc, gc, gtc, head_norm):
    bsz, t_x, _ = qx.shape
    t_c = qc.shape[1]
    vw = vx.shape[2]
    ng = gx.shape[-1]
    bspec = lambda a: pl.BlockSpec((1,) + a.shape[1:], lambda b: (b,) + (0,) * (a.ndim - 1))
    args = (qx, ktx, vx, ox, gx, gtx, qc, ktc, vc, gc, gtc)
    return pl.pallas_call(
        functools.partial(_mlstm_kernel, t_x=t_x, t_c=t_c),
        grid=(bsz,),
        in_specs=[bspec(a) for a in args] + [pl.BlockSpec((1, vw), lambda b: (0, 0))],
        out_specs=pl.BlockSpec((1, t_x, vw), lambda b: (b, 0, 0)),
        out_shape=jax.ShapeDtypeStruct((bsz, t_x, vw), BF16),
        scratch_shapes=[
            pltpu.VMEM((t_x, vw), F32),
            pltpu.VMEM((t_x, vw), F32),
            pltpu.VMEM((2 * N_HEADS, DK, 2 * DV), F32),
            pltpu.VMEM((SUB, (t_x // SUB) * ng), F32),
            pltpu.VMEM((SUB, (t_c // SUB) * ng), F32),
        ],
        compiler_params=pltpu.CompilerParams(
            dimension_semantics=("arbitrary",),
            vmem_limit_bytes=_vmem_limit(52 * 1024 * 1024)),
        name="mlstm",
    )(*args, head_norm.reshape(1, vw))


def _staggered(n_items, stages):
    state = [None] * n_items
    for step in range(len(stages) + n_items - 1):
        for i in range(n_items):
            k = step - i
            if 0 <= k < len(stages):
                state[i] = stages[k](i, state[i])


def _out_ffn_kernel(x_ref, pool_ref, ml_ref, mod_ref, wo_ref, n2_ref, wu_ref,
                    wd_ref, nf_ref, out_ref):
    d = x_ref.shape[-1]
    tm = x_ref.shape[0]
    dff = wd_ref.shape[0]
    g1 = mod_ref[0, :, 2 * d:3 * d]
    sh2 = mod_ref[0, :, 3 * d:4 * d]
    gain2 = n2_ref[...] * (1.0 + mod_ref[0, :, 4 * d:5 * d])
    g2 = mod_ref[0, :, 5 * d:6 * d]
    pw = pool_ref.shape[-1]
    rows = [slice(r0, r0 + SUB) for r0 in range(0, tm, SUB)]

    def out_proj(i, _):
        r = rows[i]
        return _dot(pool_ref[r, :], wo_ref[0:pw, :]) + _dot(ml_ref[r, :], wo_ref[pw:, :])

    def norm2(i, mix):
        x1 = x_ref[rows[i], :] + g1 * mix
        return x1, (_rms(x1) * gain2 + sh2).astype(BF16)

    def up(i, st):
        x1, h2 = st
        return x1, _dot(h2, wu_ref[:, 0:dff]), _dot(h2, wu_ref[:, dff:2 * dff])

    def act(i, st):
        x1, gg, aa = st
        return x1, (_silu(gg) * aa).astype(BF16)

    def down(i, st):
        x1, a = st
        return x1, _dot(a, wd_ref[...])

    def final(i, st):
        x1, acc = st
        x2 = x1 + g2 * acc
        out_ref[rows[i], :] = _rms(x2) * nf_ref[...]

    _staggered(len(rows), [out_proj, norm2, up, act, down, final])


def _out_ffn(x2d, pool, ml, mod3, w_out, norm2, w_up, w_down, norm_f, tm, tpb):
    n, d = x2d.shape
    tok = lambda w: pl.BlockSpec((tm, w), lambda i: (i, 0))
    const = lambda *shape: pl.BlockSpec(shape, lambda i: (0,) * len(shape),
                                        pipeline_mode=pl.Buffered(1))
    return pl.pallas_call(
        _out_ffn_kernel,
        grid=(n // tm,),
        in_specs=[tok(d), tok(pool.shape[1]), tok(ml.shape[1]),
                  pl.BlockSpec((1, 1, mod3.shape[-1]), lambda i: (i // tpb, 0, 0)),
                  const(*w_out.shape), const(1, d), const(*w_up.shape),
                  const(*w_down.shape), const(1, d)],
        out_specs=tok(d),
        out_shape=jax.ShapeDtypeStruct((n, d), F32),
        compiler_params=pltpu.CompilerParams(
            dimension_semantics=("arbitrary",),
            vmem_limit_bytes=_vmem_limit(56 * 1024 * 1024)),
        name="out_ffn",
    )(x2d, pool, ml, mod3, w_out, norm2.reshape(1, d), w_up, w_down, norm_f.reshape(1, d))


def kernel(x, c, ctx, c_ctx, w_ada, b_ada, norm1, w_in, conv_qk, gate_bias, pool_w,
           pool_scale, head_norm, w_out, norm2, w_up, w_down, norm_f):
    bsz, t, d = x.shape
    t_c = ctx.shape[1]
    assert w_ada.shape[0] == 1, "single-layer block"
    pool_width = pool_w.shape[1] * pool_w.shape[2]
    qk_width = 2 * N_HEADS * DK
    ml_width = N_HEADS * DV
    o1 = pool_width
    o2 = o1 + qk_width
    o3 = o2 + ml_width
    o4 = o3 + ml_width
    cols = (0, o1, o2, o3, o4, o4 + N_GATES)
    tm = PROJ_TILE
    assert t % tm == 0 and (bsz * t_c) % tm == 0 and t_c % SUB == 0 and t % OUT_TILE == 0
    assert w_in.shape[2] == cols[-1] and w_up.shape[2] == 2 * w_down.shape[1]

    cc = jnp.concatenate([c, c_ctx[None, :]], axis=0)
    mod = _ada(cc, w_ada[0], b_ada[0])
    mod3 = mod.reshape(bsz + 1, 1, mod.shape[-1])

    w_in_b = w_in[0].astype(BF16)
    a_np, inv_np = _pool_consts()
    gd = pool_w.shape[2]
    pw = pool_w[0].astype(BF16)
    zero = jnp.zeros((gd, gd), BF16)
    pw_bd = jnp.stack([jnp.block([[pw[0], zero], [zero, pw[1]]]),
                       jnp.block([[pw[2], zero], [zero, pw[3]]])])
    pool_parts = (jnp.asarray(a_np, BF16), jnp.asarray(inv_np, F32), pw_bd, pool_scale[0])

    tpb = t // tm
    q_x, kt_x, v_x, g_x, gt_x, pool_x, o_x = _in_proj(
        x.reshape(bsz * t, d), t, mod3, lambda i: i // tpb, norm1[0], w_in_b, cols,
        gate_bias[0], conv_qk[0], pool_parts, tm)
    q_c, kt_c, v_c, g_c, gt_c = _in_proj(
        ctx.reshape(bsz * t_c, d), t_c, mod3, lambda i: bsz, norm1[0], w_in_b, cols,
        gate_bias[0], conv_qk[0], None, tm)

    def per_seq(a, tt):
        return a.reshape((bsz, tt) + a.shape[1:])

    def per_seq_chunks(a, tt, merge=False):
        nc = tt // SUB
        if merge:
            return a.reshape(bsz, nc * a.shape[1], a.shape[2])
        return a.reshape((bsz, nc) + a.shape[1:])

    ml = _mlstm(per_seq(q_x, t), per_seq_chunks(kt_x, t), per_seq(v_x, t), per_seq(o_x, t),
                g_x.reshape(bsz, t // SUB, SUB, N_GATES), per_seq_chunks(gt_x, t, merge=True),
                per_seq(q_c, t_c), per_seq_chunks(kt_c, t_c), per_seq(v_c, t_c),
                g_c.reshape(bsz, t_c // SUB, SUB, N_GATES), per_seq_chunks(gt_c, t_c, merge=True),
                head_norm[0])

    out = _out_ffn(x.reshape(bsz * t, d), pool_x, ml.reshape(bsz * t, ml_width), mod3,
                   w_out[0].astype(BF16), norm2[0], w_up[0].astype(BF16),
                   w_down[0].astype(BF16), norm_f, OUT_TILE, t // OUT_TILE)
    return out.reshape(bsz, t, d)
```

```python
import functools

import numpy as np
import jax
import jax.numpy as jnp
from jax import lax
from jax.experimental import pallas as pl
from jax.experimental.pallas import tpu as pltpu

F32 = jnp.float32
BF16 = jnp.bfloat16

EPS = 1e-6
GRID_W = 64
POOL_WINDOWS = (2, 4, 8, 16)
N_HEADS = 4
DK = 64
DV = 128
N_GATES = 4 * N_HEADS
CONV_W = 3
LOG2E = 1.4426950408889634

V7X_VMEM_BYTES = 64 * 1024 * 1024
V7X_MXU_DIM = 256
BF16_SUBLANES = 16

SUB = V7X_MXU_DIM
PROJ_TILE = 1024
OUT_TILE = 1024
HALO = BF16_SUBLANES


def _vmem_limit(est_bytes):
    return int(min(V7X_VMEM_BYTES - 6 * 1024 * 1024, est_bytes))


def _dot(a, b):
    return jnp.dot(a, b, preferred_element_type=F32)


def _sigmoid(x):
    return 1.0 / (1.0 + jnp.exp(-x))


def _silu(x):
    return x * _sigmoid(x)


def _log_sigmoid(x):
    return jnp.minimum(x, 0.0) - jnp.log(1.0 + jnp.exp(-jnp.abs(x)))


def _rms(x):
    return x * lax.rsqrt(jnp.mean(x * x, axis=-1, keepdims=True) + EPS)


def _ada_kernel(c_ref, w_ref, b_ref, o_ref):
    s = _silu(c_ref[...]).astype(BF16)
    o_ref[...] = _dot(s, w_ref[...].astype(BF16)) + b_ref[...]


def _ada(cc, w_ada, b_ada):
    rows, d = cc.shape
    n = w_ada.shape[1]
    bn = 1536
    return pl.pallas_call(
        _ada_kernel,
        grid=(n // bn,),
        in_specs=[
            pl.BlockSpec((rows, d), lambda j: (0, 0)),
            pl.BlockSpec((d, bn), lambda j: (0, j)),
            pl.BlockSpec((1, bn), lambda j: (0, j)),
        ],
        out_specs=pl.BlockSpec((rows, bn), lambda j: (0, j)),
        out_shape=jax.ShapeDtypeStruct((rows, n), F32),
        compiler_params=pltpu.CompilerParams(
            dimension_semantics=("arbitrary",),
            vmem_limit_bytes=_vmem_limit(40 * 1024 * 1024)),
        name="ada",
    )(cc, w_ada, b_ada.reshape(1, n))


def _pool_consts():
    pos = np.arange(SUB) % GRID_W
    row = np.arange(SUB) // GRID_W
    a = np.zeros((len(POOL_WINDOWS), SUB, SUB), np.float32)
    inv = np.zeros((SUB, len(POOL_WINDOWS)), np.float32)
    for gi, win in enumerate(POOL_WINDOWS):
        lo = np.clip(pos - win // 2, 0, GRID_W - 1)
        hi = np.clip(pos + win // 2 - 1, 0, GRID_W - 1)
        for j in range(SUB):
            a[gi, j, row[j] * GRID_W + lo[j]: row[j] * GRID_W + hi[j] + 1] = 1.0
        inv[:, gi] = 1.0 / (hi - lo + 1)
    return a, inv


def _proj_kernel(*refs, tm, seq_len, cols, with_pool):
    if with_pool:
        (x_ref, xp_ref, xn_ref, mod_ref, n1_ref, w_ref, gb_ref, cw_ref,
         a_ref, inv_ref, pw_ref, ps_ref,
         q_ref, kt_ref, v_ref, g_ref, gt_ref, pool_ref, o_ref, lhs_scr) = refs
    else:
        (x_ref, xp_ref, xn_ref, mod_ref, n1_ref, w_ref, gb_ref, cw_ref,
         q_ref, kt_ref, v_ref, g_ref, gt_ref, lhs_scr) = refs
    d = x_ref.shape[-1]
    sub = SUB
    n_sub = tm // sub
    c_pool, c_qk, c_v, c_o, c_g, c_end = cols
    sh = mod_ref[0, :, 0:d]
    gain = n1_ref[...] * (1.0 + mod_ref[0, :, d:2 * d])
    qw = q_ref.shape[-1]
    row = lax.broadcasted_iota(jnp.int32, (sub, 1), 0)

    def mod_norm(xv):
        return (_rms(xv) * gain + sh).astype(BF16)

    def conv_act(cur, prev_last, next_first, k):
        start = pl.program_id(0) * tm + k * sub
        prev_last = jnp.where(start % seq_len == 0, 0.0, prev_last)
        next_first = jnp.where((start + sub) % seq_len == 0, 0.0, next_first)
        dn = jnp.where(row == 0, prev_last, pltpu.roll(cur, 1, axis=0))
        up = jnp.where(row == sub - 1, next_first, pltpu.roll(cur, sub - 1, axis=0))
        act = _silu(cw_ref[0:1, :] * dn + cw_ref[1:2, :] * cur + cw_ref[2:3, :] * up)
        q_ref[k * sub:(k + 1) * sub, :] = (act[:, :qw] * (DK ** -0.5)).astype(BF16)
        kt_ref[k] = act[:, qw:].T.astype(BF16)

    def side_dots(hb, k):
        rows = slice(k * sub, (k + 1) * sub)
        v_ref[rows, :] = _dot(hb, w_ref[:, c_v:c_o]).astype(BF16)
        g = _dot(hb, w_ref[:, c_g:c_end]) + gb_ref[...]
        g_ref[rows, :] = g
        gt_ref[k] = g.T
        if with_pool:
            o_ref[rows, :] = _dot(hb, w_ref[:, c_o:c_g]).astype(BF16)
            return _dot(hb, w_ref[:, c_pool:c_qk])
        return None

    def pool_mix(u, k):
        gd = u.shape[1] // len(POOL_WINDOWS)
        parts = []
        for gi in range(len(POOL_WINDOWS)):
            ug = u[:, gi * gd:(gi + 1) * gd]
            win = _dot(a_ref[gi], ug.astype(BF16))
            parts.append(win * inv_ref[:, gi:gi + 1] - ug)
        dmat = jnp.concatenate(parts, axis=1).astype(BF16)
        half = dmat.shape[1] // 2
        out = jnp.concatenate(
            [_dot(dmat[:, :half], pw_ref[0]), _dot(dmat[:, half:], pw_ref[1])], axis=1)
        pool_ref[k * sub:(k + 1) * sub, :] = (out * ps_ref[...]).astype(BF16)

    ps, us = [None] * n_sub, [None] * n_sub

    def own_rows(k):
        lo = HALO if k == 0 else 0
        return lo, lo + sub

    def conv(k):
        lo, hi = own_rows(k)
        if k == 0:
            prev_last = ps[0][lo - 1:lo, :]
        else:
            prev_last = ps[k - 1][own_rows(k - 1)[1] - 1:own_rows(k - 1)[1], :]
        if k == n_sub - 1:
            next_first = ps[k][hi:hi + 1, :]
        else:
            next_first = ps[k + 1][own_rows(k + 1)[0]:own_rows(k + 1)[0] + 1, :]
        conv_act(ps[k][lo:hi, :], prev_last, next_first, k)

    for k in range(n_sub):
        r0 = HALO + k * sub
        lhs_scr[r0:r0 + sub, :] = mod_norm(x_ref[k * sub:(k + 1) * sub, :])
        lo, hi = r0, r0 + sub
        if k == 0:
            lhs_scr[0:HALO, :] = mod_norm(xp_ref[...])
            lo = 0
        if k == n_sub - 1:
            lhs_scr[hi:hi + HALO, :] = mod_norm(xn_ref[...])
            hi += HALO
        ps[k] = _dot(lhs_scr[lo:hi, :], w_ref[:, c_qk:c_v])
        if k > 0:
            conv(k - 1)
        us[k] = side_dots(lhs_scr[r0:r0 + sub, :], k)
        if k > 0 and with_pool:
            pool_mix(us[k - 1], k - 1)
    conv(n_sub - 1)
    if with_pool:
        pool_mix(us[n_sub - 1], n_sub - 1)


def _in_proj(x2, seq_len, mod3, mod_row_fn, norm1, w_in, cols, gate_bias, conv_w, pool_parts, tm):
    n, d = x2.shape
    with_pool = pool_parts is not None
    c_pool, c_qk, c_v, c_o, c_g, c_end = cols
    ng = c_end - c_g
    hq = (c_v - c_qk) // 2
    vw = c_o - c_v
    hpt = tm // HALO
    const = lambda *shape: pl.BlockSpec(shape, lambda i: (0,) * len(shape))
    tok = lambda w: pl.BlockSpec((tm, w), lambda i: (i, 0))
    chunked = lambda r: pl.BlockSpec((tm // SUB, r, SUB), lambda i: (i, 0, 0))
    in_specs = [tok(d),
                pl.BlockSpec((HALO, d), lambda i: (jnp.maximum(i * hpt - 1, 0), 0)),
                pl.BlockSpec((HALO, d), lambda i: (jnp.minimum((i + 1) * hpt, n // HALO - 1), 0)),
                pl.BlockSpec((1, 1, mod3.shape[-1]), lambda i: (mod_row_fn(i), 0, 0)),
                const(1, d), const(*w_in.shape), const(1, ng), const(*conv_w.shape)]
    args = [x2, x2, x2, mod3, norm1.reshape(1, d), w_in, gate_bias.reshape(1, ng), conv_w]
    out_specs = [tok(hq), chunked(hq), tok(vw), tok(ng), chunked(ng)]
    out_shape = [jax.ShapeDtypeStruct((n, hq), BF16),
                 jax.ShapeDtypeStruct((n // SUB, hq, SUB), BF16),
                 jax.ShapeDtypeStruct((n, vw), BF16),
                 jax.ShapeDtypeStruct((n, ng), F32),
                 jax.ShapeDtypeStruct((n // SUB, ng, SUB), F32)]
    if with_pool:
        a_c, inv_c, pw_bd, p_scale = pool_parts
        pool_w = c_qk - c_pool
        in_specs += [const(*a_c.shape), const(*inv_c.shape), const(*pw_bd.shape),
                     const(1, pool_w)]
        args += [a_c, inv_c, pw_bd, p_scale.reshape(1, pool_w)]
        out_specs += [tok(pool_w), tok(c_g - c_o)]
        out_shape += [jax.ShapeDtypeStruct((n, pool_w), BF16),
                      jax.ShapeDtypeStruct((n, c_g - c_o), BF16)]
    return pl.pallas_call(
        functools.partial(_proj_kernel, tm=tm, seq_len=seq_len, cols=cols, with_pool=with_pool),
        grid=(n // tm,),
        in_specs=in_specs,
        out_specs=out_specs,
        out_shape=out_shape,
        scratch_shapes=[pltpu.VMEM((tm + 2 * HALO, d), BF16)],
        compiler_params=pltpu.CompilerParams(
            dimension_semantics=("arbitrary",),
            vmem_limit_bytes=_vmem_limit(48 * 1024 * 1024)),
        name="in_proj_pool" if with_pool else "in_proj_ctx",
    )(*args)


def _split3(x):
    hi = x.astype(BF16)
    r1 = x - hi.astype(F32)
    mid = r1.astype(BF16)
    lo = (r1 - mid.astype(F32)).astype(BF16)
    return hi, mid, lo


def _tri_dot_left(tri, x):
    return sum(_dot(tri, p) for p in _split3(x))


def _tri_dot_right(x, tri):
    return sum(_dot(p, tri) for p in _split3(x))


def _gate_tables(g_col, g_row, tril, triu):
    lf_row = _log_sigmoid(g_row) * LOG2E
    lf_col = _log_sigmoid(g_col) * LOG2E
    brow = (_tri_dot_right(lf_row, triu), _tri_dot_right(lf_row, tril))
    bcol = (_tri_dot_left(tril, lf_col), _tri_dot_left(triu, lf_col))
    return g_row * LOG2E, brow, bcol


def _chain_step(q_c, kt_c, v_aug, bcol, arow, mask, last, ct_ref, m0, m_last, want_out):
    lc = kt_c.shape[1]
    ct = ct_ref[...]
    if want_out:
        am = jnp.where(mask, arow, -jnp.inf)
        cm = jnp.max(am, axis=1, keepdims=True)
        mcol = jnp.maximum(m0, jnp.broadcast_to(cm, (lc, DV)))
        b_r = jnp.broadcast_to(bcol, (lc, DV))
        e = jnp.exp2(am - jnp.concatenate([mcol, mcol], axis=1))
        p = (_dot(q_c, kt_c) * e).astype(BF16)
        wk = e[last:last + 1, :]
        qs = (q_c.astype(F32) * jnp.exp2(m0 - mcol)[:, :DK]).astype(BF16)
    else:
        wk = jnp.exp2(arow - m_last)
    kw = (kt_c.astype(F32) * wk).astype(BF16)
    if want_out:
        r = _dot(jnp.concatenate([p, kw], axis=0), v_aug)
        nd = r[:lc, :] + _dot(qs, ct.astype(BF16))
        hout = nd[:, :DV] / jnp.maximum(jnp.abs(nd[:, DV:]), jnp.exp2(-(b_r + mcol)))
        d_ct = r[lc:, :]
    else:
        hout = None
        d_ct = _dot(kw, v_aug)
    ct_ref[...] = jnp.exp2(m0 - m_last) * ct + d_ct
    return hout


def _mlstm_kernel(qx_ref, ktx_ref, vx_ref, ox_ref, gx_ref, gtx_ref,
                  qc_ref, ktc_ref, vc_ref, gc_ref, gtc_ref, hn_ref, out_ref,
                  hf_scr, hb_scr, st_scr, gpx_scr, gpc_scr, *, t_x, t_c):
    lc = SUB
    nh = N_HEADS
    ng = N_GATES
    ri = lax.broadcasted_iota(jnp.int32, (lc, lc), 0)
    ci = lax.broadcasted_iota(jnp.int32, (lc, lc), 1)
    lower = ci <= ri
    upper = ci >= ri
    tril = lower.astype(BF16)
    triu = upper.astype(BF16)
    ones_blk = jnp.ones((lc, DV), BF16)
    dirs = ((lower, lc - 1), (upper, 0))

    st_scr[...] = jnp.zeros(st_scr.shape, F32)

    def tables(g_ref, gt_ref, gp_scr):
        for c in range(g_ref.shape[1]):
            gp_scr[:, c * ng:(c + 1) * ng] = g_ref[0, c]
        return _gate_tables(gp_scr[...], gt_ref[0], tril, triu)

    def gate_slices(tabs, cidx, di, h):
        g_row, brow, bcol = tabs
        col_i = cidx * ng + di * nh + h
        col_f = col_i + 2 * nh
        arow = g_row[col_i:col_i + 1, :] - brow[di][col_f:col_f + 1, :]
        return arow, bcol[di][:, col_f:col_f + 1]

    def stabilisers(order, tabs, m_run):
        plan = []
        for i in range(len(order[0])):
            step = []
            for di in range(2):
                _, last = dirs[di]
                for h in range(nh):
                    arow, bcol = gate_slices(tabs, order[di][i], di, h)
                    m0 = m_run[di * nh + h]
                    m_last = jnp.maximum(m0, jnp.max(arow, axis=1, keepdims=True))
                    step.append((m0, m_last))
                    m_run[di * nh + h] = bcol[last:last + 1, :] + m_last
            plan.append(step)
        return plan

    def chunk_pair(cf, cb, q_ref, kt_ref, v_ref, tabs, ms, want_out):
        for di, cidx in enumerate((cf, cb)):
            mask, last = dirs[di]
            r0 = cidx * lc
            for h in range(nh):
                arow, bcol = gate_slices(tabs, cidx, di, h)
                q_c = q_ref[0, r0:r0 + lc, h * DK:(h + 1) * DK]
                kt_c = kt_ref[0, cidx, h * DK:(h + 1) * DK, :]
                v_c = v_ref[0, r0:r0 + lc, h * DV:(h + 1) * DV]
                v_aug = jnp.concatenate([v_c, ones_blk], axis=1)
                m0, m_last = ms[di * nh + h]
                hout = _chain_step(q_c, kt_c, v_aug, bcol, arow, mask, last,
                                   st_scr.at[di * nh + h], m0, m_last, want_out)
                if want_out:
                    dst = hf_scr if di == 0 else hb_scr
                    dst[r0:r0 + lc, h * DV:(h + 1) * DV] = hout

    m_run = [jnp.zeros((1, 1), F32) for _ in range(2 * nh)]
    nc_c = t_c // lc
    tabs_c = tables(gc_ref, gtc_ref, gpc_scr)
    order_c = (list(range(nc_c)), list(range(nc_c - 1, -1, -1)))
    plan_c = stabilisers(order_c, tabs_c, m_run)
    nc_x = t_x // lc
    tabs_x = tables(gx_ref, gtx_ref, gpx_scr)
    order_x = (list(range(nc_x)), list(range(nc_x - 1, -1, -1)))
    plan_x = stabilisers(order_x, tabs_x, m_run)
    for i in range(nc_c):
        chunk_pair(order_c[0][i], order_c[1][i], qc_ref, ktc_ref, vc_ref, tabs_c, plan_c[i],
                   False)
    for i in range(nc_x):
        pl.when(pl.program_id(0) >= 0)(functools.partial(
            chunk_pair, order_x[0][i], order_x[1][i], qx_ref, ktx_ref, vx_ref, tabs_x,
            plan_x[i], True))

    for r0 in range(0, t_x, lc):
        hm = hf_scr[r0:r0 + lc, :] + hb_scr[r0:r0 + lc, :]
        parts = [_rms(hm[:, h * DV:(h + 1) * DV]) for h in range(nh)]
        hnorm = jnp.concatenate(parts, axis=1)
        gate = _sigmoid(ox_ref[0, r0:r0 + lc, :].astype(F32))
        out_ref[0, r0:r0 + lc, :] = (hnorm * hn_ref[...] * gate).astype(BF16)


def _mlstm(qx, ktx, vx, ox, gx, gtx, qc, ktc, vc, gc, gtc, head_norm):
    bsz, t_x, _ = qx.shape
    t_c = qc.shape[1]
    vw = vx.shape[2]
    ng = gx.shape[-1]
    bspec = lambda a: pl.BlockSpec((1,) + a.shape[1:], lambda b: (b,) + (0,) * (a.ndim - 1))
    args = (qx, ktx, vx, ox, gx, gtx, qc, ktc, vc, gc, gtc)
    return pl.pallas_call(
        functools.partial(_mlstm_kernel, t_x=t_x, t_c=t_c),
        grid=(bsz,),
        in_specs=[bspec(a) for a in args] + [pl.BlockSpec((1, vw), lambda b: (0, 0))],
        out_specs=pl.BlockSpec((1, t_x, vw), lambda b: (b, 0, 0)),
        out_shape=jax.ShapeDtypeStruct((bsz, t_x, vw), BF16),
        scratch_shapes=[
            pltpu.VMEM((t_x, vw), F32),
            pltpu.VMEM((t_x, vw), F32),
            pltpu.VMEM((2 * N_HEADS, DK, 2 * DV), F32),
            pltpu.VMEM((SUB, (t_x // SUB) * ng), F32),
            pltpu.VMEM((SUB, (t_c // SUB) * ng), F32),
        ],
        compiler_params=pltpu.CompilerParams(
            dimension_semantics=("arbitrary",),
            vmem_limit_bytes=_vmem_limit(52 * 1024 * 1024)),
        name="mlstm",
    )(*args, head_norm.reshape(1, vw))


def _staggered(n_items, stages):
    state = [None] * n_items
    for step in range(len(stages) + n_items - 1):
        for i in range(n_items):
            k = step - i
            if 0 <= k < len(stages):
                state[i] = stages[k](i, state[i])


def _out_ffn_kernel(x_ref, pool_ref, ml_ref, mod_ref, wo_ref, n2_ref, wu_ref,
                    wd_ref, nf_ref, out_ref):
    d = x_ref.shape[-1]
    tm = x_ref.shape[0]
    dff = wd_ref.shape[0]
    g1 = mod_ref[0, :, 2 * d:3 * d]
    sh2 = mod_ref[0, :, 3 * d:4 * d]
    gain2 = n2_ref[...] * (1.0 + mod_ref[0, :, 4 * d:5 * d])
    g2 = mod_ref[0, :, 5 * d:6 * d]
    pw = pool_ref.shape[-1]
    rows = [slice(r0, r0 + SUB) for r0 in range(0, tm, SUB)]

    def out_proj(i, _):
        r = rows[i]
        return _dot(pool_ref[r, :], wo_ref[0:pw, :]) + _dot(ml_ref[r, :], wo_ref[pw:, :])

    def norm2(i, mix):
        x1 = x_ref[rows[i], :] + g1 * mix
        return x1, (_rms(x1) * gain2 + sh2).astype(BF16)

    def up(i, st):
        x1, h2 = st
        return x1, _dot(h2, wu_ref[:, 0:dff]), _dot(h2, wu_ref[:, dff:2 * dff])

    def act(i, st):
        x1, gg, aa = st
        return x1, (_silu(gg) * aa).astype(BF16)

    def down(i, st):
        x1, a = st
        return x1, _dot(a, wd_ref[...])

    def final(i, st):
        x1, acc = st
        x2 = x1 + g2 * acc
        out_ref[rows[i], :] = _rms(x2) * nf_ref[...]

    _staggered(len(rows), [out_proj, norm2, up, act, down, final])


def _out_ffn(x2d, pool, ml, mod3, w_out, norm2, w_up, w_down, norm_f, tm, tpb):
    n, d = x2d.shape
    tok = lambda w: pl.BlockSpec((tm, w), lambda i: (i, 0))
    const = lambda *shape: pl.BlockSpec(shape, lambda i: (0,) * len(shape),
                                        pipeline_mode=pl.Buffered(1))
    return pl.pallas_call(
        _out_ffn_kernel,
        grid=(n // tm,),
        in_specs=[tok(d), tok(pool.shape[1]), tok(ml.shape[1]),
                  pl.BlockSpec((1, 1, mod3.shape[-1]), lambda i: (i // tpb, 0, 0)),
                  const(*w_out.shape), const(1, d), const(*w_up.shape),
                  const(*w_down.shape), const(1, d)],
        out_specs=tok(d),
        out_shape=jax.ShapeDtypeStruct((n, d), F32),
        compiler_params=pltpu.CompilerParams(
            dimension_semantics=("arbitrary",),
            vmem_limit_bytes=_vmem_limit(56 * 1024 * 1024)),
        name="out_ffn",
    )(x2d, pool, ml, mod3, w_out, norm2.reshape(1, d), w_up, w_down, norm_f.reshape(1, d))


def kernel(x, c, ctx, c_ctx, w_ada, b_ada, norm1, w_in, conv_qk, gate_bias, pool_w,
           pool_scale, head_norm, w_out, norm2, w_up, w_down, norm_f):
    bsz, t, d = x.shape
    t_c = ctx.shape[1]
    assert w_ada.shape[0] == 1, "single-layer block"
    pool_width = pool_w.shape[1] * pool_w.shape[2]
    qk_width = 2 * N_HEADS * DK
    ml_width = N_HEADS * DV
    o1 = pool_width
    o2 = o1 + qk_width
    o3 = o2 + ml_width
    o4 = o3 + ml_width
    cols = (0, o1, o2, o3, o4, o4 + N_GATES)
    tm = PROJ_TILE
    assert t % tm == 0 and (bsz * t_c) % tm == 0 and t_c % SUB == 0 and t % OUT_TILE == 0
    assert w_in.shape[2] == cols[-1] and w_up.shape[2] == 2 * w_down.shape[1]

    cc = jnp.concatenate([c, c_ctx[None, :]], axis=0)
    mod = _ada(cc, w_ada[0], b_ada[0])
    mod3 = mod.reshape(bsz + 1, 1, mod.shape[-1])

    w_in_b = w_in[0].astype(BF16)
    a_np, inv_np = _pool_consts()
    gd = pool_w.shape[2]
    pw = pool_w[0].astype(BF16)
    zero = jnp.zeros((gd, gd), BF16)
    pw_bd = jnp.stack([jnp.block([[pw[0], zero], [zero, pw[1]]]),
                       jnp.block([[pw[2], zero], [zero, pw[3]]])])
    pool_parts = (jnp.asarray(a_np, BF16), jnp.asarray(inv_np, F32), pw_bd, pool_scale[0])

    tpb = t // tm
    q_x, kt_x, v_x, g_x, gt_x, pool_x, o_x = _in_proj(
        x.reshape(bsz * t, d), t, mod3, lambda i: i // tpb, norm1[0], w_in_b, cols,
        gate_bias[0], conv_qk[0], pool_parts, tm)
    q_c, kt_c, v_c, g_c, gt_c = _in_proj(
        ctx.reshape(bsz * t_c, d), t_c, mod3, lambda i: bsz, norm1[0], w_in_b, cols,
        gate_bias[0], conv_qk[0], None, tm)

    def per_seq(a, tt):
        return a.reshape((bsz, tt) + a.shape[1:])

    def per_seq_chunks(a, tt, merge=False):
        nc = tt // SUB
        if merge:
            return a.reshape(bsz, nc * a.shape[1], a.shape[2])
        return a.reshape((bsz, nc) + a.shape[1:])

    ml = _mlstm(per_seq(q_x, t), per_seq_chunks(kt_x, t), per_seq(v_x, t), per_seq(o_x, t),
                g_x.reshape(bsz, t // SUB, SUB, N_GATES), per_seq_chunks(gt_x, t, merge=True),
                per_seq(q_c, t_c), per_seq_chunks(kt_c, t_c), per_seq(v_c, t_c),
                g_c.reshape(bsz, t_c // SUB, SUB, N_GATES), per_seq_chunks(gt_c, t_c, merge=True),
                head_norm[0])

    out = _out_ffn(x.reshape(bsz * t, d), pool_x, ml.reshape(bsz * t, ml_width), mod3,
                   w_out[0].astype(BF16), norm2[0], w_up[0].astype(BF16),
                   w_down[0].astype(BF16), norm_f, OUT_TILE, t // OUT_TILE)
    return out.reshape(bsz, t, d)
```

```python
import functools

import numpy as np
import jax
import jax.numpy as jnp
from jax import lax
from jax.experimental import pallas as pl
from jax.experimental.pallas import tpu as pltpu

F32 = jnp.float32
BF16 = jnp.bfloat16

EPS = 1e-6
GRID_W = 64
POOL_WINDOWS = (2, 4, 8, 16)
N_HEADS = 4
DK = 64
DV = 128
N_GATES = 4 * N_HEADS
CONV_W = 3
LOG2E = 1.4426950408889634

V7X_VMEM_BYTES = 64 * 1024 * 1024
V7X_MXU_DIM = 256
BF16_SUBLANES = 16

SUB = V7X_MXU_DIM
PROJ_TILE = 1024
OUT_TILE = 1024
HALO = BF16_SUBLANES


def _vmem_limit(est_bytes):
    return int(min(V7X_VMEM_BYTES - 6 * 1024 * 1024, est_bytes))


def _dot(a, b):
    return jnp.dot(a, b, preferred_element_type=F32)


def _sigmoid(x):
    return 1.0 / (1.0 + jnp.exp(-x))


def _silu(x):
    return x * _sigmoid(x)


def _log_sigmoid(x):
    return jnp.minimum(x, 0.0) - jnp.log(1.0 + jnp.exp(-jnp.abs(x)))


def _rms(x):
    return x * lax.rsqrt(jnp.mean(x * x, axis=-1, keepdims=True) + EPS)


def _ada_kernel(c_ref, w_ref, b_ref, o_ref):
    s = _silu(c_ref[...]).astype(BF16)
    o_ref[...] = _dot(s, w_ref[...].astype(BF16)) + b_ref[...]


def _ada(cc, w_ada, b_ada):
    rows, d = cc.shape
    n = w_ada.shape[1]
    bn = 1536
    return pl.pallas_call(
        _ada_kernel,
        grid=(n // bn,),
        in_specs=[
            pl.BlockSpec((rows, d), lambda j: (0, 0)),
            pl.BlockSpec((d, bn), lambda j: (0, j)),
            pl.BlockSpec((1, bn), lambda j: (0, j)),
        ],
        out_specs=pl.BlockSpec((rows, bn), lambda j: (0, j)),
        out_shape=jax.ShapeDtypeStruct((rows, n), F32),
        compiler_params=pltpu.CompilerParams(
            dimension_semantics=("arbitrary",),
            vmem_limit_bytes=_vmem_limit(40 * 1024 * 1024)),
        name="ada",
    )(cc, w_ada, b_ada.reshape(1, n))


def _pool_consts():
    pos = np.arange(SUB) % GRID_W
    row = np.arange(SUB) // GRID_W
    a = np.zeros((len(POOL_WINDOWS), SUB, SUB), np.float32)
    inv = np.zeros((SUB, len(POOL_WINDOWS)), np.float32)
    for gi, win in enumerate(POOL_WINDOWS):
        lo = np.clip(pos - win // 2, 0, GRID_W - 1)
        hi = np.clip(pos + win // 2 - 1, 0, GRID_W - 1)
        for j in range(SUB):
            a[gi, j, row[j] * GRID_W + lo[j]: row[j] * GRID_W + hi[j] + 1] = 1.0
        inv[:, gi] = 1.0 / (hi - lo + 1)
    return a, inv


def _proj_kernel(*refs, tm, seq_len, cols, with_pool):
    if with_pool:
        (x_ref, xp_ref, xn_ref, mod_ref, n1_ref, w_ref, gb_ref, cw_ref,
         a_ref, inv_ref, pw_ref, ps_ref,
         q_ref, kt_ref, v_ref, gt_ref, pool_ref, o_ref, lhs_scr) = refs
    else:
        (x_ref, xp_ref, xn_ref, mod_ref, n1_ref, w_ref, gb_ref, cw_ref,
         q_ref, kt_ref, v_ref, gt_ref, lhs_scr) = refs
    d = x_ref.shape[-1]
    sub = SUB
    n_sub = tm // sub
    c_pool, c_qk, c_v, c_o, c_g, c_end = cols
    sh = mod_ref[0, :, 0:d]
    gain = n1_ref[...] * (1.0 + mod_ref[0, :, d:2 * d])
    qw = q_ref.shape[-1]
    row = lax.broadcasted_iota(jnp.int32, (sub, 1), 0)

    def mod_norm(xv):
        return (_rms(xv) * gain + sh).astype(BF16)

    def conv_act(cur, prev_last, next_first, k):
        start = pl.program_id(0) * tm + k * sub
        prev_last = jnp.where(start % seq_len == 0, 0.0, prev_last)
        next_first = jnp.where((start + sub) % seq_len == 0, 0.0, next_first)
        dn = jnp.where(row == 0, prev_last, pltpu.roll(cur, 1, axis=0))
        up = jnp.where(row == sub - 1, next_first, pltpu.roll(cur, sub - 1, axis=0))
        act = _silu(cw_ref[0:1, :] * dn + cw_ref[1:2, :] * cur + cw_ref[2:3, :] * up)
        q_ref[k * sub:(k + 1) * sub, :] = (act[:, :qw] * (DK ** -0.5)).astype(BF16)
        kt_ref[k] = act[:, qw:].T.astype(BF16)

    def side_dots(hb, k):
        rows = slice(k * sub, (k + 1) * sub)
        v_ref[rows, :] = _dot(hb, w_ref[:, c_v:c_o]).astype(BF16)
        g = _dot(hb, w_ref[:, c_g:c_end]) + gb_ref[...]
        gt_ref[k] = g.T
        if with_pool:
            o_ref[rows, :] = _dot(hb, w_ref[:, c_o:c_g]).astype(BF16)
            return _dot(hb, w_ref[:, c_pool:c_qk])
        return None

    def pool_mix(u, k):
        gd = u.shape[1] // len(POOL_WINDOWS)
        parts = []
        for gi in range(len(POOL_WINDOWS)):
            ug = u[:, gi * gd:(gi + 1) * gd]
            win = _dot(a_ref[gi], ug.astype(BF16))
            parts.append(win * inv_ref[:, gi:gi + 1] - ug)
        dmat = jnp.concatenate(parts, axis=1).astype(BF16)
        half = dmat.shape[1] // 2
        out = jnp.concatenate(
            [_dot(dmat[:, :half], pw_ref[0]), _dot(dmat[:, half:], pw_ref[1])], axis=1)
        pool_ref[k * sub:(k + 1) * sub, :] = (out * ps_ref[...]).astype(BF16)

    ps, us = [None] * n_sub, [None] * n_sub

    def own_rows(k):
        lo = HALO if k == 0 else 0
        return lo, lo + sub

    def conv(k):
        lo, hi = own_rows(k)
        if k == 0:
            prev_last = ps[0][lo - 1:lo, :]
        else:
            prev_last = ps[k - 1][own_rows(k - 1)[1] - 1:own_rows(k - 1)[1], :]
        if k == n_sub - 1:
            next_first = ps[k][hi:hi + 1, :]
        else:
            next_first = ps[k + 1][own_rows(k + 1)[0]:own_rows(k + 1)[0] + 1, :]
        conv_act(ps[k][lo:hi, :], prev_last, next_first, k)

    for k in range(n_sub):
        r0 = HALO + k * sub
        lhs_scr[r0:r0 + sub, :] = mod_norm(x_ref[k * sub:(k + 1) * sub, :])
        lo, hi = r0, r0 + sub
        if k == 0:
            lhs_scr[0:HALO, :] = mod_norm(xp_ref[...])
            lo = 0
        if k == n_sub - 1:
            lhs_scr[hi:hi + HALO, :] = mod_norm(xn_ref[...])
            hi += HALO
        ps[k] = _dot(lhs_scr[lo:hi, :], w_ref[:, c_qk:c_v])
        if k > 0:
            conv(k - 1)
        us[k] = side_dots(lhs_scr[r0:r0 + sub, :], k)
        if k > 0 and with_pool:
            pool_mix(us[k - 1], k - 1)
    conv(n_sub - 1)
    if with_pool:
        pool_mix(us[n_sub - 1], n_sub - 1)


def _in_proj(x2, seq_len, mod3, mod_row_fn, norm1, w_in, cols, gate_bias, conv_w, pool_parts, tm):
    n, d = x2.shape
    with_pool = pool_parts is not None
    c_pool, c_qk, c_v, c_o, c_g, c_end = cols
    ng = c_end - c_g
    hq = (c_v - c_qk) // 2
    vw = c_o - c_v
    hpt = tm // HALO
    const = lambda *shape: pl.BlockSpec(shape, lambda i: (0,) * len(shape))
    tok = lambda w: pl.BlockSpec((tm, w), lambda i: (i, 0))
    chunked = lambda r: pl.BlockSpec((tm // SUB, r, SUB), lambda i: (i, 0, 0))
    in_specs = [tok(d),
                pl.BlockSpec((HALO, d), lambda i: (jnp.maximum(i * hpt - 1, 0), 0)),
                pl.BlockSpec((HALO, d), lambda i: (jnp.minimum((i + 1) * hpt, n // HALO - 1), 0)),
                pl.BlockSpec((1, 1, mod3.shape[-1]), lambda i: (mod_row_fn(i), 0, 0)),
                const(1, d), const(*w_in.shape), const(1, ng), const(*conv_w.shape)]
    args = [x2, x2, x2, mod3, norm1.reshape(1, d), w_in, gate_bias.reshape(1, ng), conv_w]
    out_specs = [tok(hq), chunked(hq), tok(vw), chunked(ng)]
    out_shape = [jax.ShapeDtypeStruct((n, hq), BF16),
                 jax.ShapeDtypeStruct((n // SUB, hq, SUB), BF16),
                 jax.ShapeDtypeStruct((n, vw), BF16),
                 jax.ShapeDtypeStruct((n // SUB, ng, SUB), F32)]
    if with_pool:
        a_c, inv_c, pw_bd, p_scale = pool_parts
        pool_w = c_qk - c_pool
        in_specs += [const(*a_c.shape), const(*inv_c.shape), const(*pw_bd.shape),
                     const(1, pool_w)]
        args += [a_c, inv_c, pw_bd, p_scale.reshape(1, pool_w)]
        out_specs += [tok(pool_w), tok(c_g - c_o)]
        out_shape += [jax.ShapeDtypeStruct((n, pool_w), BF16),
                      jax.ShapeDtypeStruct((n, c_g - c_o), BF16)]
    return pl.pallas_call(
        functools.partial(_proj_kernel, tm=tm, seq_len=seq_len, cols=cols, with_pool=with_pool),
        grid=(n // tm,),
        in_specs=in_specs,
        out_specs=out_specs,
        out_shape=out_shape,
        scratch_shapes=[pltpu.VMEM((tm + 2 * HALO, d), BF16)],
        compiler_params=pltpu.CompilerParams(
            dimension_semantics=("arbitrary",),
            vmem_limit_bytes=_vmem_limit(48 * 1024 * 1024)),
        name="in_proj_pool" if with_pool else "in_proj_ctx",
    )(*args)


def _split3(x):
    hi = x.astype(BF16)
    r1 = x - hi.astype(F32)
    mid = r1.astype(BF16)
    lo = (r1 - mid.astype(F32)).astype(BF16)
    return hi, mid, lo


def _tri_dot_left(tri, x):
    return sum(_dot(tri, p) for p in _split3(x))


def _tri_dot_right(x, tri):
    return sum(_dot(p, tri) for p in _split3(x))


def _gate_tables(g_col, g_row, tril, triu):
    lf_row = _log_sigmoid(g_row) * LOG2E
    lf_col = _log_sigmoid(g_col) * LOG2E
    brow = (_tri_dot_right(lf_row, triu), _tri_dot_right(lf_row, tril))
    bcol = (_tri_dot_left(tril, lf_col), _tri_dot_left(triu, lf_col))
    return g_row * LOG2E, brow, bcol


def _chain_step(q_c, kt_c, v_aug, bcol, arow, mask, last, ct_ref, m0, m_last, want_out):
    lc = kt_c.shape[1]
    ct = ct_ref[...]
    if want_out:
        am = jnp.where(mask, arow, -jnp.inf)
        cm = jnp.max(am, axis=1, keepdims=True)
        mcol = jnp.maximum(m0, jnp.broadcast_to(cm, (lc, DV)))
        b_r = jnp.broadcast_to(bcol, (lc, DV))
        e = jnp.exp2(am - jnp.concatenate([mcol, mcol], axis=1))
        p = (_dot(q_c, kt_c) * e).astype(BF16)
        wk = e[last:last + 1, :]
        qs = (q_c.astype(F32) * jnp.exp2(m0 - mcol)[:, :DK]).astype(BF16)
    else:
        wk = jnp.exp2(arow - m_last)
    kw = (kt_c.astype(F32) * wk).astype(BF16)
    if want_out:
        r = _dot(jnp.concatenate([p, kw], axis=0), v_aug)
        nd = r[:lc, :] + _dot(qs, ct.astype(BF16))
        hout = nd[:, :DV] / jnp.maximum(jnp.abs(nd[:, DV:]), jnp.exp2(-(b_r + mcol)))
        d_ct = r[lc:, :]
    else:
        hout = None
        d_ct = _dot(kw, v_aug)
    ct_ref[...] = jnp.exp2(m0 - m_last) * ct + d_ct
    return hout


def _mlstm_kernel(qx_ref, ktx_ref, vx_ref, ox_ref, gtx_ref,
                  qc_ref, ktc_ref, vc_ref, gtc_ref, hn_ref, out_ref,
                  hf_scr, hb_scr, st_scr, gpx_scr, gpc_scr, *, t_x, t_c):
    lc = SUB
    nh = N_HEADS
    ng = N_GATES
    ri = lax.broadcasted_iota(jnp.int32, (lc, lc), 0)
    ci = lax.broadcasted_iota(jnp.int32, (lc, lc), 1)
    lower = ci <= ri
    upper = ci >= ri
    tril = lower.astype(BF16)
    triu = upper.astype(BF16)
    ones_blk = jnp.ones((lc, DV), BF16)
    dirs = ((lower, lc - 1), (upper, 0))

    st_scr[...] = jnp.zeros(st_scr.shape, F32)

    def tables(gt_ref, gp_scr):
        for c in range(gt_ref.shape[1] // ng):
            gp_scr[:, c * ng:(c + 1) * ng] = gt_ref[0, c * ng:(c + 1) * ng, :].T
        return _gate_tables(gp_scr[...], gt_ref[0], tril, triu)

    def gate_slices(tabs, cidx, di, h):
        g_row, brow, bcol = tabs
        col_i = cidx * ng + di * nh + h
        col_f = col_i + 2 * nh
        arow = g_row[col_i:col_i + 1, :] - brow[di][col_f:col_f + 1, :]
        return arow, bcol[di][:, col_f:col_f + 1]

    def stabilisers(order, tabs, m_run):
        plan = []
        for i in range(len(order[0])):
            step = []
            for di in range(2):
                _, last = dirs[di]
                for h in range(nh):
                    arow, bcol = gate_slices(tabs, order[di][i], di, h)
                    m0 = m_run[di * nh + h]
                    m_last = jnp.maximum(m0, jnp.max(arow, axis=1, keepdims=True))
                    step.append((m0, m_last))
                    m_run[di * nh + h] = bcol[last:last + 1, :] + m_last
            plan.append(step)
        return plan

    def chunk_pair(cf, cb, q_ref, kt_ref, v_ref, tabs, ms, want_out):
        for di, cidx in enumerate((cf, cb)):
            mask, last = dirs[di]
            r0 = cidx * lc
            for h in range(nh):
                arow, bcol = gate_slices(tabs, cidx, di, h)
                q_c = q_ref[0, r0:r0 + lc, h * DK:(h + 1) * DK]
                kt_c = kt_ref[0, cidx, h * DK:(h + 1) * DK, :]
                v_c = v_ref[0, r0:r0 + lc, h * DV:(h + 1) * DV]
                v_aug = jnp.concatenate([v_c, ones_blk], axis=1)
                m0, m_last = ms[di * nh + h]
                hout = _chain_step(q_c, kt_c, v_aug, bcol, arow, mask, last,
                                   st_scr.at[di * nh + h], m0, m_last, want_out)
                if want_out:
                    dst = hf_scr if di == 0 else hb_scr
                    dst[r0:r0 + lc, h * DV:(h + 1) * DV] = hout

    m_run = [jnp.zeros((1, 1), F32) for _ in range(2 * nh)]
    nc_c = t_c // lc
    tabs_c = tables(gtc_ref, gpc_scr)
    order_c = (list(range(nc_c)), list(range(nc_c - 1, -1, -1)))
    plan_c = stabilisers(order_c, tabs_c, m_run)
    nc_x = t_x // lc
    tabs_x = tables(gtx_ref, gpx_scr)
    order_x = (list(range(nc_x)), list(range(nc_x - 1, -1, -1)))
    plan_x = stabilisers(order_x, tabs_x, m_run)
    for i in range(nc_c):
        chunk_pair(order_c[0][i], order_c[1][i], qc_ref, ktc_ref, vc_ref, tabs_c, plan_c[i],
                   False)
    for i in range(nc_x):
        pl.when(pl.program_id(0) >= 0)(functools.partial(
            chunk_pair, order_x[0][i], order_x[1][i], qx_ref, ktx_ref, vx_ref, tabs_x,
            plan_x[i], True))

    for r0 in range(0, t_x, lc):
        hm = hf_scr[r0:r0 + lc, :] + hb_scr[r0:r0 + lc, :]
        parts = [_rms(hm[:, h * DV:(h + 1) * DV]) for h in range(nh)]
        hnorm = jnp.concatenate(parts, axis=1)
        gate = _sigmoid(ox_ref[0, r0:r0 + lc, :].astype(F32))
        out_ref[0, r0:r0 + lc, :] = (hnorm * hn_ref[...] * gate).astype(BF16)


def _mlstm(qx, ktx, vx, ox, gtx, qc, ktc, vc, gtc, head_norm):
    bsz, t_x, _ = qx.shape
    t_c = qc.shape[1]
    vw = vx.shape[2]
    ng = N_GATES
    bspec = lambda a: pl.BlockSpec((1,) + a.shape[1:], lambda b: (b,) + (0,) * (a.ndim - 1))
    args = (qx, ktx, vx, ox, gtx, qc, ktc, vc, gtc)
    return pl.pallas_call(
        functools.partial(_mlstm_kernel, t_x=t_x, t_c=t_c),
        grid=(bsz,),
        in_specs=[bspec(a) for a in args] + [pl.BlockSpec((1, vw), lambda b: (0, 0))],
        out_specs=pl.BlockSpec((1, t_x, vw), lambda b: (b, 0, 0)),
        out_shape=jax.ShapeDtypeStruct((bsz, t_x, vw), BF16),
        scratch_shapes=[
            pltpu.VMEM((t_x, vw), F32),
            pltpu.VMEM((t_x, vw), F32),
            pltpu.VMEM((2 * N_HEADS, DK, 2 * DV), F32),
            pltpu.VMEM((SUB, (t_x // SUB) * ng), F32),
            pltpu.VMEM((SUB, (t_c // SUB) * ng), F32),
        ],
        compiler_params=pltpu.CompilerParams(
            dimension_semantics=("arbitrary",),
            vmem_limit_bytes=_vmem_limit(52 * 1024 * 1024)),
        name="mlstm",
    )(*args, head_norm.reshape(1, vw))


def _staggered(n_items, stages):
    state = [None] * n_items
    for step in range(len(stages) + n_items - 1):
        for i in range(n_items):
            k = step - i
            if 0 <= k < len(stages):
                state[i] = stages[k](i, state[i])


def _out_ffn_kernel(x_ref, pool_ref, ml_ref, mod_ref, wo_ref, n2_ref, wu_ref,
                    wd_ref, nf_ref, out_ref):
    d = x_ref.shape[-1]
    tm = x_ref.shape[0]
    dff = wd_ref.shape[0]
    g1 = mod_ref[0, :, 2 * d:3 * d]
    sh2 = mod_ref[0, :, 3 * d:4 * d]
    gain2 = n2_ref[...] * (1.0 + mod_ref[0, :, 4 * d:5 * d])
    g2 = mod_ref[0, :, 5 * d:6 * d]
    pw = pool_ref.shape[-1]
    rows = [slice(r0, r0 + SUB) for r0 in range(0, tm, SUB)]

    def out_proj(i, _):
        r = rows[i]
        return _dot(pool_ref[r, :], wo_ref[0:pw, :]) + _dot(ml_ref[r, :], wo_ref[pw:, :])

    def norm2(i, mix):
        x1 = x_ref[rows[i], :] + g1 * mix
        return x1, (_rms(x1) * gain2 + sh2).astype(BF16)

    def up(i, st):
        x1, h2 = st
        return x1, _dot(h2, wu_ref[:, 0:dff]), _dot(h2, wu_ref[:, dff:2 * dff])

    def act(i, st):
        x1, gg, aa = st
        return x1, (_silu(gg) * aa).astype(BF16)

    def down(i, st):
        x1, a = st
        return x1, _dot(a, wd_ref[...])

    def final(i, st):
        x1, acc = st
        x2 = x1 + g2 * acc
        out_ref[rows[i], :] = _rms(x2) * nf_ref[...]

    _staggered(len(rows), [out_proj, norm2, up, act, down, final])


def _out_ffn(x2d, pool, ml, mod3, w_out, norm2, w_up, w_down, norm_f, tm, tpb):
    n, d = x2d.shape
    tok = lambda w: pl.BlockSpec((tm, w), lambda i: (i, 0))
    const = lambda *shape: pl.BlockSpec(shape, lambda i: (0,) * len(shape),
                                        pipeline_mode=pl.Buffered(1))
    return pl.pallas_call(
        _out_ffn_kernel,
        grid=(n // tm,),
        in_specs=[tok(d), tok(pool.shape[1]), tok(ml.shape[1]),
                  pl.BlockSpec((1, 1, mod3.shape[-1]), lambda i: (i // tpb, 0, 0)),
                  const(*w_out.shape), const(1, d), const(*w_up.shape),
                  const(*w_down.shape), const(1, d)],
        out_specs=tok(d),
        out_shape=jax.ShapeDtypeStruct((n, d), F32),
        compiler_params=pltpu.CompilerParams(
            dimension_semantics=("arbitrary",),
            vmem_limit_bytes=_vmem_limit(56 * 1024 * 1024)),
        name="out_ffn",
    )(x2d, pool, ml, mod3, w_out, norm2.reshape(1, d), w_up, w_down, norm_f.reshape(1, d))


def kernel(x, c, ctx, c_ctx, w_ada, b_ada, norm1, w_in, conv_qk, gate_bias, pool_w,
           pool_scale, head_norm, w_out, norm2, w_up, w_down, norm_f):
    bsz, t, d = x.shape
    t_c = ctx.shape[1]
    assert w_ada.shape[0] == 1, "single-layer block"
    pool_width = pool_w.shape[1] * pool_w.shape[2]
    qk_width = 2 * N_HEADS * DK
    ml_width = N_HEADS * DV
    o1 = pool_width
    o2 = o1 + qk_width
    o3 = o2 + ml_width
    o4 = o3 + ml_width
    cols = (0, o1, o2, o3, o4, o4 + N_GATES)
    tm = PROJ_TILE
    assert t % tm == 0 and (bsz * t_c) % tm == 0 and t_c % SUB == 0 and t % OUT_TILE == 0
    assert w_in.shape[2] == cols[-1] and w_up.shape[2] == 2 * w_down.shape[1]

    cc = jnp.concatenate([c, c_ctx[None, :]], axis=0)
    mod = _ada(cc, w_ada[0], b_ada[0])
    mod3 = mod.reshape(bsz + 1, 1, mod.shape[-1])

    w_in_b = w_in[0].astype(BF16)
    a_np, inv_np = _pool_consts()
    gd = pool_w.shape[2]
    pw = pool_w[0].astype(BF16)
    zero = jnp.zeros((gd, gd), BF16)
    pw_bd = jnp.stack([jnp.block([[pw[0], zero], [zero, pw[1]]]),
                       jnp.block([[pw[2], zero], [zero, pw[3]]])])
    pool_parts = (jnp.asarray(a_np, BF16), jnp.asarray(inv_np, F32), pw_bd, pool_scale[0])

    tpb = t // tm
    q_x, kt_x, v_x, gt_x, pool_x, o_x = _in_proj(
        x.reshape(bsz * t, d), t, mod3, lambda i: i // tpb, norm1[0], w_in_b, cols,
        gate_bias[0], conv_qk[0], pool_parts, tm)
    q_c, kt_c, v_c, gt_c = _in_proj(
        ctx.reshape(bsz * t_c, d), t_c, mod3, lambda i: bsz, norm1[0], w_in_b, cols,
        gate_bias[0], conv_qk[0], None, tm)

    def per_seq(a, tt):
        return a.reshape((bsz, tt) + a.shape[1:])

    def per_seq_chunks(a, tt, merge=False):
        nc = tt // SUB
        if merge:
            return a.reshape(bsz, nc * a.shape[1], a.shape[2])
        return a.reshape((bsz, nc) + a.shape[1:])

    ml = _mlstm(per_seq(q_x, t), per_seq_chunks(kt_x, t), per_seq(v_x, t), per_seq(o_x, t),
                per_seq_chunks(gt_x, t, merge=True),
                per_seq(q_c, t_c), per_seq_chunks(kt_c, t_c), per_seq(v_c, t_c),
                per_seq_chunks(gt_c, t_c, merge=True), head_norm[0])

    out = _out_ffn(x.reshape(bsz * t, d), pool_x, ml.reshape(bsz * t, ml_width), mod3,
                   w_out[0].astype(BF16), norm2[0], w_up[0].astype(BF16),
                   w_down[0].astype(BF16), norm_f, OUT_TILE, t // OUT_TILE)
    return out.reshape(bsz, t, d)
```

```python
import functools

import numpy as np
import jax
import jax.numpy as jnp
from jax import lax
from jax.experimental import pallas as pl
from jax.experimental.pallas import tpu as pltpu

F32 = jnp.float32
BF16 = jnp.bfloat16

EPS = 1e-6
GRID_W = 64
POOL_WINDOWS = (2, 4, 8, 16)
N_HEADS = 4
DK = 64
DV = 128
N_GATES = 4 * N_HEADS
CONV_W = 3
LOG2E = 1.4426950408889634

V7X_VMEM_BYTES = 64 * 1024 * 1024
V7X_MXU_DIM = 256
BF16_SUBLANES = 16

SUB = V7X_MXU_DIM
PROJ_TILE = 1024
OUT_TILE = 1024
HALO = BF16_SUBLANES


def _vmem_limit(est_bytes):
    return int(min(V7X_VMEM_BYTES - 6 * 1024 * 1024, est_bytes))


def _dot(a, b):
    return jnp.dot(a, b, preferred_element_type=F32)


def _sigmoid(x):
    return 1.0 / (1.0 + jnp.exp(-x))


def _silu(x):
    return x * _sigmoid(x)


def _log_sigmoid(x):
    return jnp.minimum(x, 0.0) - jnp.log(1.0 + jnp.exp(-jnp.abs(x)))


def _rms(x):
    return x * lax.rsqrt(jnp.mean(x * x, axis=-1, keepdims=True) + EPS)


def _ada_kernel(c_ref, w_ref, b_ref, o_ref):
    s = _silu(c_ref[...]).astype(BF16)
    o_ref[...] = _dot(s, w_ref[...].astype(BF16)) + b_ref[...]


def _ada(cc, w_ada, b_ada):
    rows, d = cc.shape
    n = w_ada.shape[1]
    bn = 1536
    return pl.pallas_call(
        _ada_kernel,
        grid=(n // bn,),
        in_specs=[
            pl.BlockSpec((rows, d), lambda j: (0, 0)),
            pl.BlockSpec((d, bn), lambda j: (0, j)),
            pl.BlockSpec((1, bn), lambda j: (0, j)),
        ],
        out_specs=pl.BlockSpec((rows, bn), lambda j: (0, j)),
        out_shape=jax.ShapeDtypeStruct((rows, n), F32),
        compiler_params=pltpu.CompilerParams(
            dimension_semantics=("arbitrary",),
            vmem_limit_bytes=_vmem_limit(40 * 1024 * 1024)),
        name="ada",
    )(cc, w_ada, b_ada.reshape(1, n))


def _pool_consts():
    pos = np.arange(SUB) % GRID_W
    row = np.arange(SUB) // GRID_W
    a = np.zeros((len(POOL_WINDOWS), SUB, SUB), np.float32)
    inv = np.zeros((SUB, len(POOL_WINDOWS)), np.float32)
    for gi, win in enumerate(POOL_WINDOWS):
        lo = np.clip(pos - win // 2, 0, GRID_W - 1)
        hi = np.clip(pos + win // 2 - 1, 0, GRID_W - 1)
        for j in range(SUB):
            a[gi, j, row[j] * GRID_W + lo[j]: row[j] * GRID_W + hi[j] + 1] = 1.0
        inv[:, gi] = 1.0 / (hi - lo + 1)
    return a, inv


def _proj_kernel(*refs, tm, seq_len, cols, with_pool):
    if with_pool:
        (x_ref, xp_ref, xn_ref, mod_ref, n1_ref, w_ref, gb_ref, cw_ref,
         a_ref, inv_ref, pw_ref, ps_ref,
         q_ref, kt_ref, v_ref, g_ref, gt_ref, pool_ref, o_ref, lhs_scr) = refs
    else:
        (x_ref, xp_ref, xn_ref, mod_ref, n1_ref, w_ref, gb_ref, cw_ref,
         q_ref, kt_ref, v_ref, g_ref, gt_ref, lhs_scr) = refs
    d = x_ref.shape[-1]
    sub = SUB
    n_sub = tm // sub
    c_pool, c_qk, c_v, c_o, c_g, c_end = cols
    sh = mod_ref[0, :, 0:d]
    gain = n1_ref[...] * (1.0 + mod_ref[0, :, d:2 * d])
    qw = q_ref.shape[-1]
    row = lax.broadcasted_iota(jnp.int32, (sub, 1), 0)

    def mod_norm(xv):
        return (_rms(xv) * gain + sh).astype(BF16)

    def conv_act(cur, prev_last, next_first, k):
        start = pl.program_id(0) * tm + k * sub
        prev_last = jnp.where(start % seq_len == 0, 0.0, prev_last)
        next_first = jnp.where((start + sub) % seq_len == 0, 0.0, next_first)
        dn = jnp.where(row == 0, prev_last, pltpu.roll(cur, 1, axis=0))
        up = jnp.where(row == sub - 1, next_first, pltpu.roll(cur, sub - 1, axis=0))
        act = _silu(cw_ref[0:1, :] * dn + cw_ref[1:2, :] * cur + cw_ref[2:3, :] * up)
        q_ref[k * sub:(k + 1) * sub, :] = (act[:, :qw] * (DK ** -0.5)).astype(BF16)
        kt_ref[k] = act[:, qw:].T.astype(BF16)

    def side_dots(hb, k):
        rows = slice(k * sub, (k + 1) * sub)
        v_ref[rows, :] = _dot(hb, w_ref[:, c_v:c_o]).astype(BF16)
        g = _dot(hb, w_ref[:, c_g:c_end]) + gb_ref[...]
        g_ref[rows, :] = g
        gt_ref[k] = g.T
        if with_pool:
            o_ref[rows, :] = _dot(hb, w_ref[:, c_o:c_g]).astype(BF16)
            return _dot(hb, w_ref[:, c_pool:c_qk])
        return None

    def pool_mix(u, k):
        gd = u.shape[1] // len(POOL_WINDOWS)
        parts = []
        for gi in range(len(POOL_WINDOWS)):
            ug = u[:, gi * gd:(gi + 1) * gd]
            win = _dot(a_ref[gi], ug.astype(BF16))
            parts.append(win * inv_ref[:, gi:gi + 1] - ug)
        dmat = jnp.concatenate(parts, axis=1).astype(BF16)
        half = dmat.shape[1] // 2
        out = jnp.concatenate(
            [_dot(dmat[:, :half], pw_ref[0]), _dot(dmat[:, half:], pw_ref[1])], axis=1)
        pool_ref[k * sub:(k + 1) * sub, :] = (out * ps_ref[...]).astype(BF16)

    ps, us = [None] * n_sub, [None] * n_sub

    def own_rows(k):
        lo = HALO if k == 0 else 0
        return lo, lo + sub

    def conv(k):
        lo, hi = own_rows(k)
        if k == 0:
            prev_last = ps[0][lo - 1:lo, :]
        else:
            prev_last = ps[k - 1][own_rows(k - 1)[1] - 1:own_rows(k - 1)[1], :]
        if k == n_sub - 1:
            next_first = ps[k][hi:hi + 1, :]
        else:
            next_first = ps[k + 1][own_rows(k + 1)[0]:own_rows(k + 1)[0] + 1, :]
        conv_act(ps[k][lo:hi, :], prev_last, next_first, k)

    for k in range(n_sub):
        r0 = HALO + k * sub
        lhs_scr[r0:r0 + sub, :] = mod_norm(x_ref[k * sub:(k + 1) * sub, :])
        lo, hi = r0, r0 + sub
        if k == 0:
            lhs_scr[0:HALO, :] = mod_norm(xp_ref[...])
            lo = 0
        if k == n_sub - 1:
            lhs_scr[hi:hi + HALO, :] = mod_norm(xn_ref[...])
            hi += HALO
        ps[k] = _dot(lhs_scr[lo:hi, :], w_ref[:, c_qk:c_v])
        if k > 0:
            conv(k - 1)
        us[k] = side_dots(lhs_scr[r0:r0 + sub, :], k)
        if k > 0 and with_pool:
            pool_mix(us[k - 1], k - 1)
    conv(n_sub - 1)
    if with_pool:
        pool_mix(us[n_sub - 1], n_sub - 1)


def _in_proj(x2, seq_len, mod3, mod_row_fn, norm1, w_in, cols, gate_bias, conv_w, pool_parts, tm):
    n, d = x2.shape
    with_pool = pool_parts is not None
    c_pool, c_qk, c_v, c_o, c_g, c_end = cols
    ng = c_end - c_g
    hq = (c_v - c_qk) // 2
    vw = c_o - c_v
    hpt = tm // HALO
    const = lambda *shape: pl.BlockSpec(shape, lambda i: (0,) * len(shape))
    tok = lambda w: pl.BlockSpec((tm, w), lambda i: (i, 0))
    chunked = lambda r: pl.BlockSpec((tm // SUB, r, SUB), lambda i: (i, 0, 0))
    in_specs = [tok(d),
                pl.BlockSpec((HALO, d), lambda i: (jnp.maximum(i * hpt - 1, 0), 0)),
                pl.BlockSpec((HALO, d), lambda i: (jnp.minimum((i + 1) * hpt, n // HALO - 1), 0)),
                pl.BlockSpec((1, 1, mod3.shape[-1]), lambda i: (mod_row_fn(i), 0, 0)),
                const(1, d), const(*w_in.shape), const(1, ng), const(*conv_w.shape)]
    args = [x2, x2, x2, mod3, norm1.reshape(1, d), w_in, gate_bias.reshape(1, ng), conv_w]
    out_specs = [tok(hq), chunked(hq), tok(vw), tok(ng), chunked(ng)]
    out_shape = [jax.ShapeDtypeStruct((n, hq), BF16),
                 jax.ShapeDtypeStruct((n // SUB, hq, SUB), BF16),
                 jax.ShapeDtypeStruct((n, vw), BF16),
                 jax.ShapeDtypeStruct((n, ng), F32),
                 jax.ShapeDtypeStruct((n // SUB, ng, SUB), F32)]
    if with_pool:
        a_c, inv_c, pw_bd, p_scale = pool_parts
        pool_w = c_qk - c_pool
        in_specs += [const(*a_c.shape), const(*inv_c.shape), const(*pw_bd.shape),
                     const(1, pool_w)]
        args += [a_c, inv_c, pw_bd, p_scale.reshape(1, pool_w)]
        out_specs += [tok(pool_w), tok(c_g - c_o)]
        out_shape += [jax.ShapeDtypeStruct((n, pool_w), BF16),
                      jax.ShapeDtypeStruct((n, c_g - c_o), BF16)]
    return pl.pallas_call(
        functools.partial(_proj_kernel, tm=tm, seq_len=seq_len, cols=cols, with_pool=with_pool),
        grid=(n // tm,),
        in_specs=in_specs,
        out_specs=out_specs,
        out_shape=out_shape,
        scratch_shapes=[pltpu.VMEM((tm + 2 * HALO, d), BF16)],
        compiler_params=pltpu.CompilerParams(
            dimension_semantics=("arbitrary",),
            vmem_limit_bytes=_vmem_limit(48 * 1024 * 1024)),
        name="in_proj_pool" if with_pool else "in_proj_ctx",
    )(*args)


def _split3(x):
    hi = x.astype(BF16)
    r1 = x - hi.astype(F32)
    mid = r1.astype(BF16)
    lo = (r1 - mid.astype(F32)).astype(BF16)
    return hi, mid, lo


def _tri_dot_left(tri, x):
    return sum(_dot(tri, p) for p in _split3(x))


def _tri_dot_right(x, tri):
    return sum(_dot(p, tri) for p in _split3(x))


def _gate_tables(g_col, g_row, tril, triu):
    lf_row = _log_sigmoid(g_row) * LOG2E
    lf_col = _log_sigmoid(g_col) * LOG2E
    brow = (_tri_dot_right(lf_row, triu), _tri_dot_right(lf_row, tril))
    bcol = (_tri_dot_left(tril, lf_col), _tri_dot_left(triu, lf_col))
    return g_row * LOG2E, brow, bcol


def _chain_step(q_c, kt_c, v_aug, bcol, arow, mask, last, ct_ref, m0, m_last, want_out):
    lc = kt_c.shape[1]
    ct = ct_ref[...]
    if want_out:
        am = jnp.where(mask, arow, -jnp.inf)
        cm = jnp.max(am, axis=1, keepdims=True)
        mcol = jnp.maximum(m0, jnp.broadcast_to(cm, (lc, DV)))
        b_r = jnp.broadcast_to(bcol, (lc, DV))
        e = jnp.exp2(am - jnp.concatenate([mcol, mcol], axis=1))
        s = _dot(q_c, kt_c) * e
        den_intra = jnp.broadcast_to(jnp.sum(s, axis=1, keepdims=True), (lc, DV))
        p = s.astype(BF16)
        wk = e[last:last + 1, :]
        qs = (q_c.astype(F32) * jnp.exp2(m0 - mcol)[:, :DK]).astype(BF16)
    else:
        wk = jnp.exp2(arow - m_last)
    kw = (kt_c.astype(F32) * wk).astype(BF16)
    if want_out:
        r = _dot(jnp.concatenate([p, kw], axis=0), v_aug)
        r2 = _dot(qs, ct.astype(BF16))
        num = r[:lc, :DV] + r2[:, :DV]
        den = den_intra + r2[:, DV:]
        hout = num / jnp.maximum(jnp.abs(den), jnp.exp2(-(b_r + mcol)))
        d_ct = r[lc:, :]
    else:
        hout = None
        d_ct = _dot(kw, v_aug)
    ct_ref[...] = jnp.exp2(m0 - m_last) * ct + d_ct
    return hout


def _mlstm_kernel(qx_ref, ktx_ref, vx_ref, ox_ref, gx_ref, gtx_ref,
                  qc_ref, ktc_ref, vc_ref, gc_ref, gtc_ref, hn_ref, out_ref,
                  hf_scr, hb_scr, st_scr, gpx_scr, gpc_scr, *, t_x, t_c):
    lc = SUB
    nh = N_HEADS
    ng = N_GATES
    ri = lax.broadcasted_iota(jnp.int32, (lc, lc), 0)
    ci = lax.broadcasted_iota(jnp.int32, (lc, lc), 1)
    lower = ci <= ri
    upper = ci >= ri
    tril = lower.astype(BF16)
    triu = upper.astype(BF16)
    ones_blk = jnp.ones((lc, DV), BF16)
    dirs = ((lower, lc - 1), (upper, 0))

    st_scr[...] = jnp.zeros(st_scr.shape, F32)

    def tables(g_ref, gt_ref, gp_scr):
        for c in range(g_ref.shape[1]):
            gp_scr[:, c * ng:(c + 1) * ng] = g_ref[0, c]
        return _gate_tables(gp_scr[...], gt_ref[0], tril, triu)

    def gate_slices(tabs, cidx, di, h):
        g_row, brow, bcol = tabs
        col_i = cidx * ng + di * nh + h
        col_f = col_i + 2 * nh
        arow = g_row[col_i:col_i + 1, :] - brow[di][col_f:col_f + 1, :]
        return arow, bcol[di][:, col_f:col_f + 1]

    def stabilisers(order, tabs, m_run):
        plan = []
        for i in range(len(order[0])):
            step = []
            for di in range(2):
                _, last = dirs[di]
                for h in range(nh):
                    arow, bcol = gate_slices(tabs, order[di][i], di, h)
                    m0 = m_run[di * nh + h]
                    m_last = jnp.maximum(m0, jnp.max(arow, axis=1, keepdims=True))
                    step.append((m0, m_last))
                    m_run[di * nh + h] = bcol[last:last + 1, :] + m_last
            plan.append(step)
        return plan

    def chunk_pair(cf, cb, q_ref, kt_ref, v_ref, tabs, ms, want_out):
        for di, cidx in enumerate((cf, cb)):
            mask, last = dirs[di]
            r0 = cidx * lc
            for h in range(nh):
                arow, bcol = gate_slices(tabs, cidx, di, h)
                q_c = q_ref[0, r0:r0 + lc, h * DK:(h + 1) * DK]
                kt_c = kt_ref[0, cidx, h * DK:(h + 1) * DK, :]
                v_c = v_ref[0, r0:r0 + lc, h * DV:(h + 1) * DV]
                v_aug = jnp.concatenate([v_c, ones_blk], axis=1)
                m0, m_last = ms[di * nh + h]
                hout = _chain_step(q_c, kt_c, v_aug, bcol, arow, mask, last,
                                   st_scr.at[di * nh + h], m0, m_last, want_out)
                if want_out:
                    dst = hf_scr if di == 0 else hb_scr
                    dst[r0:r0 + lc, h * DV:(h + 1) * DV] = hout

    m_run = [jnp.zeros((1, 1), F32) for _ in range(2 * nh)]
    nc_c = t_c // lc
    tabs_c = tables(gc_ref, gtc_ref, gpc_scr)
    order_c = (list(range(nc_c)), list(range(nc_c - 1, -1, -1)))
    plan_c = stabilisers(order_c, tabs_c, m_run)
    nc_x = t_x // lc
    tabs_x = tables(gx_ref, gtx_ref, gpx_scr)
    order_x = (list(range(nc_x)), list(range(nc_x - 1, -1, -1)))
    plan_x = stabilisers(order_x, tabs_x, m_run)
    for i in range(nc_c):
        chunk_pair(order_c[0][i], order_c[1][i], qc_ref, ktc_ref, vc_ref, tabs_c, plan_c[i],
                   False)
    for i in range(nc_x):
        pl.when(pl.program_id(0) >= 0)(functools.partial(
            chunk_pair, order_x[0][i], order_x[1][i], qx_ref, ktx_ref, vx_ref, tabs_x,
            plan_x[i], True))

    for r0 in range(0, t_x, lc):
        hm = hf_scr[r0:r0 + lc, :] + hb_scr[r0:r0 + lc, :]
        parts = [_rms(hm[:, h * DV:(h + 1) * DV]) for h in range(nh)]
        hnorm = jnp.concatenate(parts, axis=1)
        gate = _sigmoid(ox_ref[0, r0:r0 + lc, :].astype(F32))
        out_ref[0, r0:r0 + lc, :] = (hnorm * hn_ref[...] * gate).astype(BF16)


def _mlstm(qx, ktx, vx, ox, gx, gtx, qc, ktc, vc, gc, gtc, head_norm):
    bsz, t_x, _ = qx.shape
    t_c = qc.shape[1]
    vw = vx.shape[2]
    ng = gx.shape[-1]
    bspec = lambda a: pl.BlockSpec((1,) + a.shape[1:], lambda b: (b,) + (0,) * (a.ndim - 1))
    args = (qx, ktx, vx, ox, gx, gtx, qc, ktc, vc, gc, gtc)
    return pl.pallas_call(
        functools.partial(_mlstm_kernel, t_x=t_x, t_c=t_c),
        grid=(bsz,),
        in_specs=[bspec(a) for a in args] + [pl.BlockSpec((1, vw), lambda b: (0, 0))],
        out_specs=pl.BlockSpec((1, t_x, vw), lambda b: (b, 0, 0)),
        out_shape=jax.ShapeDtypeStruct((bsz, t_x, vw), BF16),
        scratch_shapes=[
            pltpu.VMEM((t_x, vw), F32),
            pltpu.VMEM((t_x, vw), F32),
            pltpu.VMEM((2 * N_HEADS, DK, 2 * DV), F32),
            pltpu.VMEM((SUB, (t_x // SUB) * ng), F32),
            pltpu.VMEM((SUB, (t_c // SUB) * ng), F32),
        ],
        compiler_params=pltpu.CompilerParams(
            dimension_semantics=("arbitrary",),
            vmem_limit_bytes=_vmem_limit(58 * 1024 * 1024)),
        name="mlstm",
    )(*args, head_norm.reshape(1, vw))


def _staggered(n_items, stages):
    state = [None] * n_items
    for step in range(len(stages) + n_items - 1):
        for i in range(n_items):
            k = step - i
            if 0 <= k < len(stages):
                state[i] = stages[k](i, state[i])


def _out_ffn_kernel(x_ref, pool_ref, ml_ref, mod_ref, wo_ref, n2_ref, wu_ref,
                    wd_ref, nf_ref, out_ref):
    d = x_ref.shape[-1]
    tm = x_ref.shape[0]
    dff = wd_ref.shape[0]
    g1 = mod_ref[0, :, 2 * d:3 * d]
    sh2 = mod_ref[0, :, 3 * d:4 * d]
    gain2 = n2_ref[...] * (1.0 + mod_ref[0, :, 4 * d:5 * d])
    g2 = mod_ref[0, :, 5 * d:6 * d]
    pw = pool_ref.shape[-1]
    rows = [slice(r0, r0 + SUB) for r0 in range(0, tm, SUB)]

    def out_proj(i, _):
        r = rows[i]
        return _dot(pool_ref[r, :], wo_ref[0:pw, :]) + _dot(ml_ref[r, :], wo_ref[pw:, :])

    def norm2(i, mix):
        x1 = x_ref[rows[i], :] + g1 * mix
        return x1, (_rms(x1) * gain2 + sh2).astype(BF16)

    def up(i, st):
        x1, h2 = st
        return x1, _dot(h2, wu_ref[:, 0:dff]), _dot(h2, wu_ref[:, dff:2 * dff])

    def act(i, st):
        x1, gg, aa = st
        return x1, (_silu(gg) * aa).astype(BF16)

    def down(i, st):
        x1, a = st
        return x1, _dot(a, wd_ref[...])

    def final(i, st):
        x1, acc = st
        x2 = x1 + g2 * acc
        out_ref[rows[i], :] = _rms(x2) * nf_ref[...]

    _staggered(len(rows), [out_proj, norm2, up, act, down, final])


def _out_ffn(x2d, pool, ml, mod3, w_out, norm2, w_up, w_down, norm_f, tm, tpb):
    n, d = x2d.shape
    tok = lambda w: pl.BlockSpec((tm, w), lambda i: (i, 0))
    const = lambda *shape: pl.BlockSpec(shape, lambda i: (0,) * len(shape),
                                        pipeline_mode=pl.Buffered(1))
    return pl.pallas_call(
        _out_ffn_kernel,
        grid=(n // tm,),
        in_specs=[tok(d), tok(pool.shape[1]), tok(ml.shape[1]),
                  pl.BlockSpec((1, 1, mod3.shape[-1]), lambda i: (i // tpb, 0, 0)),
                  const(*w_out.shape), const(1, d), const(*w_up.shape),
                  const(*w_down.shape), const(1, d)],
        out_specs=tok(d),
        out_shape=jax.ShapeDtypeStruct((n, d), F32),
        compiler_params=pltpu.CompilerParams(
            dimension_semantics=("arbitrary",),
            vmem_limit_bytes=_vmem_limit(56 * 1024 * 1024)),
        name="out_ffn",
    )(x2d, pool, ml, mod3, w_out, norm2.reshape(1, d), w_up, w_down, norm_f.reshape(1, d))


def kernel(x, c, ctx, c_ctx, w_ada, b_ada, norm1, w_in, conv_qk, gate_bias, pool_w,
           pool_scale, head_norm, w_out, norm2, w_up, w_down, norm_f):
    bsz, t, d = x.shape
    t_c = ctx.shape[1]
    assert w_ada.shape[0] == 1, "single-layer block"
    pool_width = pool_w.shape[1] * pool_w.shape[2]
    qk_width = 2 * N_HEADS * DK
    ml_width = N_HEADS * DV
    o1 = pool_width
    o2 = o1 + qk_width
    o3 = o2 + ml_width
    o4 = o3 + ml_width
    cols = (0, o1, o2, o3, o4, o4 + N_GATES)
    tm = PROJ_TILE
    assert t % tm == 0 and (bsz * t_c) % tm == 0 and t_c % SUB == 0 and t % OUT_TILE == 0
    assert w_in.shape[2] == cols[-1] and w_up.shape[2] == 2 * w_down.shape[1]

    cc = jnp.concatenate([c, c_ctx[None, :]], axis=0)
    mod = _ada(cc, w_ada[0], b_ada[0])
    mod3 = mod.reshape(bsz + 1, 1, mod.shape[-1])

    w_in_b = w_in[0].astype(BF16)
    a_np, inv_np = _pool_consts()
    gd = pool_w.shape[2]
    pw = pool_w[0].astype(BF16)
    zero = jnp.zeros((gd, gd), BF16)
    pw_bd = jnp.stack([jnp.block([[pw[0], zero], [zero, pw[1]]]),
                       jnp.block([[pw[2], zero], [zero, pw[3]]])])
    pool_parts = (jnp.asarray(a_np, BF16), jnp.asarray(inv_np, F32), pw_bd, pool_scale[0])

    tpb = t // tm
    q_x, kt_x, v_x, g_x, gt_x, pool_x, o_x = _in_proj(
        x.reshape(bsz * t, d), t, mod3, lambda i: i // tpb, norm1[0], w_in_b, cols,
        gate_bias[0], conv_qk[0], pool_parts, tm)
    q_c, kt_c, v_c, g_c, gt_c = _in_proj(
        ctx.reshape(bsz * t_c, d), t_c, mod3, lambda i: bsz, norm1[0], w_in_b, cols,
        gate_bias[0], conv_qk[0], None, tm)

    def per_seq(a, tt):
        return a.reshape((bsz, tt) + a.shape[1:])

    def per_seq_chunks(a, tt, merge=False):
        nc = tt // SUB
        if merge:
            return a.reshape(bsz, nc * a.shape[1], a.shape[2])
        return a.reshape((bsz, nc) + a.shape[1:])

    ml = _mlstm(per_seq(q_x, t), per_seq_chunks(kt_x, t), per_seq(v_x, t), per_seq(o_x, t),
                g_x.reshape(bsz, t // SUB, SUB, N_GATES), per_seq_chunks(gt_x, t, merge=True),
                per_seq(q_c, t_c), per_seq_chunks(kt_c, t_c), per_seq(v_c, t_c),
                g_c.reshape(bsz, t_c // SUB, SUB, N_GATES), per_seq_chunks(gt_c, t_c, merge=True),
                head_norm[0])

    out = _out_ffn(x.reshape(bsz * t, d), pool_x, ml.reshape(bsz * t, ml_width), mod3,
                   w_out[0].astype(BF16), norm2[0], w_up[0].astype(BF16),
                   w_down[0].astype(BF16), norm_f, OUT_TILE, t // OUT_TILE)
    return out.reshape(bsz, t, d)
```

```python
import functools

import numpy as np
import jax
import jax.numpy as jnp
from jax import lax
from jax.experimental import pallas as pl
from jax.experimental.pallas import tpu as pltpu

F32 = jnp.float32
BF16 = jnp.bfloat16

EPS = 1e-6
GRID_W = 64
POOL_WINDOWS = (2, 4, 8, 16)
N_HEADS = 4
DK = 64
DV = 128
N_GATES = 4 * N_HEADS
CONV_W = 3
LOG2E = 1.4426950408889634

V7X_VMEM_BYTES = 64 * 1024 * 1024
V7X_MXU_DIM = 256
BF16_SUBLANES = 16

SUB = V7X_MXU_DIM
PROJ_TILE = 2048
OUT_TILE = 1024
HALO = BF16_SUBLANES


def _vmem_limit(est_bytes):
    return int(min(V7X_VMEM_BYTES - 6 * 1024 * 1024, est_bytes))


def _dot(a, b):
    return jnp.dot(a, b, preferred_element_type=F32)


def _sigmoid(x):
    return 1.0 / (1.0 + jnp.exp(-x))


def _silu(x):
    return x * _sigmoid(x)


def _log_sigmoid(x):
    return jnp.minimum(x, 0.0) - jnp.log(1.0 + jnp.exp(-jnp.abs(x)))


def _rms(x):
    return x * lax.rsqrt(jnp.mean(x * x, axis=-1, keepdims=True) + EPS)


def _ada_kernel(c_ref, w_ref, b_ref, o_ref):
    s = _silu(c_ref[...]).astype(BF16)
    o_ref[...] = _dot(s, w_ref[...].astype(BF16)) + b_ref[...]


def _ada(cc, w_ada, b_ada):
    rows, d = cc.shape
    n = w_ada.shape[1]
    bn = 1536
    return pl.pallas_call(
        _ada_kernel,
        grid=(n // bn,),
        in_specs=[
            pl.BlockSpec((rows, d), lambda j: (0, 0)),
            pl.BlockSpec((d, bn), lambda j: (0, j)),
            pl.BlockSpec((1, bn), lambda j: (0, j)),
        ],
        out_specs=pl.BlockSpec((rows, bn), lambda j: (0, j)),
        out_shape=jax.ShapeDtypeStruct((rows, n), F32),
        compiler_params=pltpu.CompilerParams(
            dimension_semantics=("arbitrary",),
            vmem_limit_bytes=_vmem_limit(40 * 1024 * 1024)),
        name="ada",
    )(cc, w_ada, b_ada.reshape(1, n))


def _pool_consts():
    pos = np.arange(SUB) % GRID_W
    row = np.arange(SUB) // GRID_W
    a = np.zeros((len(POOL_WINDOWS), SUB, SUB), np.float32)
    inv = np.zeros((SUB, len(POOL_WINDOWS)), np.float32)
    for gi, win in enumerate(POOL_WINDOWS):
        lo = np.clip(pos - win // 2, 0, GRID_W - 1)
        hi = np.clip(pos + win // 2 - 1, 0, GRID_W - 1)
        for j in range(SUB):
            a[gi, j, row[j] * GRID_W + lo[j]: row[j] * GRID_W + hi[j] + 1] = 1.0
        inv[:, gi] = 1.0 / (hi - lo + 1)
    return a, inv


def _proj_kernel(*refs, tm, seq_len, cols, with_pool):
    if with_pool:
        (x_ref, xp_ref, xn_ref, mod_ref, n1_ref, w_ref, gb_ref, cw_ref,
         a_ref, inv_ref, pw_ref, ps_ref,
         q_ref, kt_ref, v_ref, g_ref, gt_ref, pool_ref, o_ref, lhs_scr) = refs
    else:
        (x_ref, xp_ref, xn_ref, mod_ref, n1_ref, w_ref, gb_ref, cw_ref,
         q_ref, kt_ref, v_ref, g_ref, gt_ref, lhs_scr) = refs
    d = x_ref.shape[-1]
    sub = SUB
    n_sub = tm // sub
    c_pool, c_qk, c_v, c_o, c_g, c_end = cols
    sh = mod_ref[0, :, 0:d]
    gain = n1_ref[...] * (1.0 + mod_ref[0, :, d:2 * d])
    qw = q_ref.shape[-1]
    row = lax.broadcasted_iota(jnp.int32, (sub, 1), 0)

    def mod_norm(xv):
        return (_rms(xv) * gain + sh).astype(BF16)

    def conv_act(cur, prev_last, next_first, k):
        start = pl.program_id(0) * tm + k * sub
        prev_last = jnp.where(start % seq_len == 0, 0.0, prev_last)
        next_first = jnp.where((start + sub) % seq_len == 0, 0.0, next_first)
        dn = jnp.where(row == 0, prev_last, pltpu.roll(cur, 1, axis=0))
        up = jnp.where(row == sub - 1, next_first, pltpu.roll(cur, sub - 1, axis=0))
        act = _silu(cw_ref[0:1, :] * dn + cw_ref[1:2, :] * cur + cw_ref[2:3, :] * up)
        q_ref[k * sub:(k + 1) * sub, :] = (act[:, :qw] * (DK ** -0.5)).astype(BF16)
        kt_ref[k] = act[:, qw:].T.astype(BF16)

    def side_dots(hb, k):
        rows = slice(k * sub, (k + 1) * sub)
        v_ref[rows, :] = _dot(hb, w_ref[:, c_v:c_o]).astype(BF16)
        g = _dot(hb, w_ref[:, c_g:c_end]) + gb_ref[...]
        g_ref[rows, :] = g
        gt_ref[k] = g.T
        if with_pool:
            o_ref[rows, :] = _dot(hb, w_ref[:, c_o:c_g]).astype(BF16)
            return _dot(hb, w_ref[:, c_pool:c_qk])
        return None

    def pool_mix(u, k):
        gd = u.shape[1] // len(POOL_WINDOWS)
        parts = []
        for gi in range(len(POOL_WINDOWS)):
            ug = u[:, gi * gd:(gi + 1) * gd]
            win = _dot(a_ref[gi], ug.astype(BF16))
            parts.append(win * inv_ref[:, gi:gi + 1] - ug)
        dmat = jnp.concatenate(parts, axis=1).astype(BF16)
        half = dmat.shape[1] // 2
        out = jnp.concatenate(
            [_dot(dmat[:, :half], pw_ref[0]), _dot(dmat[:, half:], pw_ref[1])], axis=1)
        pool_ref[k * sub:(k + 1) * sub, :] = (out * ps_ref[...]).astype(BF16)

    ps, us = [None] * n_sub, [None] * n_sub

    def own_rows(k):
        lo = HALO if k == 0 else 0
        return lo, lo + sub

    def conv(k):
        lo, hi = own_rows(k)
        if k == 0:
            prev_last = ps[0][lo - 1:lo, :]
        else:
            prev_last = ps[k - 1][own_rows(k - 1)[1] - 1:own_rows(k - 1)[1], :]
        if k == n_sub - 1:
            next_first = ps[k][hi:hi + 1, :]
        else:
            next_first = ps[k + 1][own_rows(k + 1)[0]:own_rows(k + 1)[0] + 1, :]
        conv_act(ps[k][lo:hi, :], prev_last, next_first, k)

    for k in range(n_sub):
        r0 = HALO + k * sub
        lhs_scr[r0:r0 + sub, :] = mod_norm(x_ref[k * sub:(k + 1) * sub, :])
        lo, hi = r0, r0 + sub
        if k == 0:
            lhs_scr[0:HALO, :] = mod_norm(xp_ref[...])
            lo = 0
        if k == n_sub - 1:
            lhs_scr[hi:hi + HALO, :] = mod_norm(xn_ref[...])
            hi += HALO
        ps[k] = _dot(lhs_scr[lo:hi, :], w_ref[:, c_qk:c_v])
        if k > 0:
            conv(k - 1)
        us[k] = side_dots(lhs_scr[r0:r0 + sub, :], k)
        if k > 0 and with_pool:
            pool_mix(us[k - 1], k - 1)
    conv(n_sub - 1)
    if with_pool:
        pool_mix(us[n_sub - 1], n_sub - 1)


def _in_proj(x2, seq_len, mod3, mod_row_fn, norm1, w_in, cols, gate_bias, conv_w, pool_parts, tm):
    n, d = x2.shape
    with_pool = pool_parts is not None
    c_pool, c_qk, c_v, c_o, c_g, c_end = cols
    ng = c_end - c_g
    hq = (c_v - c_qk) // 2
    vw = c_o - c_v
    hpt = tm // HALO
    const = lambda *shape: pl.BlockSpec(shape, lambda i: (0,) * len(shape))
    tok = lambda w: pl.BlockSpec((tm, w), lambda i: (i, 0))
    chunked = lambda r: pl.BlockSpec((tm // SUB, r, SUB), lambda i: (i, 0, 0))
    in_specs = [tok(d),
                pl.BlockSpec((HALO, d), lambda i: (jnp.maximum(i * hpt - 1, 0), 0)),
                pl.BlockSpec((HALO, d), lambda i: (jnp.minimum((i + 1) * hpt, n // HALO - 1), 0)),
                pl.BlockSpec((1, 1, mod3.shape[-1]), lambda i: (mod_row_fn(i), 0, 0)),
                const(1, d), const(*w_in.shape), const(1, ng), const(*conv_w.shape)]
    args = [x2, x2, x2, mod3, norm1.reshape(1, d), w_in, gate_bias.reshape(1, ng), conv_w]
    out_specs = [tok(hq), chunked(hq), tok(vw), tok(ng), chunked(ng)]
    out_shape = [jax.ShapeDtypeStruct((n, hq), BF16),
                 jax.ShapeDtypeStruct((n // SUB, hq, SUB), BF16),
                 jax.ShapeDtypeStruct((n, vw), BF16),
                 jax.ShapeDtypeStruct((n, ng), F32),
                 jax.ShapeDtypeStruct((n // SUB, ng, SUB), F32)]
    if with_pool:
        a_c, inv_c, pw_bd, p_scale = pool_parts
        pool_w = c_qk - c_pool
        in_specs += [const(*a_c.shape), const(*inv_c.shape), const(*pw_bd.shape),
                     const(1, pool_w)]
        args += [a_c, inv_c, pw_bd, p_scale.reshape(1, pool_w)]
        out_specs += [tok(pool_w), tok(c_g - c_o)]
        out_shape += [jax.ShapeDtypeStruct((n, pool_w), BF16),
                      jax.ShapeDtypeStruct((n, c_g - c_o), BF16)]
    return pl.pallas_call(
        functools.partial(_proj_kernel, tm=tm, seq_len=seq_len, cols=cols, with_pool=with_pool),
        grid=(n // tm,),
        in_specs=in_specs,
        out_specs=out_specs,
        out_shape=out_shape,
        scratch_shapes=[pltpu.VMEM((tm + 2 * HALO, d), BF16)],
        compiler_params=pltpu.CompilerParams(
            dimension_semantics=("arbitrary",),
            vmem_limit_bytes=_vmem_limit(48 * 1024 * 1024)),
        name="in_proj_pool" if with_pool else "in_proj_ctx",
    )(*args)


def _split3(x):
    hi = x.astype(BF16)
    r1 = x - hi.astype(F32)
    mid = r1.astype(BF16)
    lo = (r1 - mid.astype(F32)).astype(BF16)
    return hi, mid, lo


def _tri_dot_left(tri, x):
    return sum(_dot(tri, p) for p in _split3(x))


def _tri_dot_right(x, tri):
    return sum(_dot(p, tri) for p in _split3(x))


def _gate_tables(g_col, g_row, tril, triu):
    lf_row = _log_sigmoid(g_row) * LOG2E
    lf_col = _log_sigmoid(g_col) * LOG2E
    brow = (_tri_dot_right(lf_row, triu), _tri_dot_right(lf_row, tril))
    bcol = (_tri_dot_left(tril, lf_col), _tri_dot_left(triu, lf_col))
    return g_row * LOG2E, brow, bcol


def _chain_step(q_c, kt_c, v_aug, bcol, arow, mask, last, ct_ref, m0, m_last, want_out):
    lc = kt_c.shape[1]
    ct = ct_ref[...]
    if want_out:
        am = jnp.where(mask, arow, -jnp.inf)
        cm = jnp.max(am, axis=1, keepdims=True)
        mcol = jnp.maximum(m0, jnp.broadcast_to(cm, (lc, DV)))
        b_r = jnp.broadcast_to(bcol, (lc, DV))
        e = jnp.exp2(am - jnp.concatenate([mcol, mcol], axis=1))
        s = _dot(q_c, kt_c) * e
        den_intra = jnp.broadcast_to(jnp.sum(s, axis=1, keepdims=True), (lc, DV))
        p = s.astype(BF16)
        wk = e[last:last + 1, :]
        qs = (q_c.astype(F32) * jnp.exp2(m0 - mcol)[:, :DK]).astype(BF16)
    else:
        wk = jnp.exp2(arow - m_last)
    kw = (kt_c.astype(F32) * wk).astype(BF16)
    if want_out:
        r = _dot(jnp.concatenate([p, kw], axis=0), v_aug)
        r2 = _dot(qs, ct.astype(BF16))
        num = r[:lc, :DV] + r2[:, :DV]
        den = den_intra + r2[:, DV:]
        hout = num / jnp.maximum(jnp.abs(den), jnp.exp2(-(b_r + mcol)))
        d_ct = r[lc:, :]
    else:
        hout = None
        d_ct = _dot(kw, v_aug)
    ct_ref[...] = jnp.exp2(m0 - m_last) * ct + d_ct
    return hout


def _mlstm_kernel(qx_ref, ktx_ref, vx_ref, ox_ref, gx_ref, gtx_ref,
                  qc_ref, ktc_ref, vc_ref, gc_ref, gtc_ref, hn_ref, out_ref,
                  hf_scr, hb_scr, st_scr, gpx_scr, gpc_scr, *, t_x, t_c):
    lc = SUB
    nh = N_HEADS
    ng = N_GATES
    ri = lax.broadcasted_iota(jnp.int32, (lc, lc), 0)
    ci = lax.broadcasted_iota(jnp.int32, (lc, lc), 1)
    lower = ci <= ri
    upper = ci >= ri
    tril = lower.astype(BF16)
    triu = upper.astype(BF16)
    ones_blk = jnp.ones((lc, DV), BF16)
    dirs = ((lower, lc - 1), (upper, 0))

    st_scr[...] = jnp.zeros(st_scr.shape, F32)

    def tables(g_ref, gt_ref, gp_scr):
        for c in range(g_ref.shape[1]):
            gp_scr[:, c * ng:(c + 1) * ng] = g_ref[0, c]
        return _gate_tables(gp_scr[...], gt_ref[0], tril, triu)

    def gate_slices(tabs, cidx, di, h):
        g_row, brow, bcol = tabs
        col_i = cidx * ng + di * nh + h
        col_f = col_i + 2 * nh
        arow = g_row[col_i:col_i + 1, :] - brow[di][col_f:col_f + 1, :]
        return arow, bcol[di][:, col_f:col_f + 1]

    def stabilisers(order, tabs, m_run):
        plan = []
        for i in range(len(order[0])):
            step = []
            for di in range(2):
                _, last = dirs[di]
                for h in range(nh):
                    arow, bcol = gate_slices(tabs, order[di][i], di, h)
                    m0 = m_run[di * nh + h]
                    m_last = jnp.maximum(m0, jnp.max(arow, axis=1, keepdims=True))
                    step.append((m0, m_last))
                    m_run[di * nh + h] = bcol[last:last + 1, :] + m_last
            plan.append(step)
        return plan

    def chunk_pair(cf, cb, q_ref, kt_ref, v_ref, tabs, ms, want_out):
        for di, cidx in enumerate((cf, cb)):
            mask, last = dirs[di]
            r0 = cidx * lc
            for h in range(nh):
                arow, bcol = gate_slices(tabs, cidx, di, h)
                q_c = q_ref[0, r0:r0 + lc, h * DK:(h + 1) * DK]
                kt_c = kt_ref[0, cidx, h * DK:(h + 1) * DK, :]
                v_c = v_ref[0, r0:r0 + lc, h * DV:(h + 1) * DV]
                v_aug = jnp.concatenate([v_c, ones_blk], axis=1)
                m0, m_last = ms[di * nh + h]
                hout = _chain_step(q_c, kt_c, v_aug, bcol, arow, mask, last,
                                   st_scr.at[di * nh + h], m0, m_last, want_out)
                if want_out:
                    dst = hf_scr if di == 0 else hb_scr
                    dst[r0:r0 + lc, h * DV:(h + 1) * DV] = hout

    m_run = [jnp.zeros((1, 1), F32) for _ in range(2 * nh)]
    nc_c = t_c // lc
    tabs_c = tables(gc_ref, gtc_ref, gpc_scr)
    order_c = (list(range(nc_c)), list(range(nc_c - 1, -1, -1)))
    plan_c = stabilisers(order_c, tabs_c, m_run)
    nc_x = t_x // lc
    tabs_x = tables(gx_ref, gtx_ref, gpx_scr)
    order_x = (list(range(nc_x)), list(range(nc_x - 1, -1, -1)))
    plan_x = stabilisers(order_x, tabs_x, m_run)
    for i in range(nc_c):
        chunk_pair(order_c[0][i], order_c[1][i], qc_ref, ktc_ref, vc_ref, tabs_c, plan_c[i],
                   False)
    for i in range(nc_x):
        pl.when(pl.program_id(0) >= 0)(functools.partial(
            chunk_pair, order_x[0][i], order_x[1][i], qx_ref, ktx_ref, vx_ref, tabs_x,
            plan_x[i], True))

    for r0 in range(0, t_x, lc):
        hm = hf_scr[r0:r0 + lc, :] + hb_scr[r0:r0 + lc, :]
        parts = [_rms(hm[:, h * DV:(h + 1) * DV]) for h in range(nh)]
        hnorm = jnp.concatenate(parts, axis=1)
        gate = _sigmoid(ox_ref[0, r0:r0 + lc, :].astype(F32))
        out_ref[0, r0:r0 + lc, :] = (hnorm * hn_ref[...] * gate).astype(BF16)


def _mlstm(qx, ktx, vx, ox, gx, gtx, qc, ktc, vc, gc, gtc, head_norm):
    bsz, t_x, _ = qx.shape
    t_c = qc.shape[1]
    vw = vx.shape[2]
    ng = gx.shape[-1]
    bspec = lambda a: pl.BlockSpec((1,) + a.shape[1:], lambda b: (b,) + (0,) * (a.ndim - 1))
    args = (qx, ktx, vx, ox, gx, gtx, qc, ktc, vc, gc, gtc)
    return pl.pallas_call(
        functools.partial(_mlstm_kernel, t_x=t_x, t_c=t_c),
        grid=(bsz,),
        in_specs=[bspec(a) for a in args] + [pl.BlockSpec((1, vw), lambda b: (0, 0))],
        out_specs=pl.BlockSpec((1, t_x, vw), lambda b: (b, 0, 0)),
        out_shape=jax.ShapeDtypeStruct((bsz, t_x, vw), BF16),
        scratch_shapes=[
            pltpu.VMEM((t_x, vw), F32),
            pltpu.VMEM((t_x, vw), F32),
            pltpu.VMEM((2 * N_HEADS, DK, 2 * DV), F32),
            pltpu.VMEM((SUB, (t_x // SUB) * ng), F32),
            pltpu.VMEM((SUB, (t_c // SUB) * ng), F32),
        ],
        compiler_params=pltpu.CompilerParams(
            dimension_semantics=("arbitrary",),
            vmem_limit_bytes=_vmem_limit(58 * 1024 * 1024)),
        name="mlstm",
    )(*args, head_norm.reshape(1, vw))


def _staggered(n_items, stages):
    state = [None] * n_items
    for step in range(len(stages) + n_items - 1):
        for i in range(n_items):
            k = step - i
            if 0 <= k < len(stages):
                state[i] = stages[k](i, state[i])


def _out_ffn_kernel(x_ref, pool_ref, ml_ref, mod_ref, wo_ref, n2_ref, wu_ref,
                    wd_ref, nf_ref, out_ref):
    d = x_ref.shape[-1]
    tm = x_ref.shape[0]
    dff = wd_ref.shape[0]
    g1 = mod_ref[0, :, 2 * d:3 * d]
    sh2 = mod_ref[0, :, 3 * d:4 * d]
    gain2 = n2_ref[...] * (1.0 + mod_ref[0, :, 4 * d:5 * d])
    g2 = mod_ref[0, :, 5 * d:6 * d]
    pw = pool_ref.shape[-1]
    rows = [slice(r0, r0 + SUB) for r0 in range(0, tm, SUB)]

    def out_proj(i, _):
        r = rows[i]
        return _dot(pool_ref[r, :], wo_ref[0:pw, :]) + _dot(ml_ref[r, :], wo_ref[pw:, :])

    def norm2(i, mix):
        x1 = x_ref[rows[i], :] + g1 * mix
        return x1, (_rms(x1) * gain2 + sh2).astype(BF16)

    def up(i, st):
        x1, h2 = st
        return x1, _dot(h2, wu_ref[:, 0:dff]), _dot(h2, wu_ref[:, dff:2 * dff])

    def act(i, st):
        x1, gg, aa = st
        return x1, (_silu(gg) * aa).astype(BF16)

    def down(i, st):
        x1, a = st
        return x1, _dot(a, wd_ref[...])

    def final(i, st):
        x1, acc = st
        x2 = x1 + g2 * acc
        out_ref[rows[i], :] = _rms(x2) * nf_ref[...]

    _staggered(len(rows), [out_proj, norm2, up, act, down, final])


def _out_ffn(x2d, pool, ml, mod3, w_out, norm2, w_up, w_down, norm_f, tm, tpb):
    n, d = x2d.shape
    tok = lambda w: pl.BlockSpec((tm, w), lambda i: (i, 0))
    const = lambda *shape: pl.BlockSpec(shape, lambda i: (0,) * len(shape),
                                        pipeline_mode=pl.Buffered(1))
    return pl.pallas_call(
        _out_ffn_kernel,
        grid=(n // tm,),
        in_specs=[tok(d), tok(pool.shape[1]), tok(ml.shape[1]),
                  pl.BlockSpec((1, 1, mod3.shape[-1]), lambda i: (i // tpb, 0, 0)),
                  const(*w_out.shape), const(1, d), const(*w_up.shape),
                  const(*w_down.shape), const(1, d)],
        out_specs=tok(d),
        out_shape=jax.ShapeDtypeStruct((n, d), F32),
        compiler_params=pltpu.CompilerParams(
            dimension_semantics=("arbitrary",),
            vmem_limit_bytes=_vmem_limit(56 * 1024 * 1024)),
        name="out_ffn",
    )(x2d, pool, ml, mod3, w_out, norm2.reshape(1, d), w_up, w_down, norm_f.reshape(1, d))


def kernel(x, c, ctx, c_ctx, w_ada, b_ada, norm1, w_in, conv_qk, gate_bias, pool_w,
           pool_scale, head_norm, w_out, norm2, w_up, w_down, norm_f):
    bsz, t, d = x.shape
    t_c = ctx.shape[1]
    assert w_ada.shape[0] == 1, "single-layer block"
    pool_width = pool_w.shape[1] * pool_w.shape[2]
    qk_width = 2 * N_HEADS * DK
    ml_width = N_HEADS * DV
    o1 = pool_width
    o2 = o1 + qk_width
    o3 = o2 + ml_width
    o4 = o3 + ml_width
    cols = (0, o1, o2, o3, o4, o4 + N_GATES)
    tm = PROJ_TILE
    assert t % tm == 0 and (bsz * t_c) % tm == 0 and t_c % SUB == 0 and t % OUT_TILE == 0
    assert w_in.shape[2] == cols[-1] and w_up.shape[2] == 2 * w_down.shape[1]

    cc = jnp.concatenate([c, c_ctx[None, :]], axis=0)
    mod = _ada(cc, w_ada[0], b_ada[0])
    mod3 = mod.reshape(bsz + 1, 1, mod.shape[-1])

    w_in_b = w_in[0].astype(BF16)
    a_np, inv_np = _pool_consts()
    gd = pool_w.shape[2]
    pw = pool_w[0].astype(BF16)
    zero = jnp.zeros((gd, gd), BF16)
    pw_bd = jnp.stack([jnp.block([[pw[0], zero], [zero, pw[1]]]),
                       jnp.block([[pw[2], zero], [zero, pw[3]]])])
    pool_parts = (jnp.asarray(a_np, BF16), jnp.asarray(inv_np, F32), pw_bd, pool_scale[0])

    tpb = t // tm
    q_x, kt_x, v_x, g_x, gt_x, pool_x, o_x = _in_proj(
        x.reshape(bsz * t, d), t, mod3, lambda i: i // tpb, norm1[0], w_in_b, cols,
        gate_bias[0], conv_qk[0], pool_parts, tm)
    q_c, kt_c, v_c, g_c, gt_c = _in_proj(
        ctx.reshape(bsz * t_c, d), t_c, mod3, lambda i: bsz, norm1[0], w_in_b, cols,
        gate_bias[0], conv_qk[0], None, tm)

    def per_seq(a, tt):
        return a.reshape((bsz, tt) + a.shape[1:])

    def per_seq_chunks(a, tt, merge=False):
        nc = tt // SUB
        if merge:
            return a.reshape(bsz, nc * a.shape[1], a.shape[2])
        return a.reshape((bsz, nc) + a.shape[1:])

    ml = _mlstm(per_seq(q_x, t), per_seq_chunks(kt_x, t), per_seq(v_x, t), per_seq(o_x, t),
                g_x.reshape(bsz, t // SUB, SUB, N_GATES), per_seq_chunks(gt_x, t, merge=True),
                per_seq(q_c, t_c), per_seq_chunks(kt_c, t_c), per_seq(v_c, t_c),
                g_c.reshape(bsz, t_c // SUB, SUB, N_GATES), per_seq_chunks(gt_c, t_c, merge=True),
                head_norm[0])

    out = _out_ffn(x.reshape(bsz * t, d), pool_x, ml.reshape(bsz * t, ml_width), mod3,
                   w_out[0].astype(BF16), norm2[0], w_up[0].astype(BF16),
                   w_down[0].astype(BF16), norm_f, OUT_TILE, t // OUT_TILE)
    return out.reshape(bsz, t, d)
```

```python
import functools

import numpy as np
import jax
import jax.numpy as jnp
from jax import lax
from jax.experimental import pallas as pl
from jax.experimental.pallas import tpu as pltpu

F32 = jnp.float32
BF16 = jnp.bfloat16

EPS = 1e-6
GRID_W = 64
POOL_WINDOWS = (2, 4, 8, 16)
N_HEADS = 4
DK = 64
DV = 128
N_GATES = 4 * N_HEADS
CONV_W = 3
LOG2E = 1.4426950408889634

V7X_VMEM_BYTES = 64 * 1024 * 1024
V7X_MXU_DIM = 256
BF16_SUBLANES = 16

SUB = V7X_MXU_DIM
PROJ_TILE = 2048
OUT_TILE = 1024
HALO = BF16_SUBLANES


def _vmem_limit(est_bytes):
    return int(min(V7X_VMEM_BYTES - 6 * 1024 * 1024, est_bytes))


def _dot(a, b):
    return jnp.dot(a, b, preferred_element_type=F32)


def _sigmoid(x):
    return 1.0 / (1.0 + jnp.exp(-x))


def _silu(x):
    return x * _sigmoid(x)


def _log_sigmoid(x):
    return jnp.minimum(x, 0.0) - jnp.log(1.0 + jnp.exp(-jnp.abs(x)))


def _rms(x):
    return x * lax.rsqrt(jnp.mean(x * x, axis=-1, keepdims=True) + EPS)


def _ada_kernel(c_ref, w_ref, b_ref, o_ref):
    s = _silu(c_ref[...]).astype(BF16)
    o_ref[...] = _dot(s, w_ref[...].astype(BF16)) + b_ref[...]


def _ada(cc, w_ada, b_ada):
    rows, d = cc.shape
    n = w_ada.shape[1]
    bn = 1536
    return pl.pallas_call(
        _ada_kernel,
        grid=(n // bn,),
        in_specs=[
            pl.BlockSpec((rows, d), lambda j: (0, 0)),
            pl.BlockSpec((d, bn), lambda j: (0, j)),
            pl.BlockSpec((1, bn), lambda j: (0, j)),
        ],
        out_specs=pl.BlockSpec((rows, bn), lambda j: (0, j)),
        out_shape=jax.ShapeDtypeStruct((rows, n), F32),
        compiler_params=pltpu.CompilerParams(
            dimension_semantics=("arbitrary",),
            vmem_limit_bytes=_vmem_limit(40 * 1024 * 1024)),
        name="ada",
    )(cc, w_ada, b_ada.reshape(1, n))


def _pool_consts():
    pos = np.arange(SUB) % GRID_W
    row = np.arange(SUB) // GRID_W
    a = np.zeros((len(POOL_WINDOWS), SUB, SUB), np.float32)
    inv = np.zeros((SUB, len(POOL_WINDOWS)), np.float32)
    for gi, win in enumerate(POOL_WINDOWS):
        lo = np.clip(pos - win // 2, 0, GRID_W - 1)
        hi = np.clip(pos + win // 2 - 1, 0, GRID_W - 1)
        for j in range(SUB):
            a[gi, j, row[j] * GRID_W + lo[j]: row[j] * GRID_W + hi[j] + 1] = 1.0
        inv[:, gi] = 1.0 / (hi - lo + 1)
    return a, inv


def _proj_kernel(*refs, tm, seq_len, cols, with_pool):
    if with_pool:
        (x_ref, xp_ref, xn_ref, mod_ref, n1_ref, w_ref, gb_ref, cw_ref,
         a_ref, inv_ref, pw_ref, ps_ref,
         q_ref, kt_ref, v_ref, g_ref, gt_ref, pool_ref, o_ref, lhs_scr) = refs
    else:
        (x_ref, xp_ref, xn_ref, mod_ref, n1_ref, w_ref, gb_ref, cw_ref,
         q_ref, kt_ref, v_ref, g_ref, gt_ref, lhs_scr) = refs
    d = x_ref.shape[-1]
    sub = SUB
    n_sub = tm // sub
    c_pool, c_qk, c_v, c_o, c_g, c_end = cols
    sh = mod_ref[0, :, 0:d]
    gain = n1_ref[...] * (1.0 + mod_ref[0, :, d:2 * d])
    qw = q_ref.shape[-1]
    row = lax.broadcasted_iota(jnp.int32, (sub, 1), 0)

    def mod_norm(xv):
        return (_rms(xv) * gain + sh).astype(BF16)

    def conv_act(cur, prev_last, next_first, k):
        start = pl.program_id(0) * tm + k * sub
        prev_last = jnp.where(start % seq_len == 0, 0.0, prev_last)
        next_first = jnp.where((start + sub) % seq_len == 0, 0.0, next_first)
        dn = jnp.where(row == 0, prev_last, pltpu.roll(cur, 1, axis=0))
        up = jnp.where(row == sub - 1, next_first, pltpu.roll(cur, sub - 1, axis=0))
        act = _silu(cw_ref[0:1, :] * dn + cw_ref[1:2, :] * cur + cw_ref[2:3, :] * up)
        q_ref[k * sub:(k + 1) * sub, :] = (act[:, :qw] * (DK ** -0.5)).astype(BF16)
        kt_ref[k] = act[:, qw:].T.astype(BF16)

    def side_dots(hb, k):
        rows = slice(k * sub, (k + 1) * sub)
        v_ref[rows, :] = _dot(hb, w_ref[:, c_v:c_o]).astype(BF16)
        g = _dot(hb, w_ref[:, c_g:c_end]) + gb_ref[...]
        g_ref[rows, :] = g
        gt_ref[k] = g.T
        if with_pool:
            o_ref[rows, :] = _dot(hb, w_ref[:, c_o:c_g]).astype(BF16)
            return _dot(hb, w_ref[:, c_pool:c_qk])
        return None

    def pool_mix(u, k):
        gd = u.shape[1] // len(POOL_WINDOWS)
        parts = []
        for gi in range(len(POOL_WINDOWS)):
            ug = u[:, gi * gd:(gi + 1) * gd]
            win = _dot(a_ref[gi], ug.astype(BF16))
            parts.append(win * inv_ref[:, gi:gi + 1] - ug)
        dmat = jnp.concatenate(parts, axis=1).astype(BF16)
        half = dmat.shape[1] // 2
        out = jnp.concatenate(
            [_dot(dmat[:, :half], pw_ref[0]), _dot(dmat[:, half:], pw_ref[1])], axis=1)
        pool_ref[k * sub:(k + 1) * sub, :] = (out * ps_ref[...]).astype(BF16)

    ps, us = [None] * n_sub, [None] * n_sub

    def own_rows(k):
        lo = HALO if k == 0 else 0
        return lo, lo + sub

    def conv(k):
        lo, hi = own_rows(k)
        if k == 0:
            prev_last = ps[0][lo - 1:lo, :]
        else:
            prev_last = ps[k - 1][own_rows(k - 1)[1] - 1:own_rows(k - 1)[1], :]
        if k == n_sub - 1:
            next_first = ps[k][hi:hi + 1, :]
        else:
            next_first = ps[k + 1][own_rows(k + 1)[0]:own_rows(k + 1)[0] + 1, :]
        conv_act(ps[k][lo:hi, :], prev_last, next_first, k)

    for k in range(n_sub):
        r0 = HALO + k * sub
        lhs_scr[r0:r0 + sub, :] = mod_norm(x_ref[k * sub:(k + 1) * sub, :])
        lo, hi = r0, r0 + sub
        if k == 0:
            lhs_scr[0:HALO, :] = mod_norm(xp_ref[...])
            lo = 0
        if k == n_sub - 1:
            lhs_scr[hi:hi + HALO, :] = mod_norm(xn_ref[...])
            hi += HALO
        ps[k] = _dot(lhs_scr[lo:hi, :], w_ref[:, c_qk:c_v])
        if k > 0:
            conv(k - 1)
        us[k] = side_dots(lhs_scr[r0:r0 + sub, :], k)
        if k > 0 and with_pool:
            pool_mix(us[k - 1], k - 1)
    conv(n_sub - 1)
    if with_pool:
        pool_mix(us[n_sub - 1], n_sub - 1)


def _in_proj(x2, seq_len, mod3, mod_row_fn, norm1, w_in, cols, gate_bias, conv_w, pool_parts, tm):
    n, d = x2.shape
    with_pool = pool_parts is not None
    c_pool, c_qk, c_v, c_o, c_g, c_end = cols
    ng = c_end - c_g
    hq = (c_v - c_qk) // 2
    vw = c_o - c_v
    hpt = tm // HALO
    const = lambda *shape: pl.BlockSpec(shape, lambda i: (0,) * len(shape))
    tok = lambda w: pl.BlockSpec((tm, w), lambda i: (i, 0))
    chunked = lambda r: pl.BlockSpec((tm // SUB, r, SUB), lambda i: (i, 0, 0))
    in_specs = [tok(d),
                pl.BlockSpec((HALO, d), lambda i: (jnp.maximum(i * hpt - 1, 0), 0)),
                pl.BlockSpec((HALO, d), lambda i: (jnp.minimum((i + 1) * hpt, n // HALO - 1), 0)),
                pl.BlockSpec((1, 1, mod3.shape[-1]), lambda i: (mod_row_fn(i), 0, 0)),
                const(1, d), const(*w_in.shape), const(1, ng), const(*conv_w.shape)]
    args = [x2, x2, x2, mod3, norm1.reshape(1, d), w_in, gate_bias.reshape(1, ng), conv_w]
    out_specs = [tok(hq), chunked(hq), tok(vw), tok(ng), chunked(ng)]
    out_shape = [jax.ShapeDtypeStruct((n, hq), BF16),
                 jax.ShapeDtypeStruct((n // SUB, hq, SUB), BF16),
                 jax.ShapeDtypeStruct((n, vw), BF16),
                 jax.ShapeDtypeStruct((n, ng), F32),
                 jax.ShapeDtypeStruct((n // SUB, ng, SUB), F32)]
    if with_pool:
        a_c, inv_c, pw_bd, p_scale = pool_parts
        pool_w = c_qk - c_pool
        in_specs += [const(*a_c.shape), const(*inv_c.shape), const(*pw_bd.shape),
                     const(1, pool_w)]
        args += [a_c, inv_c, pw_bd, p_scale.reshape(1, pool_w)]
        out_specs += [tok(pool_w), tok(c_g - c_o)]
        out_shape += [jax.ShapeDtypeStruct((n, pool_w), BF16),
                      jax.ShapeDtypeStruct((n, c_g - c_o), BF16)]
    return pl.pallas_call(
        functools.partial(_proj_kernel, tm=tm, seq_len=seq_len, cols=cols, with_pool=with_pool),
        grid=(n // tm,),
        in_specs=in_specs,
        out_specs=out_specs,
        out_shape=out_shape,
        scratch_shapes=[pltpu.VMEM((tm + 2 * HALO, d), BF16)],
        compiler_params=pltpu.CompilerParams(
            dimension_semantics=("arbitrary",),
            vmem_limit_bytes=_vmem_limit(48 * 1024 * 1024)),
        name="in_proj_pool" if with_pool else "in_proj_ctx",
    )(*args)


def _split3(x):
    hi = x.astype(BF16)
    r1 = x - hi.astype(F32)
    mid = r1.astype(BF16)
    lo = (r1 - mid.astype(F32)).astype(BF16)
    return hi, mid, lo


def _tri_dot_left(tri, x):
    return sum(_dot(tri, p) for p in _split3(x))


def _tri_dot_right(x, tri):
    return sum(_dot(p, tri) for p in _split3(x))


def _gate_tables(g_col, g_row, tril, triu):
    lf_row = _log_sigmoid(g_row) * LOG2E
    lf_col = _log_sigmoid(g_col) * LOG2E
    brow = (_tri_dot_right(lf_row, triu), _tri_dot_right(lf_row, tril))
    bcol = (_tri_dot_left(tril, lf_col), _tri_dot_left(triu, lf_col))
    return g_row * LOG2E, brow, bcol


def _chain_step(q_c, kt_c, v_aug, bcol, arow, mask, last, ct_ref, m0, m_last, want_out):
    lc = kt_c.shape[1]
    ct = ct_ref[...]
    if want_out:
        am = jnp.where(mask, arow, -jnp.inf)
        cm = jnp.max(am, axis=1, keepdims=True)
        mcol = jnp.maximum(m0, jnp.broadcast_to(cm, (lc, DV)))
        b_r = jnp.broadcast_to(bcol, (lc, DV))
        e = jnp.exp2(am - jnp.concatenate([mcol, mcol], axis=1))
        s = _dot(q_c, kt_c) * e
        den_intra = jnp.broadcast_to(jnp.sum(s, axis=1, keepdims=True), (lc, DV))
        p = s.astype(BF16)
        wk = e[last:last + 1, :]
        qs = (q_c.astype(F32) * jnp.exp2(m0 - mcol)[:, :DK]).astype(BF16)
    else:
        wk = jnp.exp2(arow - m_last)
    kw = (kt_c.astype(F32) * wk).astype(BF16)
    if want_out:
        r = _dot(jnp.concatenate([p, kw], axis=0), v_aug)
        r2 = _dot(qs, ct.astype(BF16))
        num = r[:lc, :DV] + r2[:, :DV]
        den = den_intra + r2[:, DV:]
        hout = num / jnp.maximum(jnp.abs(den), jnp.exp2(-(b_r + mcol)))
        d_ct = r[lc:, :]
    else:
        hout = None
        d_ct = _dot(kw, v_aug)
    ct_ref[...] = jnp.exp2(m0 - m_last) * ct + d_ct
    return hout


def _mlstm_kernel(qx_ref, ktx_ref, vx_ref, gx_ref, gtx_ref,
                  qc_ref, ktc_ref, vc_ref, gc_ref, gtc_ref, hn_ref, out_ref,
                  hf_scr, hb_scr, st_scr, gpx_scr, gpc_scr, *, t_x, t_c):
    lc = SUB
    nh = N_HEADS
    ng = N_GATES
    ri = lax.broadcasted_iota(jnp.int32, (lc, lc), 0)
    ci = lax.broadcasted_iota(jnp.int32, (lc, lc), 1)
    lower = ci <= ri
    upper = ci >= ri
    tril = lower.astype(BF16)
    triu = upper.astype(BF16)
    ones_blk = jnp.ones((lc, DV), BF16)
    dirs = ((lower, lc - 1), (upper, 0))

    st_scr[...] = jnp.zeros(st_scr.shape, F32)

    def tables(g_ref, gt_ref, gp_scr):
        for c in range(g_ref.shape[1]):
            gp_scr[:, c * ng:(c + 1) * ng] = g_ref[0, c]
        return _gate_tables(gp_scr[...], gt_ref[0], tril, triu)

    def gate_slices(tabs, cidx, di, h):
        g_row, brow, bcol = tabs
        col_i = cidx * ng + di * nh + h
        col_f = col_i + 2 * nh
        arow = g_row[col_i:col_i + 1, :] - brow[di][col_f:col_f + 1, :]
        return arow, bcol[di][:, col_f:col_f + 1]

    def stabilisers(order, tabs, m_run):
        plan = []
        for i in range(len(order[0])):
            step = []
            for di in range(2):
                _, last = dirs[di]
                for h in range(nh):
                    arow, bcol = gate_slices(tabs, order[di][i], di, h)
                    m0 = m_run[di * nh + h]
                    m_last = jnp.maximum(m0, jnp.max(arow, axis=1, keepdims=True))
                    step.append((m0, m_last))
                    m_run[di * nh + h] = bcol[last:last + 1, :] + m_last
            plan.append(step)
        return plan

    def chunk_pair(cf, cb, q_ref, kt_ref, v_ref, tabs, ms, want_out):
        for di, cidx in enumerate((cf, cb)):
            mask, last = dirs[di]
            r0 = cidx * lc
            for h in range(nh):
                arow, bcol = gate_slices(tabs, cidx, di, h)
                q_c = q_ref[0, r0:r0 + lc, h * DK:(h + 1) * DK]
                kt_c = kt_ref[0, cidx, h * DK:(h + 1) * DK, :]
                v_c = v_ref[0, r0:r0 + lc, h * DV:(h + 1) * DV]
                v_aug = jnp.concatenate([v_c, ones_blk], axis=1)
                m0, m_last = ms[di * nh + h]
                hout = _chain_step(q_c, kt_c, v_aug, bcol, arow, mask, last,
                                   st_scr.at[di * nh + h], m0, m_last, want_out)
                if want_out:
                    dst = hf_scr if di == 0 else hb_scr
                    dst[r0:r0 + lc, h * DV:(h + 1) * DV] = hout

    m_run = [jnp.zeros((1, 1), F32) for _ in range(2 * nh)]
    nc_c = t_c // lc
    tabs_c = tables(gc_ref, gtc_ref, gpc_scr)
    order_c = (list(range(nc_c)), list(range(nc_c - 1, -1, -1)))
    plan_c = stabilisers(order_c, tabs_c, m_run)
    nc_x = t_x // lc
    tabs_x = tables(gx_ref, gtx_ref, gpx_scr)
    order_x = (list(range(nc_x)), list(range(nc_x - 1, -1, -1)))
    plan_x = stabilisers(order_x, tabs_x, m_run)
    for i in range(nc_c):
        chunk_pair(order_c[0][i], order_c[1][i], qc_ref, ktc_ref, vc_ref, tabs_c, plan_c[i],
                   False)
    for i in range(nc_x):
        pl.when(pl.program_id(0) >= 0)(functools.partial(
            chunk_pair, order_x[0][i], order_x[1][i], qx_ref, ktx_ref, vx_ref, tabs_x,
            plan_x[i], True))

    for r0 in range(0, t_x, lc):
        hm = hf_scr[r0:r0 + lc, :] + hb_scr[r0:r0 + lc, :]
        parts = [_rms(hm[:, h * DV:(h + 1) * DV]) for h in range(nh)]
        hnorm = jnp.concatenate(parts, axis=1)
        out_ref[0, r0:r0 + lc, :] = (hnorm * hn_ref[...]).astype(BF16)


def _mlstm(qx, ktx, vx, gx, gtx, qc, ktc, vc, gc, gtc, head_norm):
    bsz, t_x, _ = qx.shape
    t_c = qc.shape[1]
    vw = vx.shape[2]
    ng = gx.shape[-1]
    bspec = lambda a: pl.BlockSpec((1,) + a.shape[1:], lambda b: (b,) + (0,) * (a.ndim - 1))
    args = (qx, ktx, vx, gx, gtx, qc, ktc, vc, gc, gtc)
    return pl.pallas_call(
        functools.partial(_mlstm_kernel, t_x=t_x, t_c=t_c),
        grid=(bsz,),
        in_specs=[bspec(a) for a in args] + [pl.BlockSpec((1, vw), lambda b: (0, 0))],
        out_specs=pl.BlockSpec((1, t_x, vw), lambda b: (b, 0, 0)),
        out_shape=jax.ShapeDtypeStruct((bsz, t_x, vw), BF16),
        scratch_shapes=[
            pltpu.VMEM((t_x, vw), F32),
            pltpu.VMEM((t_x, vw), F32),
            pltpu.VMEM((2 * N_HEADS, DK, 2 * DV), F32),
            pltpu.VMEM((SUB, (t_x // SUB) * ng), F32),
            pltpu.VMEM((SUB, (t_c // SUB) * ng), F32),
        ],
        compiler_params=pltpu.CompilerParams(
            dimension_semantics=("arbitrary",),
            vmem_limit_bytes=_vmem_limit(58 * 1024 * 1024)),
        name="mlstm",
    )(*args, head_norm.reshape(1, vw))


def _staggered(n_items, stages):
    state = [None] * n_items
    for step in range(len(stages) + n_items - 1):
        for i in range(n_items):
            k = step - i
            if 0 <= k < len(stages):
                state[i] = stages[k](i, state[i])


def _out_ffn_kernel(x_ref, pool_ref, ml_ref, o_ref, mod_ref, wo_ref, n2_ref, wu_ref,
                    wd_ref, nf_ref, out_ref):
    d = x_ref.shape[-1]
    tm = x_ref.shape[0]
    dff = wd_ref.shape[0]
    g1 = mod_ref[0, :, 2 * d:3 * d]
    sh2 = mod_ref[0, :, 3 * d:4 * d]
    gain2 = n2_ref[...] * (1.0 + mod_ref[0, :, 4 * d:5 * d])
    g2 = mod_ref[0, :, 5 * d:6 * d]
    pw = pool_ref.shape[-1]
    rows = [slice(r0, r0 + SUB) for r0 in range(0, tm, SUB)]

    def out_proj(i, _):
        r = rows[i]
        ml = (ml_ref[r, :].astype(F32) * _sigmoid(o_ref[r, :].astype(F32))).astype(BF16)
        return _dot(pool_ref[r, :], wo_ref[0:pw, :]) + _dot(ml, wo_ref[pw:, :])

    def norm2(i, mix):
        x1 = x_ref[rows[i], :] + g1 * mix
        return x1, (_rms(x1) * gain2 + sh2).astype(BF16)

    def up(i, st):
        x1, h2 = st
        return x1, _dot(h2, wu_ref[:, 0:dff]), _dot(h2, wu_ref[:, dff:2 * dff])

    def act(i, st):
        x1, gg, aa = st
        return x1, (_silu(gg) * aa).astype(BF16)

    def down(i, st):
        x1, a = st
        return x1, _dot(a, wd_ref[...])

    def final(i, st):
        x1, acc = st
        x2 = x1 + g2 * acc
        out_ref[rows[i], :] = _rms(x2) * nf_ref[...]

    _staggered(len(rows), [out_proj, norm2, up, act, down, final])


def _out_ffn(x2d, pool, ml, o_gate, mod3, w_out, norm2, w_up, w_down, norm_f, tm, tpb):
    n, d = x2d.shape
    tok = lambda w: pl.BlockSpec((tm, w), lambda i: (i, 0))
    const = lambda *shape: pl.BlockSpec(shape, lambda i: (0,) * len(shape),
                                        pipeline_mode=pl.Buffered(1))
    return pl.pallas_call(
        _out_ffn_kernel,
        grid=(n // tm,),
        in_specs=[tok(d), tok(pool.shape[1]), tok(ml.shape[1]), tok(o_gate.shape[1]),
                  pl.BlockSpec((1, 1, mod3.shape[-1]), lambda i: (i // tpb, 0, 0)),
                  const(*w_out.shape), const(1, d), const(*w_up.shape),
                  const(*w_down.shape), const(1, d)],
        out_specs=tok(d),
        out_shape=jax.ShapeDtypeStruct((n, d), F32),
        compiler_params=pltpu.CompilerParams(
            dimension_semantics=("arbitrary",),
            vmem_limit_bytes=_vmem_limit(56 * 1024 * 1024)),
        name="out_ffn",
    )(x2d, pool, ml, o_gate, mod3, w_out, norm2.reshape(1, d), w_up, w_down,
      norm_f.reshape(1, d))


def kernel(x, c, ctx, c_ctx, w_ada, b_ada, norm1, w_in, conv_qk, gate_bias, pool_w,
           pool_scale, head_norm, w_out, norm2, w_up, w_down, norm_f):
    bsz, t, d = x.shape
    t_c = ctx.shape[1]
    assert w_ada.shape[0] == 1, "single-layer block"
    pool_width = pool_w.shape[1] * pool_w.shape[2]
    qk_width = 2 * N_HEADS * DK
    ml_width = N_HEADS * DV
    o1 = pool_width
    o2 = o1 + qk_width
    o3 = o2 + ml_width
    o4 = o3 + ml_width
    cols = (0, o1, o2, o3, o4, o4 + N_GATES)
    tm = PROJ_TILE
    assert t % tm == 0 and (bsz * t_c) % tm == 0 and t_c % SUB == 0 and t % OUT_TILE == 0
    assert w_in.shape[2] == cols[-1] and w_up.shape[2] == 2 * w_down.shape[1]

    cc = jnp.concatenate([c, c_ctx[None, :]], axis=0)
    mod = _ada(cc, w_ada[0], b_ada[0])
    mod3 = mod.reshape(bsz + 1, 1, mod.shape[-1])

    w_in_b = w_in[0].astype(BF16)
    a_np, inv_np = _pool_consts()
    gd = pool_w.shape[2]
    pw = pool_w[0].astype(BF16)
    zero = jnp.zeros((gd, gd), BF16)
    pw_bd = jnp.stack([jnp.block([[pw[0], zero], [zero, pw[1]]]),
                       jnp.block([[pw[2], zero], [zero, pw[3]]])])
    pool_parts = (jnp.asarray(a_np, BF16), jnp.asarray(inv_np, F32), pw_bd, pool_scale[0])

    tpb = t // tm
    q_x, kt_x, v_x, g_x, gt_x, pool_x, o_x = _in_proj(
        x.reshape(bsz * t, d), t, mod3, lambda i: i // tpb, norm1[0], w_in_b, cols,
        gate_bias[0], conv_qk[0], pool_parts, tm)
    q_c, kt_c, v_c, g_c, gt_c = _in_proj(
        ctx.reshape(bsz * t_c, d), t_c, mod3, lambda i: bsz, norm1[0], w_in_b, cols,
        gate_bias[0], conv_qk[0], None, tm)

    def per_seq(a, tt):
        return a.reshape((bsz, tt) + a.shape[1:])

    def per_seq_chunks(a, tt, merge=False):
        nc = tt // SUB
        if merge:
            return a.reshape(bsz, nc * a.shape[1], a.shape[2])
        return a.reshape((bsz, nc) + a.shape[1:])

    ml = _mlstm(per_seq(q_x, t), per_seq_chunks(kt_x, t), per_seq(v_x, t),
                g_x.reshape(bsz, t // SUB, SUB, N_GATES), per_seq_chunks(gt_x, t, merge=True),
                per_seq(q_c, t_c), per_seq_chunks(kt_c, t_c), per_seq(v_c, t_c),
                g_c.reshape(bsz, t_c // SUB, SUB, N_GATES), per_seq_chunks(gt_c, t_c, merge=True),
                head_norm[0])

    out = _out_ffn(x.reshape(bsz * t, d), pool_x, ml.reshape(bsz * t, ml_width), o_x, mod3,
                   w_out[0].astype(BF16), norm2[0], w_up[0].astype(BF16),
                   w_down[0].astype(BF16), norm_f, OUT_TILE, t // OUT_TILE)
    return out.reshape(bsz, t, d)
```

```python
import functools

import numpy as np
import jax
import jax.numpy as jnp
from jax import lax
from jax.experimental import pallas as pl
from jax.experimental.pallas import tpu as pltpu

F32 = jnp.float32
BF16 = jnp.bfloat16

EPS = 1e-6
GRID_W = 64
POOL_WINDOWS = (2, 4, 8, 16)
N_HEADS = 4
DK = 64
DV = 128
N_GATES = 4 * N_HEADS
CONV_W = 3
LOG2E = 1.4426950408889634

V7X_VMEM_BYTES = 64 * 1024 * 1024
V7X_MXU_DIM = 256
BF16_SUBLANES = 16

SUB = V7X_MXU_DIM
PROJ_TILE = 1024
OUT_TILE = 1024
HALO = BF16_SUBLANES


def _vmem_limit(est_bytes):
    return int(min(V7X_VMEM_BYTES - 6 * 1024 * 1024, est_bytes))


def _dot(a, b):
    return jnp.dot(a, b, preferred_element_type=F32)


def _sigmoid(x):
    return 1.0 / (1.0 + jnp.exp(-x))


def _silu(x):
    return x * _sigmoid(x)


def _log_sigmoid(x):
    return jnp.minimum(x, 0.0) - jnp.log(1.0 + jnp.exp(-jnp.abs(x)))


def _rms(x):
    return x * lax.rsqrt(jnp.mean(x * x, axis=-1, keepdims=True) + EPS)


def _ada_kernel(c_ref, w_ref, b_ref, o_ref):
    s = _silu(c_ref[...]).astype(BF16)
    o_ref[...] = _dot(s, w_ref[...].astype(BF16)) + b_ref[...]


def _ada(cc, w_ada, b_ada):
    rows, d = cc.shape
    n = w_ada.shape[1]
    bn = 1536
    return pl.pallas_call(
        _ada_kernel,
        grid=(n // bn,),
        in_specs=[
            pl.BlockSpec((rows, d), lambda j: (0, 0)),
            pl.BlockSpec((d, bn), lambda j: (0, j)),
            pl.BlockSpec((1, bn), lambda j: (0, j)),
        ],
        out_specs=pl.BlockSpec((rows, bn), lambda j: (0, j)),
        out_shape=jax.ShapeDtypeStruct((rows, n), F32),
        compiler_params=pltpu.CompilerParams(
            dimension_semantics=("arbitrary",),
            vmem_limit_bytes=_vmem_limit(40 * 1024 * 1024)),
        name="ada",
    )(cc, w_ada, b_ada.reshape(1, n))


def _pool_consts():
    pos = np.arange(SUB) % GRID_W
    row = np.arange(SUB) // GRID_W
    a = np.zeros((len(POOL_WINDOWS), SUB, SUB), np.float32)
    inv = np.zeros((SUB, len(POOL_WINDOWS)), np.float32)
    for gi, win in enumerate(POOL_WINDOWS):
        lo = np.clip(pos - win // 2, 0, GRID_W - 1)
        hi = np.clip(pos + win // 2 - 1, 0, GRID_W - 1)
        for j in range(SUB):
            a[gi, j, row[j] * GRID_W + lo[j]: row[j] * GRID_W + hi[j] + 1] = 1.0
        inv[:, gi] = 1.0 / (hi - lo + 1)
    return a, inv


def _proj_kernel(*refs, tm, seq_len, cols, with_pool, n_cast):
    n_in = (12 if with_pool else 8) + n_cast
    cast_in = refs[n_in - n_cast:n_in]
    cast_out = refs[len(refs) - 1 - n_cast:len(refs) - 1]
    refs = refs[:n_in - n_cast] + refs[n_in:len(refs) - 1 - n_cast] + refs[-1:]
    for src, dst in zip(cast_in, cast_out):
        dst[...] = src[...].astype(BF16)
    if with_pool:
        (x_ref, xp_ref, xn_ref, mod_ref, n1_ref, w_ref, gb_ref, cw_ref,
         a_ref, inv_ref, pw_ref, ps_ref,
         q_ref, kt_ref, v_ref, g_ref, gt_ref, pool_ref, o_ref, lhs_scr) = refs
    else:
        (x_ref, xp_ref, xn_ref, mod_ref, n1_ref, w_ref, gb_ref, cw_ref,
         q_ref, kt_ref, v_ref, g_ref, gt_ref, lhs_scr) = refs
    d = x_ref.shape[-1]
    sub = SUB
    n_sub = tm // sub
    c_pool, c_qk, c_v, c_o, c_g, c_end = cols
    sh = mod_ref[0, :, 0:d]
    gain = n1_ref[...] * (1.0 + mod_ref[0, :, d:2 * d])
    qw = q_ref.shape[-1]
    row = lax.broadcasted_iota(jnp.int32, (sub, 1), 0)

    def mod_norm(xv):
        return (_rms(xv) * gain + sh).astype(BF16)

    def conv_act(cur, prev_last, next_first, k):
        start = pl.program_id(0) * tm + k * sub
        prev_last = jnp.where(start % seq_len == 0, 0.0, prev_last)
        next_first = jnp.where((start + sub) % seq_len == 0, 0.0, next_first)
        dn = jnp.where(row == 0, prev_last, pltpu.roll(cur, 1, axis=0))
        up = jnp.where(row == sub - 1, next_first, pltpu.roll(cur, sub - 1, axis=0))
        act = _silu(cw_ref[0:1, :] * dn + cw_ref[1:2, :] * cur + cw_ref[2:3, :] * up)
        q_ref[k * sub:(k + 1) * sub, :] = (act[:, :qw] * (DK ** -0.5)).astype(BF16)
        kt_ref[k] = act[:, qw:].T.astype(BF16)

    def side_dots(hb, k):
        rows = slice(k * sub, (k + 1) * sub)
        v_ref[rows, :] = _dot(hb, w_ref[:, c_v:c_o]).astype(BF16)
        g = _dot(hb, w_ref[:, c_g:c_end]) + gb_ref[...]
        g_ref[rows, :] = g
        gt_ref[k] = g.T
        if with_pool:
            o_ref[rows, :] = _dot(hb, w_ref[:, c_o:c_g]).astype(BF16)
            return _dot(hb, w_ref[:, c_pool:c_qk])
        return None

    def pool_mix(u, k):
        gd = u.shape[1] // len(POOL_WINDOWS)
        parts = []
        for gi in range(len(POOL_WINDOWS)):
            ug = u[:, gi * gd:(gi + 1) * gd]
            win = _dot(a_ref[gi], ug.astype(BF16))
            parts.append(win * inv_ref[:, gi:gi + 1] - ug)
        dmat = jnp.concatenate(parts, axis=1).astype(BF16)
        half = dmat.shape[1] // 2
        out = jnp.concatenate(
            [_dot(dmat[:, :half], pw_ref[0]), _dot(dmat[:, half:], pw_ref[1])], axis=1)
        pool_ref[k * sub:(k + 1) * sub, :] = (out * ps_ref[...]).astype(BF16)

    ps, us = [None] * n_sub, [None] * n_sub

    def own_rows(k):
        lo = HALO if k == 0 else 0
        return lo, lo + sub

    def conv(k):
        lo, hi = own_rows(k)
        if k == 0:
            prev_last = ps[0][lo - 1:lo, :]
        else:
            prev_last = ps[k - 1][own_rows(k - 1)[1] - 1:own_rows(k - 1)[1], :]
        if k == n_sub - 1:
            next_first = ps[k][hi:hi + 1, :]
        else:
            next_first = ps[k + 1][own_rows(k + 1)[0]:own_rows(k + 1)[0] + 1, :]
        conv_act(ps[k][lo:hi, :], prev_last, next_first, k)

    for k in range(n_sub):
        r0 = HALO + k * sub
        lhs_scr[r0:r0 + sub, :] = mod_norm(x_ref[k * sub:(k + 1) * sub, :])
        lo, hi = r0, r0 + sub
        if k == 0:
            lhs_scr[0:HALO, :] = mod_norm(xp_ref[...])
            lo = 0
        if k == n_sub - 1:
            lhs_scr[hi:hi + HALO, :] = mod_norm(xn_ref[...])
            hi += HALO
        ps[k] = _dot(lhs_scr[lo:hi, :], w_ref[:, c_qk:c_v])
        if k > 0:
            conv(k - 1)
        us[k] = side_dots(lhs_scr[r0:r0 + sub, :], k)
        if k > 0 and with_pool:
            pool_mix(us[k - 1], k - 1)
    conv(n_sub - 1)
    if with_pool:
        pool_mix(us[n_sub - 1], n_sub - 1)


def _in_proj(x2, seq_len, mod3, mod_row_fn, norm1, w_in, cols, gate_bias, conv_w, pool_parts, tm,
             cast_weights=()):
    n, d = x2.shape
    with_pool = pool_parts is not None
    c_pool, c_qk, c_v, c_o, c_g, c_end = cols
    ng = c_end - c_g
    hq = (c_v - c_qk) // 2
    vw = c_o - c_v
    hpt = tm // HALO
    const = lambda *shape: pl.BlockSpec(shape, lambda i: (0,) * len(shape))
    tok = lambda w: pl.BlockSpec((tm, w), lambda i: (i, 0))
    chunked = lambda r: pl.BlockSpec((tm // SUB, r, SUB), lambda i: (i, 0, 0))
    in_specs = [tok(d),
                pl.BlockSpec((HALO, d), lambda i: (jnp.maximum(i * hpt - 1, 0), 0)),
                pl.BlockSpec((HALO, d), lambda i: (jnp.minimum((i + 1) * hpt, n // HALO - 1), 0)),
                pl.BlockSpec((1, 1, mod3.shape[-1]), lambda i: (mod_row_fn(i), 0, 0)),
                const(1, d), const(*w_in.shape), const(1, ng), const(*conv_w.shape)]
    args = [x2, x2, x2, mod3, norm1.reshape(1, d), w_in, gate_bias.reshape(1, ng), conv_w]
    out_specs = [tok(hq), chunked(hq), tok(vw), tok(ng), chunked(ng)]
    out_shape = [jax.ShapeDtypeStruct((n, hq), BF16),
                 jax.ShapeDtypeStruct((n // SUB, hq, SUB), BF16),
                 jax.ShapeDtypeStruct((n, vw), BF16),
                 jax.ShapeDtypeStruct((n, ng), F32),
                 jax.ShapeDtypeStruct((n // SUB, ng, SUB), F32)]
    if with_pool:
        a_c, inv_c, pw_bd, p_scale = pool_parts
        pool_w = c_qk - c_pool
        in_specs += [const(*a_c.shape), const(*inv_c.shape), const(*pw_bd.shape),
                     const(1, pool_w)]
        args += [a_c, inv_c, pw_bd, p_scale.reshape(1, pool_w)]
        out_specs += [tok(pool_w), tok(c_g - c_o)]
        out_shape += [jax.ShapeDtypeStruct((n, pool_w), BF16),
                      jax.ShapeDtypeStruct((n, c_g - c_o), BF16)]
    steps = n // tm
    lanes = 128
    for wgt in cast_weights:
        slab_rows = wgt.size // (steps * lanes)
        assert slab_rows * steps * lanes == wgt.size and slab_rows % BF16_SUBLANES == 0
        slab = pl.BlockSpec((1, slab_rows, lanes), lambda i: (i, 0, 0))
        in_specs.append(slab)
        args.append(wgt.reshape(steps, slab_rows, lanes))
        out_specs.append(slab)
        out_shape.append(jax.ShapeDtypeStruct((steps, slab_rows, lanes), BF16))
    outs = pl.pallas_call(
        functools.partial(_proj_kernel, tm=tm, seq_len=seq_len, cols=cols, with_pool=with_pool,
                          n_cast=len(cast_weights)),
        grid=(steps,),
        in_specs=in_specs,
        out_specs=out_specs,
        out_shape=out_shape,
        scratch_shapes=[pltpu.VMEM((tm + 2 * HALO, d), BF16)],
        compiler_params=pltpu.CompilerParams(
            dimension_semantics=("arbitrary",),
            vmem_limit_bytes=_vmem_limit(48 * 1024 * 1024)),
        name="in_proj_pool" if with_pool else "in_proj_ctx",
    )(*args)
    n_main = len(outs) - len(cast_weights)
    casts = [o.reshape(wgt.shape) for o, wgt in zip(outs[n_main:], cast_weights)]
    return list(outs[:n_main]) + casts


def _split3(x):
    hi = x.astype(BF16)
    r1 = x - hi.astype(F32)
    mid = r1.astype(BF16)
    lo = (r1 - mid.astype(F32)).astype(BF16)
    return hi, mid, lo


def _tri_dot_left(tri, x):
    return sum(_dot(tri, p) for p in _split3(x))


def _tri_dot_right(x, tri):
    return sum(_dot(p, tri) for p in _split3(x))


def _gate_tables(g_col, g_row, tril, triu):
    lf_row = _log_sigmoid(g_row) * LOG2E
    lf_col = _log_sigmoid(g_col) * LOG2E
    brow = (_tri_dot_right(lf_row, triu), _tri_dot_right(lf_row, tril))
    bcol = (_tri_dot_left(tril, lf_col), _tri_dot_left(triu, lf_col))
    return g_row * LOG2E, brow, bcol


def _chain_step(q_c, kt_c, v_aug, bcol, arow, mask, last, ct_ref, m0, m_last, want_out):
    lc = kt_c.shape[1]
    ct = ct_ref[...]
    if want_out:
        am = jnp.where(mask, arow, -jnp.inf)
        cm = jnp.max(am, axis=1, keepdims=True)
        mcol = jnp.maximum(m0, jnp.broadcast_to(cm, (lc, DV)))
        b_r = jnp.broadcast_to(bcol, (lc, DV))
        e = jnp.exp2(am - jnp.concatenate([mcol, mcol], axis=1))
        s = _dot(q_c, kt_c) * e
        den_intra = jnp.broadcast_to(jnp.sum(s, axis=1, keepdims=True), (lc, DV))
        p = s.astype(BF16)
        wk = e[last:last + 1, :]
        qs = (q_c.astype(F32) * jnp.exp2(m0 - mcol)[:, :DK]).astype(BF16)
    else:
        wk = jnp.exp2(arow - m_last)
    kw = (kt_c.astype(F32) * wk).astype(BF16)
    if want_out:
        r = _dot(jnp.concatenate([p, kw], axis=0), v_aug)
        r2 = _dot(qs, ct.astype(BF16))
        num = r[:lc, :DV] + r2[:, :DV]
        den = den_intra + r2[:, DV:]
        hout = num / jnp.maximum(jnp.abs(den), jnp.exp2(-(b_r + mcol)))
        d_ct = r[lc:, :]
    else:
        hout = None
        d_ct = _dot(kw, v_aug)
    ct_ref[...] = jnp.exp2(m0 - m_last) * ct + d_ct
    return hout


def _mlstm_kernel(qx_ref, ktx_ref, vx_ref, ox_ref, gx_ref, gtx_ref,
                  qc_ref, ktc_ref, vc_ref, gc_ref, gtc_ref, hn_ref, out_ref,
                  hf_scr, hb_scr, st_scr, gpx_scr, gpc_scr, *, t_x, t_c):
    lc = SUB
    nh = N_HEADS
    ng = N_GATES
    ri = lax.broadcasted_iota(jnp.int32, (lc, lc), 0)
    ci = lax.broadcasted_iota(jnp.int32, (lc, lc), 1)
    lower = ci <= ri
    upper = ci >= ri
    tril = lower.astype(BF16)
    triu = upper.astype(BF16)
    ones_blk = jnp.ones((lc, DV), BF16)
    dirs = ((lower, lc - 1), (upper, 0))

    st_scr[...] = jnp.zeros(st_scr.shape, F32)

    def tables(g_ref, gt_ref, gp_scr):
        for c in range(g_ref.shape[1]):
            gp_scr[:, c * ng:(c + 1) * ng] = g_ref[0, c]
        return _gate_tables(gp_scr[...], gt_ref[0], tril, triu)

    def gate_slices(tabs, cidx, di, h):
        g_row, brow, bcol = tabs
        col_i = cidx * ng + di * nh + h
        col_f = col_i + 2 * nh
        arow = g_row[col_i:col_i + 1, :] - brow[di][col_f:col_f + 1, :]
        return arow, bcol[di][:, col_f:col_f + 1]

    def stabilisers(order, tabs, m_run):
        plan = []
        for i in range(len(order[0])):
            step = []
            for di in range(2):
                _, last = dirs[di]
                for h in range(nh):
                    arow, bcol = gate_slices(tabs, order[di][i], di, h)
                    m0 = m_run[di * nh + h]
                    m_last = jnp.maximum(m0, jnp.max(arow, axis=1, keepdims=True))
                    step.append((m0, m_last))
                    m_run[di * nh + h] = bcol[last:last + 1, :] + m_last
            plan.append(step)
        return plan

    def chunk_pair(cf, cb, q_ref, kt_ref, v_ref, tabs, ms, want_out):
        for di, cidx in enumerate((cf, cb)):
            mask, last = dirs[di]
            r0 = cidx * lc
            for h in range(nh):
                arow, bcol = gate_slices(tabs, cidx, di, h)
                q_c = q_ref[0, r0:r0 + lc, h * DK:(h + 1) * DK]
                kt_c = kt_ref[0, cidx, h * DK:(h + 1) * DK, :]
                v_c = v_ref[0, r0:r0 + lc, h * DV:(h + 1) * DV]
                v_aug = jnp.concatenate([v_c, ones_blk], axis=1)
                m0, m_last = ms[di * nh + h]
                hout = _chain_step(q_c, kt_c, v_aug, bcol, arow, mask, last,
                                   st_scr.at[di * nh + h], m0, m_last, want_out)
                if want_out:
                    dst = hf_scr if di == 0 else hb_scr
                    dst[r0:r0 + lc, h * DV:(h + 1) * DV] = hout

    m_run = [jnp.zeros((1, 1), F32) for _ in range(2 * nh)]
    nc_c = t_c // lc
    tabs_c = tables(gc_ref, gtc_ref, gpc_scr)
    order_c = (list(range(nc_c)), list(range(nc_c - 1, -1, -1)))
    plan_c = stabilisers(order_c, tabs_c, m_run)
    nc_x = t_x // lc
    tabs_x = tables(gx_ref, gtx_ref, gpx_scr)
    order_x = (list(range(nc_x)), list(range(nc_x - 1, -1, -1)))
    plan_x = stabilisers(order_x, tabs_x, m_run)
    for i in range(nc_c):
        chunk_pair(order_c[0][i], order_c[1][i], qc_ref, ktc_ref, vc_ref, tabs_c, plan_c[i],
                   False)
    for i in range(nc_x):
        pl.when(pl.program_id(0) >= 0)(functools.partial(
            chunk_pair, order_x[0][i], order_x[1][i], qx_ref, ktx_ref, vx_ref, tabs_x,
            plan_x[i], True))

    for r0 in range(0, t_x, lc):
        hm = hf_scr[r0:r0 + lc, :] + hb_scr[r0:r0 + lc, :]
        parts = [_rms(hm[:, h * DV:(h + 1) * DV]) for h in range(nh)]
        hnorm = jnp.concatenate(parts, axis=1)
        gate = _sigmoid(ox_ref[0, r0:r0 + lc, :].astype(F32))
        out_ref[0, r0:r0 + lc, :] = (hnorm * hn_ref[...] * gate).astype(BF16)


def _mlstm(qx, ktx, vx, ox, gx, gtx, qc, ktc, vc, gc, gtc, head_norm):
    bsz, t_x, _ = qx.shape
    t_c = qc.shape[1]
    vw = vx.shape[2]
    ng = gx.shape[-1]
    bspec = lambda a: pl.BlockSpec((1,) + a.shape[1:], lambda b: (b,) + (0,) * (a.ndim - 1))
    args = (qx, ktx, vx, ox, gx, gtx, qc, ktc, vc, gc, gtc)
    return pl.pallas_call(
        functools.partial(_mlstm_kernel, t_x=t_x, t_c=t_c),
        grid=(bsz,),
        in_specs=[bspec(a) for a in args] + [pl.BlockSpec((1, vw), lambda b: (0, 0))],
        out_specs=pl.BlockSpec((1, t_x, vw), lambda b: (b, 0, 0)),
        out_shape=jax.ShapeDtypeStruct((bsz, t_x, vw), BF16),
        scratch_shapes=[
            pltpu.VMEM((t_x, vw), F32),
            pltpu.VMEM((t_x, vw), F32),
            pltpu.VMEM((2 * N_HEADS, DK, 2 * DV), F32),
            pltpu.VMEM((SUB, (t_x // SUB) * ng), F32),
            pltpu.VMEM((SUB, (t_c // SUB) * ng), F32),
        ],
        compiler_params=pltpu.CompilerParams(
            dimension_semantics=("arbitrary",),
            vmem_limit_bytes=_vmem_limit(58 * 1024 * 1024)),
        name="mlstm",
    )(*args, head_norm.reshape(1, vw))


def _staggered(n_items, stages):
    state = [None] * n_items
    for step in range(len(stages) + n_items - 1):
        for i in range(n_items):
            k = step - i
            if 0 <= k < len(stages):
                state[i] = stages[k](i, state[i])


def _out_ffn_kernel(x_ref, pool_ref, ml_ref, mod_ref, wo_ref, n2_ref, wu_ref,
                    wd_ref, nf_ref, out_ref):
    d = x_ref.shape[-1]
    tm = x_ref.shape[0]
    dff = wd_ref.shape[0]
    g1 = mod_ref[0, :, 2 * d:3 * d]
    sh2 = mod_ref[0, :, 3 * d:4 * d]
    gain2 = n2_ref[...] * (1.0 + mod_ref[0, :, 4 * d:5 * d])
    g2 = mod_ref[0, :, 5 * d:6 * d]
    pw = pool_ref.shape[-1]
    rows = [slice(r0, r0 + SUB) for r0 in range(0, tm, SUB)]

    def out_proj(i, _):
        r = rows[i]
        return _dot(pool_ref[r, :], wo_ref[0:pw, :]) + _dot(ml_ref[r, :], wo_ref[pw:, :])

    def norm2(i, mix):
        x1 = x_ref[rows[i], :] + g1 * mix
        return x1, (_rms(x1) * gain2 + sh2).astype(BF16)

    def up(i, st):
        x1, h2 = st
        return x1, _dot(h2, wu_ref[:, 0:dff]), _dot(h2, wu_ref[:, dff:2 * dff])

    def act(i, st):
        x1, gg, aa = st
        return x1, (_silu(gg) * aa).astype(BF16)

    def down(i, st):
        x1, a = st
        return x1, _dot(a, wd_ref[...])

    def final(i, st):
        x1, acc = st
        x2 = x1 + g2 * acc
        out_ref[rows[i], :] = _rms(x2) * nf_ref[...]

    _staggered(len(rows), [out_proj, norm2, up, act, down, final])


def _out_ffn(x2d, pool, ml, mod3, w_out, norm2, w_up, w_down, norm_f, tm, tpb):
    n, d = x2d.shape
    tok = lambda w: pl.BlockSpec((tm, w), lambda i: (i, 0))
    const = lambda *shape: pl.BlockSpec(shape, lambda i: (0,) * len(shape),
                                        pipeline_mode=pl.Buffered(1))
    return pl.pallas_call(
        _out_ffn_kernel,
        grid=(n // tm,),
        in_specs=[tok(d), tok(pool.shape[1]), tok(ml.shape[1]),
                  pl.BlockSpec((1, 1, mod3.shape[-1]), lambda i: (i // tpb, 0, 0)),
                  const(*w_out.shape), const(1, d), const(*w_up.shape),
                  const(*w_down.shape), const(1, d)],
        out_specs=tok(d),
        out_shape=jax.ShapeDtypeStruct((n, d), F32),
        compiler_params=pltpu.CompilerParams(
            dimension_semantics=("arbitrary",),
            vmem_limit_bytes=_vmem_limit(56 * 1024 * 1024)),
        name="out_ffn",
    )(x2d, pool, ml, mod3, w_out, norm2.reshape(1, d), w_up, w_down, norm_f.reshape(1, d))


def kernel(x, c, ctx, c_ctx, w_ada, b_ada, norm1, w_in, conv_qk, gate_bias, pool_w,
           pool_scale, head_norm, w_out, norm2, w_up, w_down, norm_f):
    bsz, t, d = x.shape
    t_c = ctx.shape[1]
    assert w_ada.shape[0] == 1, "single-layer block"
    pool_width = pool_w.shape[1] * pool_w.shape[2]
    qk_width = 2 * N_HEADS * DK
    ml_width = N_HEADS * DV
    o1 = pool_width
    o2 = o1 + qk_width
    o3 = o2 + ml_width
    o4 = o3 + ml_width
    cols = (0, o1, o2, o3, o4, o4 + N_GATES)
    tm = PROJ_TILE
    assert t % tm == 0 and (bsz * t_c) % tm == 0 and t_c % SUB == 0 and t % OUT_TILE == 0
    assert w_in.shape[2] == cols[-1] and w_up.shape[2] == 2 * w_down.shape[1]

    cc = jnp.concatenate([c, c_ctx[None, :]], axis=0)
    mod = _ada(cc, w_ada[0], b_ada[0])
    mod3 = mod.reshape(bsz + 1, 1, mod.shape[-1])

    w_in_b = w_in[0].astype(BF16)
    a_np, inv_np = _pool_consts()
    gd = pool_w.shape[2]
    pw = pool_w[0].astype(BF16)
    zero = jnp.zeros((gd, gd), BF16)
    pw_bd = jnp.stack([jnp.block([[pw[0], zero], [zero, pw[1]]]),
                       jnp.block([[pw[2], zero], [zero, pw[3]]])])
    pool_parts = (jnp.asarray(a_np, BF16), jnp.asarray(inv_np, F32), pw_bd, pool_scale[0])

    tpb = t // tm
    q_x, kt_x, v_x, g_x, gt_x, pool_x, o_x, w_out_b, w_up_b, w_down_b = _in_proj(
        x.reshape(bsz * t, d), t, mod3, lambda i: i // tpb, norm1[0], w_in_b, cols,
        gate_bias[0], conv_qk[0], pool_parts, tm, cast_weights=(w_out[0], w_up[0], w_down[0]))
    q_c, kt_c, v_c, g_c, gt_c = _in_proj(
        ctx.reshape(bsz * t_c, d), t_c, mod3, lambda i: bsz, norm1[0], w_in_b, cols,
        gate_bias[0], conv_qk[0], None, tm)

    def per_seq(a, tt):
        return a.reshape((bsz, tt) + a.shape[1:])

    def per_seq_chunks(a, tt, merge=False):
        nc = tt // SUB
        if merge:
            return a.reshape(bsz, nc * a.shape[1], a.shape[2])
        return a.reshape((bsz, nc) + a.shape[1:])

    ml = _mlstm(per_seq(q_x, t), per_seq_chunks(kt_x, t), per_seq(v_x, t), per_seq(o_x, t),
                g_x.reshape(bsz, t // SUB, SUB, N_GATES), per_seq_chunks(gt_x, t, merge=True),
                per_seq(q_c, t_c), per_seq_chunks(kt_c, t_c), per_seq(v_c, t_c),
                g_c.reshape(bsz, t_c // SUB, SUB, N_GATES), per_seq_chunks(gt_c, t_c, merge=True),
                head_norm[0])

    out = _out_ffn(x.reshape(bsz * t, d), pool_x, ml.reshape(bsz * t, ml_width), mod3,
                   w_out_b, norm2[0], w_up_b, w_down_b, norm_f, OUT_TILE, t // OUT_TILE)
    return out.reshape(bsz, t, d)
```

```python
import functools

import numpy as np
import jax
import jax.numpy as jnp
from jax import lax
from jax.experimental import pallas as pl
from jax.experimental.pallas import tpu as pltpu

F32 = jnp.float32
BF16 = jnp.bfloat16

EPS = 1e-6
GRID_W = 64
POOL_WINDOWS = (2, 4, 8, 16)
N_HEADS = 4
DK = 64
DV = 128
N_GATES = 4 * N_HEADS
CONV_W = 3
LOG2E = 1.4426950408889634

V7X_VMEM_BYTES = 64 * 1024 * 1024
V7X_MXU_DIM = 256
BF16_SUBLANES = 16

SUB = V7X_MXU_DIM
PROJ_TILE = 1024
OUT_TILE = 1024
HALO = BF16_SUBLANES


def _vmem_limit(est_bytes):
    return int(min(V7X_VMEM_BYTES - 6 * 1024 * 1024, est_bytes))


def _dot(a, b):
    return jnp.dot(a, b, preferred_element_type=F32)


def _sigmoid(x):
    return 1.0 / (1.0 + jnp.exp(-x))


def _silu(x):
    return x * _sigmoid(x)


def _log_sigmoid(x):
    return jnp.minimum(x, 0.0) - jnp.log(1.0 + jnp.exp(-jnp.abs(x)))


def _rms(x):
    return x * lax.rsqrt(jnp.mean(x * x, axis=-1, keepdims=True) + EPS)


def _ada_kernel(c_ref, w_ref, b_ref, o_ref):
    s = _silu(c_ref[...]).astype(BF16)
    o_ref[...] = _dot(s, w_ref[...].astype(BF16)) + b_ref[...]


def _ada(cc, w_ada, b_ada):
    rows, d = cc.shape
    n = w_ada.shape[1]
    bn = 1536
    return pl.pallas_call(
        _ada_kernel,
        grid=(n // bn,),
        in_specs=[
            pl.BlockSpec((rows, d), lambda j: (0, 0)),
            pl.BlockSpec((d, bn), lambda j: (0, j)),
            pl.BlockSpec((1, bn), lambda j: (0, j)),
        ],
        out_specs=pl.BlockSpec((rows, bn), lambda j: (0, j)),
        out_shape=jax.ShapeDtypeStruct((rows, n), F32),
        compiler_params=pltpu.CompilerParams(
            dimension_semantics=("arbitrary",),
            vmem_limit_bytes=_vmem_limit(40 * 1024 * 1024)),
        name="ada",
    )(cc, w_ada, b_ada.reshape(1, n))


def _pool_consts():
    pos = np.arange(SUB) % GRID_W
    row = np.arange(SUB) // GRID_W
    a = np.zeros((len(POOL_WINDOWS), SUB, SUB), np.float32)
    inv = np.zeros((SUB, len(POOL_WINDOWS)), np.float32)
    for gi, win in enumerate(POOL_WINDOWS):
        lo = np.clip(pos - win // 2, 0, GRID_W - 1)
        hi = np.clip(pos + win // 2 - 1, 0, GRID_W - 1)
        for j in range(SUB):
            a[gi, j, row[j] * GRID_W + lo[j]: row[j] * GRID_W + hi[j] + 1] = 1.0
        inv[:, gi] = 1.0 / (hi - lo + 1)
    return a, inv


def _proj_kernel(*refs, tm, seq_len, cols, with_pool, n_cast):
    n_in = (12 if with_pool else 8) + n_cast
    cast_in = refs[n_in - n_cast:n_in]
    cast_out = refs[len(refs) - 1 - n_cast:len(refs) - 1]
    refs = refs[:n_in - n_cast] + refs[n_in:len(refs) - 1 - n_cast] + refs[-1:]
    for src, dst in zip(cast_in, cast_out):
        dst[...] = src[...].astype(BF16)
    if with_pool:
        (x_ref, xp_ref, xn_ref, mod_ref, n1_ref, w_ref, gb_ref, cw_ref,
         a_ref, inv_ref, pw_ref, ps_ref,
         q_ref, kt_ref, v_ref, g_ref, gt_ref, pool_ref, o_ref, lhs_scr) = refs
    else:
        (x_ref, xp_ref, xn_ref, mod_ref, n1_ref, w_ref, gb_ref, cw_ref,
         q_ref, kt_ref, v_ref, g_ref, gt_ref, lhs_scr) = refs
    d = x_ref.shape[-1]
    sub = SUB
    n_sub = tm // sub
    c_pool, c_qk, c_v, c_o, c_g, c_end = cols
    sh = mod_ref[0, :, 0:d]
    gain = n1_ref[...] * (1.0 + mod_ref[0, :, d:2 * d])
    qw = q_ref.shape[-1]
    row = lax.broadcasted_iota(jnp.int32, (sub, 1), 0)

    def mod_norm(xv):
        return (_rms(xv) * gain + sh).astype(BF16)

    def conv_act(cur, prev_last, next_first, k):
        start = pl.program_id(0) * tm + k * sub
        prev_last = jnp.where(start % seq_len == 0, 0.0, prev_last)
        next_first = jnp.where((start + sub) % seq_len == 0, 0.0, next_first)
        dn = jnp.where(row == 0, prev_last, pltpu.roll(cur, 1, axis=0))
        up = jnp.where(row == sub - 1, next_first, pltpu.roll(cur, sub - 1, axis=0))
        act = _silu(cw_ref[0:1, :] * dn + cw_ref[1:2, :] * cur + cw_ref[2:3, :] * up)
        q_ref[k * sub:(k + 1) * sub, :] = (act[:, :qw] * (DK ** -0.5)).astype(BF16)
        kt_ref[k] = act[:, qw:].T.astype(BF16)

    def side_dots(hb, k):
        rows = slice(k * sub, (k + 1) * sub)
        v_ref[rows, :] = _dot(hb, w_ref[:, c_v:c_o]).astype(BF16)
        g = _dot(hb, w_ref[:, c_g:c_end]) + gb_ref[...]
        g_ref[rows, :] = g
        gt_ref[k] = g.T
        if with_pool:
            o_ref[rows, :] = _dot(hb, w_ref[:, c_o:c_g]).astype(BF16)
            return _dot(hb, w_ref[:, c_pool:c_qk])
        return None

    def pool_mix(u, k):
        gd = u.shape[1] // len(POOL_WINDOWS)
        parts = []
        for gi in range(len(POOL_WINDOWS)):
            ug = u[:, gi * gd:(gi + 1) * gd]
            win = _dot(a_ref[gi], ug.astype(BF16))
            parts.append(win * inv_ref[:, gi:gi + 1] - ug)
        dmat = jnp.concatenate(parts, axis=1).astype(BF16)
        half = dmat.shape[1] // 2
        out = jnp.concatenate(
            [_dot(dmat[:, :half], pw_ref[0]), _dot(dmat[:, half:], pw_ref[1])], axis=1)
        pool_ref[k * sub:(k + 1) * sub, :] = (out * ps_ref[...]).astype(BF16)

    ps, us = [None] * n_sub, [None] * n_sub

    def own_rows(k):
        lo = HALO if k == 0 else 0
        return lo, lo + sub

    def conv(k):
        lo, hi = own_rows(k)
        if k == 0:
            prev_last = ps[0][lo - 1:lo, :]
        else:
            prev_last = ps[k - 1][own_rows(k - 1)[1] - 1:own_rows(k - 1)[1], :]
        if k == n_sub - 1:
            next_first = ps[k][hi:hi + 1, :]
        else:
            next_first = ps[k + 1][own_rows(k + 1)[0]:own_rows(k + 1)[0] + 1, :]
        conv_act(ps[k][lo:hi, :], prev_last, next_first, k)

    for k in range(n_sub):
        r0 = HALO + k * sub
        lhs_scr[r0:r0 + sub, :] = mod_norm(x_ref[k * sub:(k + 1) * sub, :])
        lo, hi = r0, r0 + sub
        if k == 0:
            lhs_scr[0:HALO, :] = mod_norm(xp_ref[...])
            lo = 0
        if k == n_sub - 1:
            lhs_scr[hi:hi + HALO, :] = mod_norm(xn_ref[...])
            hi += HALO
        ps[k] = _dot(lhs_scr[lo:hi, :], w_ref[:, c_qk:c_v])
        if k > 0:
            conv(k - 1)
        us[k] = side_dots(lhs_scr[r0:r0 + sub, :], k)
        if k > 0 and with_pool:
            pool_mix(us[k - 1], k - 1)
    conv(n_sub - 1)
    if with_pool:
        pool_mix(us[n_sub - 1], n_sub - 1)


def _in_proj(x2, seq_len, mod3, mod_row_fn, norm1, w_in, cols, gate_bias, conv_w, pool_parts, tm,
             cast_weights=()):
    n, d = x2.shape
    with_pool = pool_parts is not None
    c_pool, c_qk, c_v, c_o, c_g, c_end = cols
    ng = c_end - c_g
    hq = (c_v - c_qk) // 2
    vw = c_o - c_v
    hpt = tm // HALO
    const = lambda *shape: pl.BlockSpec(shape, lambda i: (0,) * len(shape))
    tok = lambda w: pl.BlockSpec((tm, w), lambda i: (i, 0))
    chunked = lambda r: pl.BlockSpec((tm // SUB, r, SUB), lambda i: (i, 0, 0))
    in_specs = [tok(d),
                pl.BlockSpec((HALO, d), lambda i: (jnp.maximum(i * hpt - 1, 0), 0)),
                pl.BlockSpec((HALO, d), lambda i: (jnp.minimum((i + 1) * hpt, n // HALO - 1), 0)),
                pl.BlockSpec((1, 1, mod3.shape[-1]), lambda i: (mod_row_fn(i), 0, 0)),
                const(1, d), const(*w_in.shape), const(1, ng), const(*conv_w.shape)]
    args = [x2, x2, x2, mod3, norm1.reshape(1, d), w_in, gate_bias.reshape(1, ng), conv_w]
    out_specs = [tok(hq), chunked(hq), tok(vw), tok(ng), chunked(ng)]
    out_shape = [jax.ShapeDtypeStruct((n, hq), BF16),
                 jax.ShapeDtypeStruct((n // SUB, hq, SUB), BF16),
                 jax.ShapeDtypeStruct((n, vw), BF16),
                 jax.ShapeDtypeStruct((n, ng), F32),
                 jax.ShapeDtypeStruct((n // SUB, ng, SUB), F32)]
    if with_pool:
        a_c, inv_c, pw_bd, p_scale = pool_parts
        pool_w = c_qk - c_pool
        in_specs += [const(*a_c.shape), const(*inv_c.shape), const(*pw_bd.shape),
                     const(1, pool_w)]
        args += [a_c, inv_c, pw_bd, p_scale.reshape(1, pool_w)]
        out_specs += [tok(pool_w), tok(c_g - c_o)]
        out_shape += [jax.ShapeDtypeStruct((n, pool_w), BF16),
                      jax.ShapeDtypeStruct((n, c_g - c_o), BF16)]
    steps = n // tm
    for wgt in cast_weights:
        hold = 1
        while (wgt.shape[0] * hold) % (steps * BF16_SUBLANES):
            hold *= 2
        slab_rows = wgt.shape[0] * hold // steps
        slab = pl.BlockSpec((slab_rows, wgt.shape[1]), functools.partial(
            lambda i, h: (i // h, 0), h=hold))
        in_specs.append(slab)
        args.append(wgt)
        out_specs.append(slab)
        out_shape.append(jax.ShapeDtypeStruct(wgt.shape, BF16))
    outs = pl.pallas_call(
        functools.partial(_proj_kernel, tm=tm, seq_len=seq_len, cols=cols, with_pool=with_pool,
                          n_cast=len(cast_weights)),
        grid=(steps,),
        in_specs=in_specs,
        out_specs=out_specs,
        out_shape=out_shape,
        scratch_shapes=[pltpu.VMEM((tm + 2 * HALO, d), BF16)],
        compiler_params=pltpu.CompilerParams(
            dimension_semantics=("arbitrary",),
            vmem_limit_bytes=_vmem_limit(48 * 1024 * 1024)),
        name="in_proj_pool" if with_pool else "in_proj_ctx",
    )(*args)
    return list(outs)


def _split3(x):
    hi = x.astype(BF16)
    r1 = x - hi.astype(F32)
    mid = r1.astype(BF16)
    lo = (r1 - mid.astype(F32)).astype(BF16)
    return hi, mid, lo


def _tri_dot_left(tri, x):
    return sum(_dot(tri, p) for p in _split3(x))


def _tri_dot_right(x, tri):
    return sum(_dot(p, tri) for p in _split3(x))


def _gate_tables(g_col, g_row, tril, triu):
    lf_row = _log_sigmoid(g_row) * LOG2E
    lf_col = _log_sigmoid(g_col) * LOG2E
    brow = (_tri_dot_right(lf_row, triu), _tri_dot_right(lf_row, tril))
    bcol = (_tri_dot_left(tril, lf_col), _tri_dot_left(triu, lf_col))
    return g_row * LOG2E, brow, bcol


def _chain_step(q_c, kt_c, v_aug, bcol, arow, mask, last, ct_ref, m0, m_last, want_out):
    lc = kt_c.shape[1]
    ct = ct_ref[...]
    if want_out:
        am = jnp.where(mask, arow, -jnp.inf)
        cm = jnp.max(am, axis=1, keepdims=True)
        mcol = jnp.maximum(m0, jnp.broadcast_to(cm, (lc, DV)))
        b_r = jnp.broadcast_to(bcol, (lc, DV))
        e = jnp.exp2(am - jnp.concatenate([mcol, mcol], axis=1))
        s = _dot(q_c, kt_c) * e
        den_intra = jnp.broadcast_to(jnp.sum(s, axis=1, keepdims=True), (lc, DV))
        p = s.astype(BF16)
        wk = e[last:last + 1, :]
        qs = (q_c.astype(F32) * jnp.exp2(m0 - mcol)[:, :DK]).astype(BF16)
    else:
        wk = jnp.exp2(arow - m_last)
    kw = (kt_c.astype(F32) * wk).astype(BF16)
    if want_out:
        r = _dot(jnp.concatenate([p, kw], axis=0), v_aug)
        r2 = _dot(qs, ct.astype(BF16))
        num = r[:lc, :DV] + r2[:, :DV]
        den = den_intra + r2[:, DV:]
        hout = num / jnp.maximum(jnp.abs(den), jnp.exp2(-(b_r + mcol)))
        d_ct = r[lc:, :]
    else:
        hout = None
        d_ct = _dot(kw, v_aug)
    ct_ref[...] = jnp.exp2(m0 - m_last) * ct + d_ct
    return hout


def _mlstm_kernel(qx_ref, ktx_ref, vx_ref, ox_ref, gx_ref, gtx_ref,
                  qc_ref, ktc_ref, vc_ref, gc_ref, gtc_ref, hn_ref, out_ref,
                  hf_scr, hb_scr, st_scr, gpx_scr, gpc_scr, *, t_x, t_c):
    lc = SUB
    nh = N_HEADS
    ng = N_GATES
    ri = lax.broadcasted_iota(jnp.int32, (lc, lc), 0)
    ci = lax.broadcasted_iota(jnp.int32, (lc, lc), 1)
    lower = ci <= ri
    upper = ci >= ri
    tril = lower.astype(BF16)
    triu = upper.astype(BF16)
    ones_blk = jnp.ones((lc, DV), BF16)
    dirs = ((lower, lc - 1), (upper, 0))

    st_scr[...] = jnp.zeros(st_scr.shape, F32)

    def tables(g_ref, gt_ref, gp_scr):
        for c in range(g_ref.shape[1]):
            gp_scr[:, c * ng:(c + 1) * ng] = g_ref[0, c]
        return _gate_tables(gp_scr[...], gt_ref[0], tril, triu)

    def gate_slices(tabs, cidx, di, h):
        g_row, brow, bcol = tabs
        col_i = cidx * ng + di * nh + h
        col_f = col_i + 2 * nh
        arow = g_row[col_i:col_i + 1, :] - brow[di][col_f:col_f + 1, :]
        return arow, bcol[di][:, col_f:col_f + 1]

    def stabilisers(order, tabs, m_run):
        plan = []
        for i in range(len(order[0])):
            step = []
            for di in range(2):
                _, last = dirs[di]
                for h in range(nh):
                    arow, bcol = gate_slices(tabs, order[di][i], di, h)
                    m0 = m_run[di * nh + h]
                    m_last = jnp.maximum(m0, jnp.max(arow, axis=1, keepdims=True))
                    step.append((m0, m_last))
                    m_run[di * nh + h] = bcol[last:last + 1, :] + m_last
            plan.append(step)
        return plan

    def chunk_pair(cf, cb, q_ref, kt_ref, v_ref, tabs, ms, want_out):
        for di, cidx in enumerate((cf, cb)):
            mask, last = dirs[di]
            r0 = cidx * lc
            for h in range(nh):
                arow, bcol = gate_slices(tabs, cidx, di, h)
                q_c = q_ref[0, r0:r0 + lc, h * DK:(h + 1) * DK]
                kt_c = kt_ref[0, cidx, h * DK:(h + 1) * DK, :]
                v_c = v_ref[0, r0:r0 + lc, h * DV:(h + 1) * DV]
                v_aug = jnp.concatenate([v_c, ones_blk], axis=1)
                m0, m_last = ms[di * nh + h]
                hout = _chain_step(q_c, kt_c, v_aug, bcol, arow, mask, last,
                                   st_scr.at[di * nh + h], m0, m_last, want_out)
                if want_out:
                    dst = hf_scr if di == 0 else hb_scr
                    dst[r0:r0 + lc, h * DV:(h + 1) * DV] = hout

    m_run = [jnp.zeros((1, 1), F32) for _ in range(2 * nh)]
    nc_c = t_c // lc
    tabs_c = tables(gc_ref, gtc_ref, gpc_scr)
    order_c = (list(range(nc_c)), list(range(nc_c - 1, -1, -1)))
    plan_c = stabilisers(order_c, tabs_c, m_run)
    nc_x = t_x // lc
    tabs_x = tables(gx_ref, gtx_ref, gpx_scr)
    order_x = (list(range(nc_x)), list(range(nc_x - 1, -1, -1)))
    plan_x = stabilisers(order_x, tabs_x, m_run)
    for i in range(nc_c):
        chunk_pair(order_c[0][i], order_c[1][i], qc_ref, ktc_ref, vc_ref, tabs_c, plan_c[i],
                   False)
    for i in range(nc_x):
        pl.when(pl.program_id(0) >= 0)(functools.partial(
            chunk_pair, order_x[0][i], order_x[1][i], qx_ref, ktx_ref, vx_ref, tabs_x,
            plan_x[i], True))

    for r0 in range(0, t_x, lc):
        hm = hf_scr[r0:r0 + lc, :] + hb_scr[r0:r0 + lc, :]
        parts = [_rms(hm[:, h * DV:(h + 1) * DV]) for h in range(nh)]
        hnorm = jnp.concatenate(parts, axis=1)
        gate = _sigmoid(ox_ref[0, r0:r0 + lc, :].astype(F32))
        out_ref[0, r0:r0 + lc, :] = (hnorm * hn_ref[...] * gate).astype(BF16)


def _mlstm(qx, ktx, vx, ox, gx, gtx, qc, ktc, vc, gc, gtc, head_norm):
    bsz, t_x, _ = qx.shape
    t_c = qc.shape[1]
    vw = vx.shape[2]
    ng = gx.shape[-1]
    bspec = lambda a: pl.BlockSpec((1,) + a.shape[1:], lambda b: (b,) + (0,) * (a.ndim - 1))
    args = (qx, ktx, vx, ox, gx, gtx, qc, ktc, vc, gc, gtc)
    return pl.pallas_call(
        functools.partial(_mlstm_kernel, t_x=t_x, t_c=t_c),
        grid=(bsz,),
        in_specs=[bspec(a) for a in args] + [pl.BlockSpec((1, vw), lambda b: (0, 0))],
        out_specs=pl.BlockSpec((1, t_x, vw), lambda b: (b, 0, 0)),
        out_shape=jax.ShapeDtypeStruct((bsz, t_x, vw), BF16),
        scratch_shapes=[
            pltpu.VMEM((t_x, vw), F32),
            pltpu.VMEM((t_x, vw), F32),
            pltpu.VMEM((2 * N_HEADS, DK, 2 * DV), F32),
            pltpu.VMEM((SUB, (t_x // SUB) * ng), F32),
            pltpu.VMEM((SUB, (t_c // SUB) * ng), F32),
        ],
        compiler_params=pltpu.CompilerParams(
            dimension_semantics=("arbitrary",),
            vmem_limit_bytes=_vmem_limit(58 * 1024 * 1024)),
        name="mlstm",
    )(*args, head_norm.reshape(1, vw))


def _staggered(n_items, stages):
    state = [None] * n_items
    for step in range(len(stages) + n_items - 1):
        for i in range(n_items):
            k = step - i
            if 0 <= k < len(stages):
                state[i] = stages[k](i, state[i])


def _out_ffn_kernel(x_ref, pool_ref, ml_ref, mod_ref, wo_ref, n2_ref, wu_ref,
                    wd_ref, nf_ref, out_ref):
    d = x_ref.shape[-1]
    tm = x_ref.shape[0]
    dff = wd_ref.shape[0]
    g1 = mod_ref[0, :, 2 * d:3 * d]
    sh2 = mod_ref[0, :, 3 * d:4 * d]
    gain2 = n2_ref[...] * (1.0 + mod_ref[0, :, 4 * d:5 * d])
    g2 = mod_ref[0, :, 5 * d:6 * d]
    pw = pool_ref.shape[-1]
    rows = [slice(r0, r0 + SUB) for r0 in range(0, tm, SUB)]

    def out_proj(i, _):
        r = rows[i]
        return _dot(pool_ref[r, :], wo_ref[0:pw, :]) + _dot(ml_ref[r, :], wo_ref[pw:, :])

    def norm2(i, mix):
        x1 = x_ref[rows[i], :] + g1 * mix
        return x1, (_rms(x1) * gain2 + sh2).astype(BF16)

    def up(i, st):
        x1, h2 = st
        return x1, _dot(h2, wu_ref[:, 0:dff]), _dot(h2, wu_ref[:, dff:2 * dff])

    def act(i, st):
        x1, gg, aa = st
        return x1, (_silu(gg) * aa).astype(BF16)

    def down(i, st):
        x1, a = st
        return x1, _dot(a, wd_ref[...])

    def final(i, st):
        x1, acc = st
        x2 = x1 + g2 * acc
        out_ref[rows[i], :] = _rms(x2) * nf_ref[...]

    _staggered(len(rows), [out_proj, norm2, up, act, down, final])


def _out_ffn(x2d, pool, ml, mod3, w_out, norm2, w_up, w_down, norm_f, tm, tpb):
    n, d = x2d.shape
    tok = lambda w: pl.BlockSpec((tm, w), lambda i: (i, 0))
    const = lambda *shape: pl.BlockSpec(shape, lambda i: (0,) * len(shape),
                                        pipeline_mode=pl.Buffered(1))
    return pl.pallas_call(
        _out_ffn_kernel,
        grid=(n // tm,),
        in_specs=[tok(d), tok(pool.shape[1]), tok(ml.shape[1]),
                  pl.BlockSpec((1, 1, mod3.shape[-1]), lambda i: (i // tpb, 0, 0)),
                  const(*w_out.shape), const(1, d), const(*w_up.shape),
                  const(*w_down.shape), const(1, d)],
        out_specs=tok(d),
        out_shape=jax.ShapeDtypeStruct((n, d), F32),
        compiler_params=pltpu.CompilerParams(
            dimension_semantics=("arbitrary",),
            vmem_limit_bytes=_vmem_limit(56 * 1024 * 1024)),
        name="out_ffn",
    )(x2d, pool, ml, mod3, w_out, norm2.reshape(1, d), w_up, w_down, norm_f.reshape(1, d))


def kernel(x, c, ctx, c_ctx, w_ada, b_ada, norm1, w_in, conv_qk, gate_bias, pool_w,
           pool_scale, head_norm, w_out, norm2, w_up, w_down, norm_f):
    bsz, t, d = x.shape
    t_c = ctx.shape[1]
    assert w_ada.shape[0] == 1, "single-layer block"
    pool_width = pool_w.shape[1] * pool_w.shape[2]
    qk_width = 2 * N_HEADS * DK
    ml_width = N_HEADS * DV
    o1 = pool_width
    o2 = o1 + qk_width
    o3 = o2 + ml_width
    o4 = o3 + ml_width
    cols = (0, o1, o2, o3, o4, o4 + N_GATES)
    tm = PROJ_TILE
    assert t % tm == 0 and (bsz * t_c) % tm == 0 and t_c % SUB == 0 and t % OUT_TILE == 0
    assert w_in.shape[2] == cols[-1] and w_up.shape[2] == 2 * w_down.shape[1]

    cc = jnp.concatenate([c, c_ctx[None, :]], axis=0)
    mod = _ada(cc, w_ada[0], b_ada[0])
    mod3 = mod.reshape(bsz + 1, 1, mod.shape[-1])

    w_in_b = w_in[0].astype(BF16)
    a_np, inv_np = _pool_consts()
    gd = pool_w.shape[2]
    pw = pool_w[0].astype(BF16)
    zero = jnp.zeros((gd, gd), BF16)
    pw_bd = jnp.stack([jnp.block([[pw[0], zero], [zero, pw[1]]]),
                       jnp.block([[pw[2], zero], [zero, pw[3]]])])
    pool_parts = (jnp.asarray(a_np, BF16), jnp.asarray(inv_np, F32), pw_bd, pool_scale[0])

    tpb = t // tm
    q_x, kt_x, v_x, g_x, gt_x, pool_x, o_x, w_out_b, w_up_b, w_down_b = _in_proj(
        x.reshape(bsz * t, d), t, mod3, lambda i: i // tpb, norm1[0], w_in_b, cols,
        gate_bias[0], conv_qk[0], pool_parts, tm, cast_weights=(w_out[0], w_up[0], w_down[0]))
    q_c, kt_c, v_c, g_c, gt_c = _in_proj(
        ctx.reshape(bsz * t_c, d), t_c, mod3, lambda i: bsz, norm1[0], w_in_b, cols,
        gate_bias[0], conv_qk[0], None, tm)

    def per_seq(a, tt):
        return a.reshape((bsz, tt) + a.shape[1:])

    def per_seq_chunks(a, tt, merge=False):
        nc = tt // SUB
        if merge:
            return a.reshape(bsz, nc * a.shape[1], a.shape[2])
        return a.reshape((bsz, nc) + a.shape[1:])

    ml = _mlstm(per_seq(q_x, t), per_seq_chunks(kt_x, t), per_seq(v_x, t), per_seq(o_x, t),
                g_x.reshape(bsz, t // SUB, SUB, N_GATES), per_seq_chunks(gt_x, t, merge=True),
                per_seq(q_c, t_c), per_seq_chunks(kt_c, t_c), per_seq(v_c, t_c),
                g_c.reshape(bsz, t_c // SUB, SUB, N_GATES), per_seq_chunks(gt_c, t_c, merge=True),
                head_norm[0])

    out = _out_ffn(x.reshape(bsz * t, d), pool_x, ml.reshape(bsz * t, ml_width), mod3,
                   w_out_b, norm2[0], w_up_b, w_down_b, norm_f, OUT_TILE, t // OUT_TILE)
    return out.reshape(bsz, t, d)
```

```python
import functools

import numpy as np
import jax
import jax.numpy as jnp
from jax import lax
from jax.experimental import pallas as pl
from jax.experimental.pallas import tpu as pltpu

F32 = jnp.float32
BF16 = jnp.bfloat16

EPS = 1e-6
GRID_W = 64
POOL_WINDOWS = (2, 4, 8, 16)
N_HEADS = 4
DK = 64
DV = 128
N_GATES = 4 * N_HEADS
CONV_W = 3
LOG2E = 1.4426950408889634

V7X_VMEM_BYTES = 64 * 1024 * 1024
V7X_MXU_DIM = 256
BF16_SUBLANES = 16

SUB = V7X_MXU_DIM
PROJ_TILE = 2048
OUT_TILE = 1024
HALO = BF16_SUBLANES


def _vmem_limit(est_bytes):
    return int(min(V7X_VMEM_BYTES - 6 * 1024 * 1024, est_bytes))


def _dot(a, b):
    return jnp.dot(a, b, preferred_element_type=F32)


def _sigmoid(x):
    return 1.0 / (1.0 + jnp.exp(-x))


def _silu(x):
    return x * _sigmoid(x)


def _log_sigmoid(x):
    return jnp.minimum(x, 0.0) - jnp.log(1.0 + jnp.exp(-jnp.abs(x)))


def _rms(x):
    return x * lax.rsqrt(jnp.mean(x * x, axis=-1, keepdims=True) + EPS)


def _ada_kernel(c_ref, w_ref, b_ref, o_ref):
    s = _silu(c_ref[...]).astype(BF16)
    o_ref[...] = _dot(s, w_ref[...].astype(BF16)) + b_ref[...]


def _ada(cc, w_ada, b_ada):
    rows, d = cc.shape
    n = w_ada.shape[1]
    bn = 1536
    return pl.pallas_call(
        _ada_kernel,
        grid=(n // bn,),
        in_specs=[
            pl.BlockSpec((rows, d), lambda j: (0, 0)),
            pl.BlockSpec((d, bn), lambda j: (0, j)),
            pl.BlockSpec((1, bn), lambda j: (0, j)),
        ],
        out_specs=pl.BlockSpec((rows, bn), lambda j: (0, j)),
        out_shape=jax.ShapeDtypeStruct((rows, n), F32),
        compiler_params=pltpu.CompilerParams(
            dimension_semantics=("arbitrary",),
            vmem_limit_bytes=_vmem_limit(40 * 1024 * 1024)),
        name="ada",
    )(cc, w_ada, b_ada.reshape(1, n))


def _pool_consts():
    pos = np.arange(SUB) % GRID_W
    row = np.arange(SUB) // GRID_W
    a = np.zeros((len(POOL_WINDOWS), SUB, SUB), np.float32)
    inv = np.zeros((SUB, len(POOL_WINDOWS)), np.float32)
    for gi, win in enumerate(POOL_WINDOWS):
        lo = np.clip(pos - win // 2, 0, GRID_W - 1)
        hi = np.clip(pos + win // 2 - 1, 0, GRID_W - 1)
        for j in range(SUB):
            a[gi, j, row[j] * GRID_W + lo[j]: row[j] * GRID_W + hi[j] + 1] = 1.0
        inv[:, gi] = 1.0 / (hi - lo + 1)
    return a, inv


def _proj_kernel(*refs, tm, seq_len, cols, with_pool, n_cast):
    n_in = (12 if with_pool else 8) + n_cast
    cast_in = refs[n_in - n_cast:n_in]
    cast_out = refs[len(refs) - 1 - n_cast:len(refs) - 1]
    refs = refs[:n_in - n_cast] + refs[n_in:len(refs) - 1 - n_cast] + refs[-1:]
    for src, dst in zip(cast_in, cast_out):
        dst[...] = src[...].astype(BF16)
    if with_pool:
        (x_ref, xp_ref, xn_ref, mod_ref, n1_ref, w_ref, gb_ref, cw_ref,
         a_ref, inv_ref, pw_ref, ps_ref,
         q_ref, kt_ref, v_ref, g_ref, gt_ref, pool_ref, o_ref, lhs_scr) = refs
    else:
        (x_ref, xp_ref, xn_ref, mod_ref, n1_ref, w_ref, gb_ref, cw_ref,
         q_ref, kt_ref, v_ref, g_ref, gt_ref, lhs_scr) = refs
    d = x_ref.shape[-1]
    sub = SUB
    n_sub = tm // sub
    c_pool, c_qk, c_v, c_o, c_g, c_end = cols
    sh = mod_ref[0, :, 0:d]
    gain = n1_ref[...] * (1.0 + mod_ref[0, :, d:2 * d])
    qw = q_ref.shape[-1]
    row = lax.broadcasted_iota(jnp.int32, (sub, 1), 0)

    def mod_norm(xv):
        return (_rms(xv) * gain + sh).astype(BF16)

    def conv_act(cur, prev_last, next_first, k):
        start = pl.program_id(0) * tm + k * sub
        prev_last = jnp.where(start % seq_len == 0, 0.0, prev_last)
        next_first = jnp.where((start + sub) % seq_len == 0, 0.0, next_first)
        dn = jnp.where(row == 0, prev_last, pltpu.roll(cur, 1, axis=0))
        up = jnp.where(row == sub - 1, next_first, pltpu.roll(cur, sub - 1, axis=0))
        act = _silu(cw_ref[0:1, :] * dn + cw_ref[1:2, :] * cur + cw_ref[2:3, :] * up)
        q_ref[k * sub:(k + 1) * sub, :] = (act[:, :qw] * (DK ** -0.5)).astype(BF16)
        kt_ref[k] = act[:, qw:].T.astype(BF16)

    def side_dots(hb, k):
        rows = slice(k * sub, (k + 1) * sub)
        v_ref[rows, :] = _dot(hb, w_ref[:, c_v:c_o]).astype(BF16)
        g = _dot(hb, w_ref[:, c_g:c_end]) + gb_ref[...]
        g_ref[rows, :] = g
        gt_ref[k] = g.T
        if with_pool:
            o_ref[rows, :] = _dot(hb, w_ref[:, c_o:c_g]).astype(BF16)
            return _dot(hb, w_ref[:, c_pool:c_qk])
        return None

    def pool_mix(u, k):
        gd = u.shape[1] // len(POOL_WINDOWS)
        parts = []
        for gi in range(len(POOL_WINDOWS)):
            ug = u[:, gi * gd:(gi + 1) * gd]
            win = _dot(a_ref[gi], ug.astype(BF16))
            parts.append(win * inv_ref[:, gi:gi + 1] - ug)
        dmat = jnp.concatenate(parts, axis=1).astype(BF16)
        half = dmat.shape[1] // 2
        out = jnp.concatenate(
            [_dot(dmat[:, :half], pw_ref[0]), _dot(dmat[:, half:], pw_ref[1])], axis=1)
        pool_ref[k * sub:(k + 1) * sub, :] = (out * ps_ref[...]).astype(BF16)

    ps, us = [None] * n_sub, [None] * n_sub

    def own_rows(k):
        lo = HALO if k == 0 else 0
        return lo, lo + sub

    def conv(k):
        lo, hi = own_rows(k)
        if k == 0:
            prev_last = ps[0][lo - 1:lo, :]
        else:
            prev_last = ps[k - 1][own_rows(k - 1)[1] - 1:own_rows(k - 1)[1], :]
        if k == n_sub - 1:
            next_first = ps[k][hi:hi + 1, :]
        else:
            next_first = ps[k + 1][own_rows(k + 1)[0]:own_rows(k + 1)[0] + 1, :]
        conv_act(ps[k][lo:hi, :], prev_last, next_first, k)

    for k in range(n_sub):
        r0 = HALO + k * sub
        lhs_scr[r0:r0 + sub, :] = mod_norm(x_ref[k * sub:(k + 1) * sub, :])
        lo, hi = r0, r0 + sub
        if k == 0:
            lhs_scr[0:HALO, :] = mod_norm(xp_ref[...])
            lo = 0
        if k == n_sub - 1:
            lhs_scr[hi:hi + HALO, :] = mod_norm(xn_ref[...])
            hi += HALO
        ps[k] = _dot(lhs_scr[lo:hi, :], w_ref[:, c_qk:c_v])
        if k > 0:
            conv(k - 1)
        us[k] = side_dots(lhs_scr[r0:r0 + sub, :], k)
        if k > 0 and with_pool:
            pool_mix(us[k - 1], k - 1)
    conv(n_sub - 1)
    if with_pool:
        pool_mix(us[n_sub - 1], n_sub - 1)


def _in_proj(x2, seq_len, mod3, mod_row_fn, norm1, w_in, cols, gate_bias, conv_w, pool_parts, tm,
             cast_weights=()):
    n, d = x2.shape
    with_pool = pool_parts is not None
    c_pool, c_qk, c_v, c_o, c_g, c_end = cols
    ng = c_end - c_g
    hq = (c_v - c_qk) // 2
    vw = c_o - c_v
    hpt = tm // HALO
    const = lambda *shape: pl.BlockSpec(shape, lambda i: (0,) * len(shape))
    tok = lambda w: pl.BlockSpec((tm, w), lambda i: (i, 0))
    chunked = lambda r: pl.BlockSpec((tm // SUB, r, SUB), lambda i: (i, 0, 0))
    in_specs = [tok(d),
                pl.BlockSpec((HALO, d), lambda i: (jnp.maximum(i * hpt - 1, 0), 0)),
                pl.BlockSpec((HALO, d), lambda i: (jnp.minimum((i + 1) * hpt, n // HALO - 1), 0)),
                pl.BlockSpec((1, 1, mod3.shape[-1]), lambda i: (mod_row_fn(i), 0, 0)),
                const(1, d), const(*w_in.shape), const(1, ng), const(*conv_w.shape)]
    args = [x2, x2, x2, mod3, norm1.reshape(1, d), w_in, gate_bias.reshape(1, ng), conv_w]
    out_specs = [tok(hq), chunked(hq), tok(vw), tok(ng), chunked(ng)]
    out_shape = [jax.ShapeDtypeStruct((n, hq), BF16),
                 jax.ShapeDtypeStruct((n // SUB, hq, SUB), BF16),
                 jax.ShapeDtypeStruct((n, vw), BF16),
                 jax.ShapeDtypeStruct((n, ng), F32),
                 jax.ShapeDtypeStruct((n // SUB, ng, SUB), F32)]
    if with_pool:
        a_c, inv_c, pw_bd, p_scale = pool_parts
        pool_w = c_qk - c_pool
        in_specs += [const(*a_c.shape), const(*inv_c.shape), const(*pw_bd.shape),
                     const(1, pool_w)]
        args += [a_c, inv_c, pw_bd, p_scale.reshape(1, pool_w)]
        out_specs += [tok(pool_w), tok(c_g - c_o)]
        out_shape += [jax.ShapeDtypeStruct((n, pool_w), BF16),
                      jax.ShapeDtypeStruct((n, c_g - c_o), BF16)]
    steps = n // tm
    for wgt in cast_weights:
        hold = 1
        while (wgt.shape[0] * hold) % (steps * BF16_SUBLANES):
            hold *= 2
        slab_rows = wgt.shape[0] * hold // steps
        slab = pl.BlockSpec((slab_rows, wgt.shape[1]), functools.partial(
            lambda i, h: (i // h, 0), h=hold))
        in_specs.append(slab)
        args.append(wgt)
        out_specs.append(slab)
        out_shape.append(jax.ShapeDtypeStruct(wgt.shape, BF16))
    outs = pl.pallas_call(
        functools.partial(_proj_kernel, tm=tm, seq_len=seq_len, cols=cols, with_pool=with_pool,
                          n_cast=len(cast_weights)),
        grid=(steps,),
        in_specs=in_specs,
        out_specs=out_specs,
        out_shape=out_shape,
        scratch_shapes=[pltpu.VMEM((tm + 2 * HALO, d), BF16)],
        compiler_params=pltpu.CompilerParams(
            dimension_semantics=("arbitrary",),
            vmem_limit_bytes=_vmem_limit(56 * 1024 * 1024)),
        name="in_proj_pool" if with_pool else "in_proj_ctx",
    )(*args)
    return list(outs)


def _split3(x):
    hi = x.astype(BF16)
    r1 = x - hi.astype(F32)
    mid = r1.astype(BF16)
    lo = (r1 - mid.astype(F32)).astype(BF16)
    return hi, mid, lo


def _tri_dot_left(tri, x):
    return sum(_dot(tri, p) for p in _split3(x))


def _tri_dot_right(x, tri):
    return sum(_dot(p, tri) for p in _split3(x))


def _gate_tables(g_col, g_row, tril, triu):
    lf_row = _log_sigmoid(g_row) * LOG2E
    lf_col = _log_sigmoid(g_col) * LOG2E
    brow = (_tri_dot_right(lf_row, triu), _tri_dot_right(lf_row, tril))
    bcol = (_tri_dot_left(tril, lf_col), _tri_dot_left(triu, lf_col))
    return g_row * LOG2E, brow, bcol


def _chain_step(q_c, kt_c, v_aug, bcol, arow, mask, last, ct_ref, m0, m_last, want_out):
    lc = kt_c.shape[1]
    ct = ct_ref[...]
    if want_out:
        am = jnp.where(mask, arow, -jnp.inf)
        cm = jnp.max(am, axis=1, keepdims=True)
        mcol = jnp.maximum(m0, jnp.broadcast_to(cm, (lc, DV)))
        b_r = jnp.broadcast_to(bcol, (lc, DV))
        e = jnp.exp2(am - jnp.concatenate([mcol, mcol], axis=1))
        s = _dot(q_c, kt_c) * e
        den_intra = jnp.broadcast_to(jnp.sum(s, axis=1, keepdims=True), (lc, DV))
        p = s.astype(BF16)
        wk = e[last:last + 1, :]
        qs = (q_c.astype(F32) * jnp.exp2(m0 - mcol)[:, :DK]).astype(BF16)
    else:
        wk = jnp.exp2(arow - m_last)
    kw = (kt_c.astype(F32) * wk).astype(BF16)
    if want_out:
        r = _dot(jnp.concatenate([p, kw], axis=0), v_aug)
        r2 = _dot(qs, ct.astype(BF16))
        num = r[:lc, :DV] + r2[:, :DV]
        den = den_intra + r2[:, DV:]
        hout = num / jnp.maximum(jnp.abs(den), jnp.exp2(-(b_r + mcol)))
        d_ct = r[lc:, :]
    else:
        hout = None
        d_ct = _dot(kw, v_aug)
    ct_ref[...] = jnp.exp2(m0 - m_last) * ct + d_ct
    return hout


def _mlstm_kernel(qx_ref, ktx_ref, vx_ref, ox_ref, gx_ref, gtx_ref,
                  qc_ref, ktc_ref, vc_ref, gc_ref, gtc_ref, hn_ref, out_ref,
                  hf_scr, hb_scr, st_scr, gpx_scr, gpc_scr, *, t_x, t_c):
    lc = SUB
    nh = N_HEADS
    ng = N_GATES
    ri = lax.broadcasted_iota(jnp.int32, (lc, lc), 0)
    ci = lax.broadcasted_iota(jnp.int32, (lc, lc), 1)
    lower = ci <= ri
    upper = ci >= ri
    tril = lower.astype(BF16)
    triu = upper.astype(BF16)
    ones_blk = jnp.ones((lc, DV), BF16)
    dirs = ((lower, lc - 1), (upper, 0))

    st_scr[...] = jnp.zeros(st_scr.shape, F32)

    def tables(g_ref, gt_ref, gp_scr):
        for c in range(g_ref.shape[1]):
            gp_scr[:, c * ng:(c + 1) * ng] = g_ref[0, c]
        return _gate_tables(gp_scr[...], gt_ref[0], tril, triu)

    def gate_slices(tabs, cidx, di, h):
        g_row, brow, bcol = tabs
        col_i = cidx * ng + di * nh + h
        col_f = col_i + 2 * nh
        arow = g_row[col_i:col_i + 1, :] - brow[di][col_f:col_f + 1, :]
        return arow, bcol[di][:, col_f:col_f + 1]

    def stabilisers(order, tabs, m_run):
        plan = []
        for i in range(len(order[0])):
            step = []
            for di in range(2):
                _, last = dirs[di]
                for h in range(nh):
                    arow, bcol = gate_slices(tabs, order[di][i], di, h)
                    m0 = m_run[di * nh + h]
                    m_last = jnp.maximum(m0, jnp.max(arow, axis=1, keepdims=True))
                    step.append((m0, m_last))
                    m_run[di * nh + h] = bcol[last:last + 1, :] + m_last
            plan.append(step)
        return plan

    def chunk_pair(cf, cb, q_ref, kt_ref, v_ref, tabs, ms, want_out):
        for di, cidx in enumerate((cf, cb)):
            mask, last = dirs[di]
            r0 = cidx * lc
            for h in range(nh):
                arow, bcol = gate_slices(tabs, cidx, di, h)
                q_c = q_ref[0, r0:r0 + lc, h * DK:(h + 1) * DK]
                kt_c = kt_ref[0, cidx, h * DK:(h + 1) * DK, :]
                v_c = v_ref[0, r0:r0 + lc, h * DV:(h + 1) * DV]
                v_aug = jnp.concatenate([v_c, ones_blk], axis=1)
                m0, m_last = ms[di * nh + h]
                hout = _chain_step(q_c, kt_c, v_aug, bcol, arow, mask, last,
                                   st_scr.at[di * nh + h], m0, m_last, want_out)
                if want_out:
                    dst = hf_scr if di == 0 else hb_scr
                    dst[r0:r0 + lc, h * DV:(h + 1) * DV] = hout

    m_run = [jnp.zeros((1, 1), F32) for _ in range(2 * nh)]
    nc_c = t_c // lc
    tabs_c = tables(gc_ref, gtc_ref, gpc_scr)
    order_c = (list(range(nc_c)), list(range(nc_c - 1, -1, -1)))
    plan_c = stabilisers(order_c, tabs_c, m_run)
    nc_x = t_x // lc
    tabs_x = tables(gx_ref, gtx_ref, gpx_scr)
    order_x = (list(range(nc_x)), list(range(nc_x - 1, -1, -1)))
    plan_x = stabilisers(order_x, tabs_x, m_run)
    for i in range(nc_c):
        chunk_pair(order_c[0][i], order_c[1][i], qc_ref, ktc_ref, vc_ref, tabs_c, plan_c[i],
                   False)
    for i in range(nc_x):
        pl.when(pl.program_id(0) >= 0)(functools.partial(
            chunk_pair, order_x[0][i], order_x[1][i], qx_ref, ktx_ref, vx_ref, tabs_x,
            plan_x[i], True))

    for r0 in range(0, t_x, lc):
        hm = hf_scr[r0:r0 + lc, :] + hb_scr[r0:r0 + lc, :]
        parts = [_rms(hm[:, h * DV:(h + 1) * DV]) for h in range(nh)]
        hnorm = jnp.concatenate(parts, axis=1)
        gate = _sigmoid(ox_ref[0, r0:r0 + lc, :].astype(F32))
        out_ref[0, r0:r0 + lc, :] = (hnorm * hn_ref[...] * gate).astype(BF16)


def _mlstm(qx, ktx, vx, ox, gx, gtx, qc, ktc, vc, gc, gtc, head_norm):
    bsz, t_x, _ = qx.shape
    t_c = qc.shape[1]
    vw = vx.shape[2]
    ng = gx.shape[-1]
    bspec = lambda a: pl.BlockSpec((1,) + a.shape[1:], lambda b: (b,) + (0,) * (a.ndim - 1))
    args = (qx, ktx, vx, ox, gx, gtx, qc, ktc, vc, gc, gtc)
    return pl.pallas_call(
        functools.partial(_mlstm_kernel, t_x=t_x, t_c=t_c),
        grid=(bsz,),
        in_specs=[bspec(a) for a in args] + [pl.BlockSpec((1, vw), lambda b: (0, 0))],
        out_specs=pl.BlockSpec((1, t_x, vw), lambda b: (b, 0, 0)),
        out_shape=jax.ShapeDtypeStruct((bsz, t_x, vw), BF16),
        scratch_shapes=[
            pltpu.VMEM((t_x, vw), F32),
            pltpu.VMEM((t_x, vw), F32),
            pltpu.VMEM((2 * N_HEADS, DK, 2 * DV), F32),
            pltpu.VMEM((SUB, (t_x // SUB) * ng), F32),
            pltpu.VMEM((SUB, (t_c // SUB) * ng), F32),
        ],
        compiler_params=pltpu.CompilerParams(
            dimension_semantics=("arbitrary",),
            vmem_limit_bytes=_vmem_limit(58 * 1024 * 1024)),
        name="mlstm",
    )(*args, head_norm.reshape(1, vw))


def _staggered(n_items, stages):
    state = [None] * n_items
    for step in range(len(stages) + n_items - 1):
        for i in range(n_items):
            k = step - i
            if 0 <= k < len(stages):
                state[i] = stages[k](i, state[i])


def _out_ffn_kernel(x_ref, pool_ref, ml_ref, mod_ref, wo_ref, n2_ref, wu_ref,
                    wd_ref, nf_ref, out_ref):
    d = x_ref.shape[-1]
    tm = x_ref.shape[0]
    dff = wd_ref.shape[0]
    g1 = mod_ref[0, :, 2 * d:3 * d]
    sh2 = mod_ref[0, :, 3 * d:4 * d]
    gain2 = n2_ref[...] * (1.0 + mod_ref[0, :, 4 * d:5 * d])
    g2 = mod_ref[0, :, 5 * d:6 * d]
    pw = pool_ref.shape[-1]
    rows = [slice(r0, r0 + SUB) for r0 in range(0, tm, SUB)]

    def out_proj(i, _):
        r = rows[i]
        return _dot(pool_ref[r, :], wo_ref[0:pw, :]) + _dot(ml_ref[r, :], wo_ref[pw:, :])

    def norm2(i, mix):
        x1 = x_ref[rows[i], :] + g1 * mix
        return x1, (_rms(x1) * gain2 + sh2).astype(BF16)

    def up(i, st):
        x1, h2 = st
        return x1, _dot(h2, wu_ref[:, 0:dff]), _dot(h2, wu_ref[:, dff:2 * dff])

    def act(i, st):
        x1, gg, aa = st
        return x1, (_silu(gg) * aa).astype(BF16)

    def down(i, st):
        x1, a = st
        return x1, _dot(a, wd_ref[...])

    def final(i, st):
        x1, acc = st
        x2 = x1 + g2 * acc
        out_ref[rows[i], :] = _rms(x2) * nf_ref[...]

    _staggered(len(rows), [out_proj, norm2, up, act, down, final])


def _out_ffn(x2d, pool, ml, mod3, w_out, norm2, w_up, w_down, norm_f, tm, tpb):
    n, d = x2d.shape
    tok = lambda w: pl.BlockSpec((tm, w), lambda i: (i, 0))
    const = lambda *shape: pl.BlockSpec(shape, lambda i: (0,) * len(shape),
                                        pipeline_mode=pl.Buffered(1))
    return pl.pallas_call(
        _out_ffn_kernel,
        grid=(n // tm,),
        in_specs=[tok(d), tok(pool.shape[1]), tok(ml.shape[1]),
                  pl.BlockSpec((1, 1, mod3.shape[-1]), lambda i: (i // tpb, 0, 0)),
                  const(*w_out.shape), const(1, d), const(*w_up.shape),
                  const(*w_down.shape), const(1, d)],
        out_specs=tok(d),
        out_shape=jax.ShapeDtypeStruct((n, d), F32),
        compiler_params=pltpu.CompilerParams(
            dimension_semantics=("arbitrary",),
            vmem_limit_bytes=_vmem_limit(56 * 1024 * 1024)),
        name="out_ffn",
    )(x2d, pool, ml, mod3, w_out, norm2.reshape(1, d), w_up, w_down, norm_f.reshape(1, d))


def kernel(x, c, ctx, c_ctx, w_ada, b_ada, norm1, w_in, conv_qk, gate_bias, pool_w,
           pool_scale, head_norm, w_out, norm2, w_up, w_down, norm_f):
    bsz, t, d = x.shape
    t_c = ctx.shape[1]
    assert w_ada.shape[0] == 1, "single-layer block"
    pool_width = pool_w.shape[1] * pool_w.shape[2]
    qk_width = 2 * N_HEADS * DK
    ml_width = N_HEADS * DV
    o1 = pool_width
    o2 = o1 + qk_width
    o3 = o2 + ml_width
    o4 = o3 + ml_width
    cols = (0, o1, o2, o3, o4, o4 + N_GATES)
    tm = PROJ_TILE
    assert t % tm == 0 and (bsz * t_c) % tm == 0 and t_c % SUB == 0 and t % OUT_TILE == 0
    assert w_in.shape[2] == cols[-1] and w_up.shape[2] == 2 * w_down.shape[1]

    cc = jnp.concatenate([c, c_ctx[None, :]], axis=0)
    mod = _ada(cc, w_ada[0], b_ada[0])
    mod3 = mod.reshape(bsz + 1, 1, mod.shape[-1])

    w_in_b = w_in[0].astype(BF16)
    a_np, inv_np = _pool_consts()
    gd = pool_w.shape[2]
    pw = pool_w[0].astype(BF16)
    zero = jnp.zeros((gd, gd), BF16)
    pw_bd = jnp.stack([jnp.block([[pw[0], zero], [zero, pw[1]]]),
                       jnp.block([[pw[2], zero], [zero, pw[3]]])])
    pool_parts = (jnp.asarray(a_np, BF16), jnp.asarray(inv_np, F32), pw_bd, pool_scale[0])

    tpb = t // tm
    q_x, kt_x, v_x, g_x, gt_x, pool_x, o_x, w_out_b, w_up_b, w_down_b = _in_proj(
        x.reshape(bsz * t, d), t, mod3, lambda i: i // tpb, norm1[0], w_in_b, cols,
        gate_bias[0], conv_qk[0], pool_parts, tm, cast_weights=(w_out[0], w_up[0], w_down[0]))
    q_c, kt_c, v_c, g_c, gt_c = _in_proj(
        ctx.reshape(bsz * t_c, d), t_c, mod3, lambda i: bsz, norm1[0], w_in_b, cols,
        gate_bias[0], conv_qk[0], None, tm)

    def per_seq(a, tt):
        return a.reshape((bsz, tt) + a.shape[1:])

    def per_seq_chunks(a, tt, merge=False):
        nc = tt // SUB
        if merge:
            return a.reshape(bsz, nc * a.shape[1], a.shape[2])
        return a.reshape((bsz, nc) + a.shape[1:])

    ml = _mlstm(per_seq(q_x, t), per_seq_chunks(kt_x, t), per_seq(v_x, t), per_seq(o_x, t),
                g_x.reshape(bsz, t // SUB, SUB, N_GATES), per_seq_chunks(gt_x, t, merge=True),
                per_seq(q_c, t_c), per_seq_chunks(kt_c, t_c), per_seq(v_c, t_c),
                g_c.reshape(bsz, t_c // SUB, SUB, N_GATES), per_seq_chunks(gt_c, t_c, merge=True),
                head_norm[0])

    out = _out_ffn(x.reshape(bsz * t, d), pool_x, ml.reshape(bsz * t, ml_width), mod3,
                   w_out_b, norm2[0], w_up_b, w_down_b, norm_f, OUT_TILE, t // OUT_TILE)
    return out.reshape(bsz, t, d)
```

```python
import functools

import numpy as np
import jax
import jax.numpy as jnp
from jax import lax
from jax.experimental import pallas as pl
from jax.experimental.pallas import tpu as pltpu

F32 = jnp.float32
BF16 = jnp.bfloat16

EPS = 1e-6
GRID_W = 64
POOL_WINDOWS = (2, 4, 8, 16)
N_HEADS = 4
DK = 64
DV = 128
N_GATES = 4 * N_HEADS
CONV_W = 3
LOG2E = 1.4426950408889634

V7X_VMEM_BYTES = 64 * 1024 * 1024
V7X_MXU_DIM = 256
BF16_SUBLANES = 16

SUB = V7X_MXU_DIM
PROJ_TILE = 1024
OUT_TILE = 1024
HALO = BF16_SUBLANES


def _vmem_limit(est_bytes):
    return int(min(V7X_VMEM_BYTES - 6 * 1024 * 1024, est_bytes))


def _dot(a, b):
    return jnp.dot(a, b, preferred_element_type=F32)


def _sigmoid(x):
    return 1.0 / (1.0 + jnp.exp(-x))


def _silu(x):
    return x * _sigmoid(x)


def _log_sigmoid(x):
    return jnp.minimum(x, 0.0) - jnp.log(1.0 + jnp.exp(-jnp.abs(x)))


def _rms(x):
    return x * lax.rsqrt(jnp.mean(x * x, axis=-1, keepdims=True) + EPS)


def _ada_kernel(c_ref, w_ref, b_ref, win_ref, o_ref, winb_ref):
    s = _silu(c_ref[...]).astype(BF16)
    o_ref[...] = _dot(s, w_ref[...].astype(BF16)) + b_ref[...]
    winb_ref[...] = win_ref[...].astype(BF16)


def _ada(cc, w_ada, b_ada, w_in):
    rows, d = cc.shape
    n = w_ada.shape[1]
    bn = 1536
    slab = w_in.shape[0] // (n // bn)
    assert slab * (n // bn) == w_in.shape[0] and slab % BF16_SUBLANES == 0
    return pl.pallas_call(
        _ada_kernel,
        grid=(n // bn,),
        in_specs=[
            pl.BlockSpec((rows, d), lambda j: (0, 0)),
            pl.BlockSpec((d, bn), lambda j: (0, j)),
            pl.BlockSpec((1, bn), lambda j: (0, j)),
            pl.BlockSpec((slab, w_in.shape[1]), lambda j: (j, 0)),
        ],
        out_specs=[pl.BlockSpec((rows, bn), lambda j: (0, j)),
                   pl.BlockSpec((slab, w_in.shape[1]), lambda j: (j, 0))],
        out_shape=[jax.ShapeDtypeStruct((rows, n), F32),
                   jax.ShapeDtypeStruct(w_in.shape, BF16)],
        compiler_params=pltpu.CompilerParams(
            dimension_semantics=("arbitrary",),
            vmem_limit_bytes=_vmem_limit(40 * 1024 * 1024)),
        name="ada",
    )(cc, w_ada, b_ada.reshape(1, n), w_in)


def _pool_consts():
    pos = np.arange(SUB) % GRID_W
    row = np.arange(SUB) // GRID_W
    a = np.zeros((len(POOL_WINDOWS), SUB, SUB), np.float32)
    inv = np.zeros((SUB, len(POOL_WINDOWS)), np.float32)
    for gi, win in enumerate(POOL_WINDOWS):
        lo = np.clip(pos - win // 2, 0, GRID_W - 1)
        hi = np.clip(pos + win // 2 - 1, 0, GRID_W - 1)
        for j in range(SUB):
            a[gi, j, row[j] * GRID_W + lo[j]: row[j] * GRID_W + hi[j] + 1] = 1.0
        inv[:, gi] = 1.0 / (hi - lo + 1)
    return a, inv


def _proj_kernel(*refs, tm, seq_len, cols, with_pool, n_cast):
    n_in = (12 if with_pool else 8) + n_cast
    cast_in = refs[n_in - n_cast:n_in]
    cast_out = refs[len(refs) - 1 - n_cast:len(refs) - 1]
    refs = refs[:n_in - n_cast] + refs[n_in:len(refs) - 1 - n_cast] + refs[-1:]
    for src, dst in zip(cast_in, cast_out):
        dst[...] = src[...].astype(BF16)
    if with_pool:
        (x_ref, xp_ref, xn_ref, mod_ref, n1_ref, w_ref, gb_ref, cw_ref,
         a_ref, inv_ref, pw_ref, ps_ref,
         q_ref, kt_ref, v_ref, g_ref, gt_ref, pool_ref, o_ref, lhs_scr) = refs
    else:
        (x_ref, xp_ref, xn_ref, mod_ref, n1_ref, w_ref, gb_ref, cw_ref,
         q_ref, kt_ref, v_ref, g_ref, gt_ref, lhs_scr) = refs
    d = x_ref.shape[-1]
    sub = SUB
    n_sub = tm // sub
    c_pool, c_qk, c_v, c_o, c_g, c_end = cols
    sh = mod_ref[0, :, 0:d]
    gain = n1_ref[...] * (1.0 + mod_ref[0, :, d:2 * d])
    qw = q_ref.shape[-1]
    row = lax.broadcasted_iota(jnp.int32, (sub, 1), 0)

    def mod_norm(xv):
        return (_rms(xv) * gain + sh).astype(BF16)

    def conv_act(cur, prev_last, next_first, k):
        start = pl.program_id(0) * tm + k * sub
        prev_last = jnp.where(start % seq_len == 0, 0.0, prev_last)
        next_first = jnp.where((start + sub) % seq_len == 0, 0.0, next_first)
        dn = jnp.where(row == 0, prev_last, pltpu.roll(cur, 1, axis=0))
        up = jnp.where(row == sub - 1, next_first, pltpu.roll(cur, sub - 1, axis=0))
        act = _silu(cw_ref[0:1, :] * dn + cw_ref[1:2, :] * cur + cw_ref[2:3, :] * up)
        q_ref[k * sub:(k + 1) * sub, :] = (act[:, :qw] * (DK ** -0.5)).astype(BF16)
        kt_ref[k] = act[:, qw:].T.astype(BF16)

    def side_dots(hb, k):
        rows = slice(k * sub, (k + 1) * sub)
        v_ref[rows, :] = _dot(hb, w_ref[:, c_v:c_o]).astype(BF16)
        g = _dot(hb, w_ref[:, c_g:c_end]) + gb_ref[...]
        g_ref[rows, :] = g
        gt_ref[k] = g.T
        if with_pool:
            o_ref[rows, :] = _dot(hb, w_ref[:, c_o:c_g]).astype(BF16)
            return _dot(hb, w_ref[:, c_pool:c_qk])
        return None

    def pool_mix(u, k):
        gd = u.shape[1] // len(POOL_WINDOWS)
        parts = []
        for gi in range(len(POOL_WINDOWS)):
            ug = u[:, gi * gd:(gi + 1) * gd]
            win = _dot(a_ref[gi], ug.astype(BF16))
            parts.append(win * inv_ref[:, gi:gi + 1] - ug)
        dmat = jnp.concatenate(parts, axis=1).astype(BF16)
        half = dmat.shape[1] // 2
        out = jnp.concatenate(
            [_dot(dmat[:, :half], pw_ref[0]), _dot(dmat[:, half:], pw_ref[1])], axis=1)
        pool_ref[k * sub:(k + 1) * sub, :] = (out * ps_ref[...]).astype(BF16)

    ps, us = [None] * n_sub, [None] * n_sub

    def own_rows(k):
        lo = HALO if k == 0 else 0
        return lo, lo + sub

    def conv(k):
        lo, hi = own_rows(k)
        if k == 0:
            prev_last = ps[0][lo - 1:lo, :]
        else:
            prev_last = ps[k - 1][own_rows(k - 1)[1] - 1:own_rows(k - 1)[1], :]
        if k == n_sub - 1:
            next_first = ps[k][hi:hi + 1, :]
        else:
            next_first = ps[k + 1][own_rows(k + 1)[0]:own_rows(k + 1)[0] + 1, :]
        conv_act(ps[k][lo:hi, :], prev_last, next_first, k)

    for k in range(n_sub):
        r0 = HALO + k * sub
        lhs_scr[r0:r0 + sub, :] = mod_norm(x_ref[k * sub:(k + 1) * sub, :])
        lo, hi = r0, r0 + sub
        if k == 0:
            lhs_scr[0:HALO, :] = mod_norm(xp_ref[...])
            lo = 0
        if k == n_sub - 1:
            lhs_scr[hi:hi + HALO, :] = mod_norm(xn_ref[...])
            hi += HALO
        ps[k] = _dot(lhs_scr[lo:hi, :], w_ref[:, c_qk:c_v])
        if k > 0:
            conv(k - 1)
        us[k] = side_dots(lhs_scr[r0:r0 + sub, :], k)
        if k > 0 and with_pool:
            pool_mix(us[k - 1], k - 1)
    conv(n_sub - 1)
    if with_pool:
        pool_mix(us[n_sub - 1], n_sub - 1)


def _in_proj(x2, seq_len, mod3, mod_row_fn, norm1, w_in, cols, gate_bias, conv_w, pool_parts, tm,
             cast_weights=()):
    n, d = x2.shape
    with_pool = pool_parts is not None
    c_pool, c_qk, c_v, c_o, c_g, c_end = cols
    ng = c_end - c_g
    hq = (c_v - c_qk) // 2
    vw = c_o - c_v
    hpt = tm // HALO
    const = lambda *shape: pl.BlockSpec(shape, lambda i: (0,) * len(shape))
    tok = lambda w: pl.BlockSpec((tm, w), lambda i: (i, 0))
    chunked = lambda r: pl.BlockSpec((tm // SUB, r, SUB), lambda i: (i, 0, 0))
    in_specs = [tok(d),
                pl.BlockSpec((HALO, d), lambda i: (jnp.maximum(i * hpt - 1, 0), 0)),
                pl.BlockSpec((HALO, d), lambda i: (jnp.minimum((i + 1) * hpt, n // HALO - 1), 0)),
                pl.BlockSpec((1, 1, mod3.shape[-1]), lambda i: (mod_row_fn(i), 0, 0)),
                const(1, d), const(*w_in.shape), const(1, ng), const(*conv_w.shape)]
    args = [x2, x2, x2, mod3, norm1.reshape(1, d), w_in, gate_bias.reshape(1, ng), conv_w]
    out_specs = [tok(hq), chunked(hq), tok(vw), tok(ng), chunked(ng)]
    out_shape = [jax.ShapeDtypeStruct((n, hq), BF16),
                 jax.ShapeDtypeStruct((n // SUB, hq, SUB), BF16),
                 jax.ShapeDtypeStruct((n, vw), BF16),
                 jax.ShapeDtypeStruct((n, ng), F32),
                 jax.ShapeDtypeStruct((n // SUB, ng, SUB), F32)]
    if with_pool:
        a_c, inv_c, pw_bd, p_scale = pool_parts
        pool_w = c_qk - c_pool
        in_specs += [const(*a_c.shape), const(*inv_c.shape), const(*pw_bd.shape),
                     const(1, pool_w)]
        args += [a_c, inv_c, pw_bd, p_scale.reshape(1, pool_w)]
        out_specs += [tok(pool_w), tok(c_g - c_o)]
        out_shape += [jax.ShapeDtypeStruct((n, pool_w), BF16),
                      jax.ShapeDtypeStruct((n, c_g - c_o), BF16)]
    steps = n // tm
    for wgt in cast_weights:
        hold = 1
        while (wgt.shape[0] * hold) % (steps * BF16_SUBLANES):
            hold *= 2
        slab_rows = wgt.shape[0] * hold // steps
        slab = pl.BlockSpec((slab_rows, wgt.shape[1]), functools.partial(
            lambda i, h: (i // h, 0), h=hold))
        in_specs.append(slab)
        args.append(wgt)
        out_specs.append(slab)
        out_shape.append(jax.ShapeDtypeStruct(wgt.shape, BF16))
    outs = pl.pallas_call(
        functools.partial(_proj_kernel, tm=tm, seq_len=seq_len, cols=cols, with_pool=with_pool,
                          n_cast=len(cast_weights)),
        grid=(steps,),
        in_specs=in_specs,
        out_specs=out_specs,
        out_shape=out_shape,
        scratch_shapes=[pltpu.VMEM((tm + 2 * HALO, d), BF16)],
        compiler_params=pltpu.CompilerParams(
            dimension_semantics=("arbitrary",),
            vmem_limit_bytes=_vmem_limit(48 * 1024 * 1024)),
        name="in_proj_pool" if with_pool else "in_proj_ctx",
    )(*args)
    return list(outs)


def _split3(x):
    hi = x.astype(BF16)
    r1 = x - hi.astype(F32)
    mid = r1.astype(BF16)
    lo = (r1 - mid.astype(F32)).astype(BF16)
    return hi, mid, lo


def _tri_dot_left(tri, x):
    return sum(_dot(tri, p) for p in _split3(x))


def _tri_dot_right(x, tri):
    return sum(_dot(p, tri) for p in _split3(x))


def _gate_tables(g_col, g_row, tril, triu):
    lf_row = _log_sigmoid(g_row) * LOG2E
    lf_col = _log_sigmoid(g_col) * LOG2E
    brow = (_tri_dot_right(lf_row, triu), _tri_dot_right(lf_row, tril))
    bcol = (_tri_dot_left(tril, lf_col), _tri_dot_left(triu, lf_col))
    return g_row * LOG2E, brow, bcol


def _chain_step(q_c, kt_c, v_aug, bcol, arow, mask, last, ct_ref, m0, m_last, want_out):
    lc = kt_c.shape[1]
    ct = ct_ref[...]
    if want_out:
        am = jnp.where(mask, arow, -jnp.inf)
        cm = jnp.max(am, axis=1, keepdims=True)
        mcol = jnp.maximum(m0, jnp.broadcast_to(cm, (lc, DV)))
        b_r = jnp.broadcast_to(bcol, (lc, DV))
        e = jnp.exp2(am - jnp.concatenate([mcol, mcol], axis=1))
        s = _dot(q_c, kt_c) * e
        den_intra = jnp.broadcast_to(jnp.sum(s, axis=1, keepdims=True), (lc, DV))
        p = s.astype(BF16)
        wk = e[last:last + 1, :]
        qs = (q_c.astype(F32) * jnp.exp2(m0 - mcol)[:, :DK]).astype(BF16)
    else:
        wk = jnp.exp2(arow - m_last)
    kw = (kt_c.astype(F32) * wk).astype(BF16)
    if want_out:
        r = _dot(jnp.concatenate([p, kw], axis=0), v_aug)
        r2 = _dot(qs, ct.astype(BF16))
        num = r[:lc, :DV] + r2[:, :DV]
        den = den_intra + r2[:, DV:]
        hout = num / jnp.maximum(jnp.abs(den), jnp.exp2(-(b_r + mcol)))
        d_ct = r[lc:, :]
    else:
        hout = None
        d_ct = _dot(kw, v_aug)
    ct_ref[...] = jnp.exp2(m0 - m_last) * ct + d_ct
    return hout


def _mlstm_kernel(qx_ref, ktx_ref, vx_ref, ox_ref, gx_ref, gtx_ref,
                  qc_ref, ktc_ref, vc_ref, gc_ref, gtc_ref, hn_ref, out_ref,
                  hf_scr, hb_scr, st_scr, gpx_scr, gpc_scr, *, t_x, t_c):
    lc = SUB
    nh = N_HEADS
    ng = N_GATES
    ri = lax.broadcasted_iota(jnp.int32, (lc, lc), 0)
    ci = lax.broadcasted_iota(jnp.int32, (lc, lc), 1)
    lower = ci <= ri
    upper = ci >= ri
    tril = lower.astype(BF16)
    triu = upper.astype(BF16)
    ones_blk = jnp.ones((lc, DV), BF16)
    dirs = ((lower, lc - 1), (upper, 0))

    st_scr[...] = jnp.zeros(st_scr.shape, F32)

    def tables(g_ref, gt_ref, gp_scr):
        for c in range(g_ref.shape[1]):
            gp_scr[:, c * ng:(c + 1) * ng] = g_ref[0, c]
        return _gate_tables(gp_scr[...], gt_ref[0], tril, triu)

    def gate_slices(tabs, cidx, di, h):
        g_row, brow, bcol = tabs
        col_i = cidx * ng + di * nh + h
        col_f = col_i + 2 * nh
        arow = g_row[col_i:col_i + 1, :] - brow[di][col_f:col_f + 1, :]
        return arow, bcol[di][:, col_f:col_f + 1]

    def stabilisers(order, tabs, m_run):
        plan = []
        for i in range(len(order[0])):
            step = []
            for di in range(2):
                _, last = dirs[di]
                for h in range(nh):
                    arow, bcol = gate_slices(tabs, order[di][i], di, h)
                    m0 = m_run[di * nh + h]
                    m_last = jnp.maximum(m0, jnp.max(arow, axis=1, keepdims=True))
                    step.append((m0, m_last))
                    m_run[di * nh + h] = bcol[last:last + 1, :] + m_last
            plan.append(step)
        return plan

    def chunk_pair(cf, cb, q_ref, kt_ref, v_ref, tabs, ms, want_out):
        for di, cidx in enumerate((cf, cb)):
            mask, last = dirs[di]
            r0 = cidx * lc
            for h in range(nh):
                arow, bcol = gate_slices(tabs, cidx, di, h)
                q_c = q_ref[0, r0:r0 + lc, h * DK:(h + 1) * DK]
                kt_c = kt_ref[0, cidx, h * DK:(h + 1) * DK, :]
                v_c = v_ref[0, r0:r0 + lc, h * DV:(h + 1) * DV]
                v_aug = jnp.concatenate([v_c, ones_blk], axis=1)
                m0, m_last = ms[di * nh + h]
                hout = _chain_step(q_c, kt_c, v_aug, bcol, arow, mask, last,
                                   st_scr.at[di * nh + h], m0, m_last, want_out)
                if want_out:
                    dst = hf_scr if di == 0 else hb_scr
                    dst[r0:r0 + lc, h * DV:(h + 1) * DV] = hout

    m_run = [jnp.zeros((1, 1), F32) for _ in range(2 * nh)]
    nc_c = t_c // lc
    tabs_c = tables(gc_ref, gtc_ref, gpc_scr)
    order_c = (list(range(nc_c)), list(range(nc_c - 1, -1, -1)))
    plan_c = stabilisers(order_c, tabs_c, m_run)
    nc_x = t_x // lc
    tabs_x = tables(gx_ref, gtx_ref, gpx_scr)
    order_x = (list(range(nc_x)), list(range(nc_x - 1, -1, -1)))
    plan_x = stabilisers(order_x, tabs_x, m_run)
    for i in range(nc_c):
        chunk_pair(order_c[0][i], order_c[1][i], qc_ref, ktc_ref, vc_ref, tabs_c, plan_c[i],
                   False)
    for i in range(nc_x):
        pl.when(pl.program_id(0) >= 0)(functools.partial(
            chunk_pair, order_x[0][i], order_x[1][i], qx_ref, ktx_ref, vx_ref, tabs_x,
            plan_x[i], True))

    for r0 in range(0, t_x, lc):
        hm = hf_scr[r0:r0 + lc, :] + hb_scr[r0:r0 + lc, :]
        parts = [_rms(hm[:, h * DV:(h + 1) * DV]) for h in range(nh)]
        hnorm = jnp.concatenate(parts, axis=1)
        gate = _sigmoid(ox_ref[0, r0:r0 + lc, :].astype(F32))
        out_ref[0, r0:r0 + lc, :] = (hnorm * hn_ref[...] * gate).astype(BF16)


def _mlstm(qx, ktx, vx, ox, gx, gtx, qc, ktc, vc, gc, gtc, head_norm):
    bsz, t_x, _ = qx.shape
    t_c = qc.shape[1]
    vw = vx.shape[2]
    ng = gx.shape[-1]
    bspec = lambda a: pl.BlockSpec((1,) + a.shape[1:], lambda b: (b,) + (0,) * (a.ndim - 1))
    args = (qx, ktx, vx, ox, gx, gtx, qc, ktc, vc, gc, gtc)
    return pl.pallas_call(
        functools.partial(_mlstm_kernel, t_x=t_x, t_c=t_c),
        grid=(bsz,),
        in_specs=[bspec(a) for a in args] + [pl.BlockSpec((1, vw), lambda b: (0, 0))],
        out_specs=pl.BlockSpec((1, t_x, vw), lambda b: (b, 0, 0)),
        out_shape=jax.ShapeDtypeStruct((bsz, t_x, vw), BF16),
        scratch_shapes=[
            pltpu.VMEM((t_x, vw), F32),
            pltpu.VMEM((t_x, vw), F32),
            pltpu.VMEM((2 * N_HEADS, DK, 2 * DV), F32),
            pltpu.VMEM((SUB, (t_x // SUB) * ng), F32),
            pltpu.VMEM((SUB, (t_c // SUB) * ng), F32),
        ],
        compiler_params=pltpu.CompilerParams(
            dimension_semantics=("arbitrary",),
            vmem_limit_bytes=_vmem_limit(58 * 1024 * 1024)),
        name="mlstm",
    )(*args, head_norm.reshape(1, vw))


def _staggered(n_items, stages):
    state = [None] * n_items
    for step in range(len(stages) + n_items - 1):
        for i in range(n_items):
            k = step - i
            if 0 <= k < len(stages):
                state[i] = stages[k](i, state[i])


def _out_ffn_kernel(x_ref, pool_ref, ml_ref, mod_ref, wo_ref, n2_ref, wu_ref,
                    wd_ref, nf_ref, out_ref):
    d = x_ref.shape[-1]
    tm = x_ref.shape[0]
    dff = wd_ref.shape[0]
    g1 = mod_ref[0, :, 2 * d:3 * d]
    sh2 = mod_ref[0, :, 3 * d:4 * d]
    gain2 = n2_ref[...] * (1.0 + mod_ref[0, :, 4 * d:5 * d])
    g2 = mod_ref[0, :, 5 * d:6 * d]
    pw = pool_ref.shape[-1]
    rows = [slice(r0, r0 + SUB) for r0 in range(0, tm, SUB)]

    def out_proj(i, _):
        r = rows[i]
        return _dot(pool_ref[r, :], wo_ref[0:pw, :]) + _dot(ml_ref[r, :], wo_ref[pw:, :])

    def norm2(i, mix):
        x1 = x_ref[rows[i], :] + g1 * mix
        return x1, (_rms(x1) * gain2 + sh2).astype(BF16)

    def up(i, st):
        x1, h2 = st
        return x1, _dot(h2, wu_ref[:, 0:dff]), _dot(h2, wu_ref[:, dff:2 * dff])

    def act(i, st):
        x1, gg, aa = st
        return x1, (_silu(gg) * aa).astype(BF16)

    def down(i, st):
        x1, a = st
        return x1, _dot(a, wd_ref[...])

    def final(i, st):
        x1, acc = st
        x2 = x1 + g2 * acc
        out_ref[rows[i], :] = _rms(x2) * nf_ref[...]

    _staggered(len(rows), [out_proj, norm2, up, act, down, final])


def _out_ffn(x2d, pool, ml, mod3, w_out, norm2, w_up, w_down, norm_f, tm, tpb):
    n, d = x2d.shape
    tok = lambda w: pl.BlockSpec((tm, w), lambda i: (i, 0))
    const = lambda *shape: pl.BlockSpec(shape, lambda i: (0,) * len(shape),
                                        pipeline_mode=pl.Buffered(1))
    return pl.pallas_call(
        _out_ffn_kernel,
        grid=(n // tm,),
        in_specs=[tok(d), tok(pool.shape[1]), tok(ml.shape[1]),
                  pl.BlockSpec((1, 1, mod3.shape[-1]), lambda i: (i // tpb, 0, 0)),
                  const(*w_out.shape), const(1, d), const(*w_up.shape),
                  const(*w_down.shape), const(1, d)],
        out_specs=tok(d),
        out_shape=jax.ShapeDtypeStruct((n, d), F32),
        compiler_params=pltpu.CompilerParams(
            dimension_semantics=("arbitrary",),
            vmem_limit_bytes=_vmem_limit(56 * 1024 * 1024)),
        name="out_ffn",
    )(x2d, pool, ml, mod3, w_out, norm2.reshape(1, d), w_up, w_down, norm_f.reshape(1, d))


def kernel(x, c, ctx, c_ctx, w_ada, b_ada, norm1, w_in, conv_qk, gate_bias, pool_w,
           pool_scale, head_norm, w_out, norm2, w_up, w_down, norm_f):
    bsz, t, d = x.shape
    t_c = ctx.shape[1]
    assert w_ada.shape[0] == 1, "single-layer block"
    pool_width = pool_w.shape[1] * pool_w.shape[2]
    qk_width = 2 * N_HEADS * DK
    ml_width = N_HEADS * DV
    o1 = pool_width
    o2 = o1 + qk_width
    o3 = o2 + ml_width
    o4 = o3 + ml_width
    cols = (0, o1, o2, o3, o4, o4 + N_GATES)
    tm = PROJ_TILE
    assert t % tm == 0 and (bsz * t_c) % tm == 0 and t_c % SUB == 0 and t % OUT_TILE == 0
    assert w_in.shape[2] == cols[-1] and w_up.shape[2] == 2 * w_down.shape[1]

    cc = jnp.concatenate([c, c_ctx[None, :]], axis=0)
    mod, w_in_b = _ada(cc, w_ada[0], b_ada[0], w_in[0])
    mod3 = mod.reshape(bsz + 1, 1, mod.shape[-1])

    a_np, inv_np = _pool_consts()
    gd = pool_w.shape[2]
    pw = pool_w[0].astype(BF16)
    zero = jnp.zeros((gd, gd), BF16)
    pw_bd = jnp.stack([jnp.block([[pw[0], zero], [zero, pw[1]]]),
                       jnp.block([[pw[2], zero], [zero, pw[3]]])])
    pool_parts = (jnp.asarray(a_np, BF16), jnp.asarray(inv_np, F32), pw_bd, pool_scale[0])

    tpb = t // tm
    q_x, kt_x, v_x, g_x, gt_x, pool_x, o_x, w_out_b, w_up_b, w_down_b = _in_proj(
        x.reshape(bsz * t, d), t, mod3, lambda i: i // tpb, norm1[0], w_in_b, cols,
        gate_bias[0], conv_qk[0], pool_parts, tm, cast_weights=(w_out[0], w_up[0], w_down[0]))
    q_c, kt_c, v_c, g_c, gt_c = _in_proj(
        ctx.reshape(bsz * t_c, d), t_c, mod3, lambda i: bsz, norm1[0], w_in_b, cols,
        gate_bias[0], conv_qk[0], None, tm)

    def per_seq(a, tt):
        return a.reshape((bsz, tt) + a.shape[1:])

    def per_seq_chunks(a, tt, merge=False):
        nc = tt // SUB
        if merge:
            return a.reshape(bsz, nc * a.shape[1], a.shape[2])
        return a.reshape((bsz, nc) + a.shape[1:])

    ml = _mlstm(per_seq(q_x, t), per_seq_chunks(kt_x, t), per_seq(v_x, t), per_seq(o_x, t),
                g_x.reshape(bsz, t // SUB, SUB, N_GATES), per_seq_chunks(gt_x, t, merge=True),
                per_seq(q_c, t_c), per_seq_chunks(kt_c, t_c), per_seq(v_c, t_c),
                g_c.reshape(bsz, t_c // SUB, SUB, N_GATES), per_seq_chunks(gt_c, t_c, merge=True),
                head_norm[0])

    out = _out_ffn(x.reshape(bsz * t, d), pool_x, ml.reshape(bsz * t, ml_width), mod3,
                   w_out_b, norm2[0], w_up_b, w_down_b, norm_f, OUT_TILE, t // OUT_TILE)
    return out.reshape(bsz, t, d)
```
